```python
import math
import jax
import jax.numpy as jnp
from jax import lax
import numpy as np

D_MODEL = 4096
BATCH = 4
SEQ = 2048
DEPTH = 1
DEC_BATCH = 128
DEC_SEQ = 1
PAST_LEN = 16384
PAGE_SIZE = 128

N_META = 16
D_MIX = D_MODEL
D_SSM = D_MIX // 2
SSM_GROUP = 16
N_SSM_GROUPS = D_SSM // SSM_GROUP
SSM_STATE = 64
DT_MIN = 1e-3
DT_MAX = 1e-1
D_GLA = D_MIX - D_SSM
GLA_HEADS = 4
GLA_DK = D_GLA // 2 // GLA_HEADS
GLA_DV = D_GLA // GLA_HEADS
GLA_QK = GLA_HEADS * GLA_DK
GLA_GATE_RANK = 16
GLA_TAU = 16.0
GLA_CHUNK = 64
SPLIT_POINTS = (D_SSM, D_SSM + GLA_QK, D_SSM + 2 * GLA_QK, D_SSM + 2 * GLA_QK + D_GLA, D_SSM + 2 * GLA_QK + 2 * D_GLA)
D_IN = D_SSM + 2 * GLA_QK + 2 * D_GLA + GLA_GATE_RANK
PEER_HEADS = 8
PEER_NKEYS = 128
PEER_N = PEER_NKEYS * PEER_NKEYS
PEER_QDIM = 256
PEER_HALF = PEER_QDIM // 2
PEER_TOPK = 16
PEER_BLOCK = 64
EPS = 1e-6

kernel_name = 'hymba_s5_gla_peer_step'


def _rmsnorm(x, g):
    xf = x.astype(jnp.float32)
    y = xf * lax.rsqrt(jnp.mean(xf * xf, axis=-1, keepdims=True) + EPS)
    return (y * g.astype(jnp.float32)).astype(x.dtype)


def _complex_scan_op(e1, e2):
    a1r, a1i, b1r, b1i = e1
    a2r, a2i, b2r, b2i = e2
    return (a2r * a1r - a2i * a1i,
            a2r * a1i + a2i * a1r,
            a2r * b1r - a2i * b1i + b2r,
            a2r * b1i + a2i * b1r + b2i)


def _s5_mix(u, h0_re, h0_im, p):
    f32 = jnp.float32
    B, L, _ = u.shape
    uf = u.astype(f32)
    ug = uf.reshape(B, L, N_SSM_GROUPS, SSM_GROUP)
    lr = p['s5_lam_re'].astype(f32)
    li = p['s5_lam_im'].astype(f32)
    dt = jnp.exp(p['s5_log_dt'].astype(f32))[:, None]
    mag = jnp.exp(lr * dt)
    ar = mag * jnp.cos(li * dt)
    ai = mag * jnp.sin(li * dt)
    den = lr * lr + li * li
    nr = ar - 1.0
    qr = (nr * lr + ai * li) / den
    qi = (ai * lr - nr * li) / den
    b_re = p['s5_b_re'].astype(f32)
    b_im = p['s5_b_im'].astype(f32)
    bbr = qr[..., None] * b_re - qi[..., None] * b_im
    bbi = qr[..., None] * b_im + qi[..., None] * b_re
    xr = jnp.einsum('blgh,gph->blgp', ug, bbr)
    xi = jnp.einsum('blgh,gph->blgp', ug, bbi)
    h0r = h0_re.astype(f32)
    h0i = h0_im.astype(f32)
    xr = xr.at[:, 0].add(ar * h0r - ai * h0i)
    xi = xi.at[:, 0].add(ar * h0i + ai * h0r)
    Ar = jnp.broadcast_to(ar, xr.shape)
    Ai = jnp.broadcast_to(ai, xr.shape)
    _, _, hr, hi = lax.associative_scan(_complex_scan_op, (Ar, Ai, xr, xi), axis=1)
    y = (jnp.einsum('blgp,ghp->blgh', hr, p['s5_c_re'].astype(f32))
         - jnp.einsum('blgp,ghp->blgh', hi, p['s5_c_im'].astype(f32)))
    y = y.reshape(B, L, D_SSM) + p['s5_d'].astype(f32) * uf
    z = jax.nn.gelu(y)
    z = z * jax.nn.sigmoid(z @ p['s5_w_glu'].astype(f32) + p['s5_b_glu'].astype(f32))
    return z, hr[:, -1], hi[:, -1]


def _gla_chunk(S, q, k, v, lg):
    C = q.shape[1]
    b = jnp.cumsum(lg, axis=1)
    o = jnp.einsum('bihk,bhkv->bihv', q * jnp.exp(b), S)
    causal = (jnp.arange(C)[:, None] >= jnp.arange(C)[None, :])[None, :, :, None, None]
    diff = b[:, :, None] - b[:, None, :]
    decay = jnp.where(causal, jnp.exp(jnp.where(causal, diff, 0.0)), 0.0)
    scores = jnp.einsum('bihk,bjhk,bijhk->bijh', q, k, decay)
    o = o + jnp.einsum('bijh,bjhv->bihv', scores, v)
    bl = b[:, -1]
    S_new = jnp.exp(bl)[..., None] * S + jnp.einsum('bjhk,bjhv->bhkv', k * jnp.exp(bl[:, None] - b), v)
    return S_new, o


def _gla_blocks(S0, q, k, v, lg, chunk):
    B, L = q.shape[0], q.shape[1]
    n = L // chunk

    def to_blocks(t):
        return jnp.moveaxis(t.reshape((B, n, chunk) + t.shape[2:]), 1, 0)

    def step(S, inp):
        qc, kc, vc, gc = inp
        return _gla_chunk(S, qc, kc, vc, gc)

    S, o = lax.scan(step, S0, (to_blocks(q), to_blocks(k), to_blocks(v), to_blocks(lg)))
    return S, jnp.moveaxis(o, 0, 1).reshape((B, L) + o.shape[3:])


def _chunk_len(n):
    return GLA_CHUNK if n % GLA_CHUNK == 0 else n


def _peer(xn, w_q, keys, u_tab, v_tab):
    f32 = jnp.float32
    B, L, D = xn.shape
    T = B * L
    nblk = -(-T // PEER_BLOCK)
    xf = jnp.pad(xn.reshape(T, D), ((0, nblk * PEER_BLOCK - T), (0, 0)))
    k1 = keys[:, 0].astype(f32)
    k2 = keys[:, 1].astype(f32)

    def block(xb):
        nt = xb.shape[0]
        q = (xb @ w_q).astype(f32).reshape(nt, PEER_HEADS, 2, PEER_HALF)
        s1 = jnp.einsum('thc,hnc->thn', q[:, :, 0], k1)
        s2 = jnp.einsum('thc,hnc->thn', q[:, :, 1], k2)
        v1, i1 = lax.top_k(s1, PEER_TOPK)
        v2, i2 = lax.top_k(s2, PEER_TOPK)
        cand = (v1[..., :, None] + v2[..., None, :]).reshape(nt, PEER_HEADS, PEER_TOPK * PEER_TOPK)
        cid = (i1[..., :, None] * PEER_NKEYS + i2[..., None, :]).reshape(nt, PEER_HEADS, PEER_TOPK * PEER_TOPK)
        sc, sel = lax.top_k(cand, PEER_TOPK)
        eid = jnp.take_along_axis(cid, sel, axis=-1)
        g = jax.nn.softmax(sc, axis=-1)
        act = jax.nn.gelu(jnp.einsum('td,thkd->thk', xb, u_tab[eid]).astype(f32))
        return jnp.einsum('thk,thkd->td', (g * act).astype(xb.dtype), v_tab[eid])

    y = lax.map(block, xf.reshape(nblk, PEER_BLOCK, D))
    return y.reshape(nblk * PEER_BLOCK, D)[:T].reshape(B, L, D)


def _hybrid_layer(x, h0_re, h0_im, S0, n_lead, p):
    f32 = jnp.float32
    B, L, _ = x.shape
    xn = _rmsnorm(x, p['norm_mix_g'])
    proj = xn @ p['w_in']
    u, xq, xk, xv, xr, xg = jnp.split(proj, SPLIT_POINTS, axis=-1)
    y_ssm, hT_re, hT_im = _s5_mix(u, h0_re, h0_im, p)
    y_ssm = _rmsnorm(y_ssm, p['s5_norm_g'])
    lg = jax.nn.log_sigmoid((xg @ p['gla_w_gate2'] + p['gla_b_gate2']).astype(f32)) / GLA_TAU
    lg = lg.reshape(B, L, GLA_HEADS, GLA_DK)
    q = xq.astype(f32).reshape(B, L, GLA_HEADS, GLA_DK) * (GLA_DK ** -0.5)
    k = xk.astype(f32).reshape(B, L, GLA_HEADS, GLA_DK)
    v = xv.astype(f32).reshape(B, L, GLA_HEADS, GLA_DV)
    S0 = S0.astype(f32)
    if n_lead > 0:
        S_mid, o_lead = _gla_chunk(S0, q[:, :n_lead], k[:, :n_lead], v[:, :n_lead], lg[:, :n_lead])
        S_T, o_rest = _gla_blocks(S_mid, q[:, n_lead:], k[:, n_lead:], v[:, n_lead:], lg[:, n_lead:], _chunk_len(L - n_lead))
        o = jnp.concatenate([o_lead, o_rest], axis=1)
    else:
        S_T, o = _gla_blocks(S0, q, k, v, lg, _chunk_len(L))
    o = o * lax.rsqrt(jnp.mean(o * o, axis=-1, keepdims=True) + EPS)
    o = o.reshape(B, L, D_GLA) * p['gla_norm_g'].astype(f32) * jax.nn.silu(xr.astype(f32))
    mix = jnp.concatenate([y_ssm.astype(x.dtype), o.astype(x.dtype)], axis=-1) @ p['w_out']
    h = x + mix.astype(x.dtype)
    h = h + _peer(_rmsnorm(h, p['norm_ffn_g']), p['peer_w_q'], p['peer_keys'], p['peer_u'], p['peer_v']).astype(x.dtype)
    return h, hT_re, hT_im, S_T


def setup_inputs(seed: int = 0) -> dict:
    key = jax.random.key(seed)
    ks = jax.random.split(key, 32)
    nrm = jax.random.normal
    f32 = jnp.float32
    G, P, Hs = N_SSM_GROUPS, SSM_STATE, SSM_GROUP
    lam_im0 = jnp.pi * jnp.arange(P, dtype=f32)
    return {
        'x_prompt': nrm(ks[0], (BATCH, SEQ, D_MODEL), f32),
        'x_sample': nrm(ks[1], (DEC_BATCH, DEC_SEQ, D_MODEL), f32),
        'state_s5_re': 0.5 * nrm(ks[2], (DEPTH, DEC_BATCH, G, P), f32),
        'state_s5_im': 0.5 * nrm(ks[3], (DEPTH, DEC_BATCH, G, P), f32),
        'state_gla': 0.5 * nrm(ks[4], (DEPTH, DEC_BATCH, GLA_HEADS, GLA_DK, GLA_DV), f32),
        'meta_tokens': nrm(ks[5], (N_META, D_MODEL), f32),
        'norm_mix_g': 1.0 + 0.02 * nrm(ks[6], (DEPTH, D_MODEL), f32),
        'w_in': nrm(ks[7], (DEPTH, D_MODEL, D_IN), f32) * D_MODEL ** -0.5,
        's5_lam_re': -0.5 + 0.01 * nrm(ks[8], (DEPTH, G, P), f32),
        's5_lam_im': lam_im0 + 0.01 * nrm(ks[9], (DEPTH, G, P), f32),
        's5_log_dt': jax.random.uniform(ks[10], (DEPTH, G), f32, math.log(DT_MIN), math.log(DT_MAX)),
        's5_b_re': nrm(ks[11], (DEPTH, G, P, Hs), f32) * (2 * Hs) ** -0.5,
        's5_b_im': nrm(ks[12], (DEPTH, G, P, Hs), f32) * (2 * Hs) ** -0.5,
        's5_c_re': nrm(ks[13], (DEPTH, G, Hs, P), f32) * (2 * P) ** -0.5,
        's5_c_im': nrm(ks[14], (DEPTH, G, Hs, P), f32) * (2 * P) ** -0.5,
        's5_d': nrm(ks[15], (DEPTH, D_SSM), f32),
        's5_w_glu': nrm(ks[16], (DEPTH, D_SSM, D_SSM), f32) * D_SSM ** -0.5,
        's5_b_glu': 0.01 * nrm(ks[17], (DEPTH, D_SSM), f32),
        's5_norm_g': 1.0 + 0.02 * nrm(ks[18], (DEPTH, D_SSM), f32),
        'gla_w_gate2': nrm(ks[19], (DEPTH, GLA_GATE_RANK, GLA_QK), f32) * GLA_GATE_RANK ** -0.5,
        'gla_b_gate2': 0.5 * nrm(ks[20], (DEPTH, GLA_QK), f32),
        'gla_norm_g': 1.0 + 0.02 * nrm(ks[21], (DEPTH, D_GLA), f32),
        'w_out': nrm(ks[22], (DEPTH, D_MIX, D_MODEL), f32) * D_MIX ** -0.5,
        'norm_ffn_g': 1.0 + 0.02 * nrm(ks[23], (DEPTH, D_MODEL), f32),
        'peer_w_q': nrm(ks[24], (DEPTH, D_MODEL, PEER_HEADS * PEER_QDIM), f32) * D_MODEL ** -0.5,
        'peer_keys': nrm(ks[25], (DEPTH, PEER_HEADS, 2, PEER_NKEYS, PEER_HALF), f32) * PEER_HALF ** -0.5,
        'peer_u': nrm(ks[26], (DEPTH, PEER_N, D_MODEL), f32) * D_MODEL ** -0.5,
        'peer_v': 0.3 * nrm(ks[27], (DEPTH, PEER_N, D_MODEL), f32),
        'norm_final_g': 1.0 + 0.02 * nrm(ks[28], (D_MODEL,), f32),
    }


def reference(x_prompt, x_sample, state_s5_re, state_s5_im, state_gla, meta_tokens,
              norm_mix_g, w_in, s5_lam_re, s5_lam_im, s5_log_dt, s5_b_re, s5_b_im,
              s5_c_re, s5_c_im, s5_d, s5_w_glu, s5_b_glu, s5_norm_g,
              gla_w_gate2, gla_b_gate2, gla_norm_g, w_out, norm_ffn_g,
              peer_w_q, peer_keys, peer_u, peer_v, norm_final_g):
    f32 = jnp.float32
    B = x_prompt.shape[0]
    meta = jnp.broadcast_to(meta_tokens[None].astype(x_prompt.dtype), (B, N_META, D_MODEL))
    hp = jnp.concatenate([meta, x_prompt], axis=1)
    hs = x_sample
    zero_s5 = jnp.zeros((B, N_SSM_GROUPS, SSM_STATE), f32)
    zero_gla = jnp.zeros((B, GLA_HEADS, GLA_DK, GLA_DV), f32)
    sp_re, sp_im, sp_gla, ss_re, ss_im, ss_gla = [], [], [], [], [], []
    for l in range(DEPTH):
        p = {
            'norm_mix_g': norm_mix_g[l], 'w_in': w_in[l],
            's5_lam_re': s5_lam_re[l], 's5_lam_im': s5_lam_im[l], 's5_log_dt': s5_log_dt[l],
            's5_b_re': s5_b_re[l], 's5_b_im': s5_b_im[l], 's5_c_re': s5_c_re[l], 's5_c_im': s5_c_im[l],
            's5_d': s5_d[l], 's5_w_glu': s5_w_glu[l], 's5_b_glu': s5_b_glu[l], 's5_norm_g': s5_norm_g[l],
            'gla_w_gate2': gla_w_gate2[l], 'gla_b_gate2': gla_b_gate2[l], 'gla_norm_g': gla_norm_g[l],
            'w_out': w_out[l], 'norm_ffn_g': norm_ffn_g[l],
            'peer_w_q': peer_w_q[l], 'peer_keys': peer_keys[l], 'peer_u': peer_u[l], 'peer_v': peer_v[l],
        }
        hp, pr, pi_, pg = _hybrid_layer(hp, zero_s5, zero_s5, zero_gla, N_META, p)
        hs, sr, si, sg = _hybrid_layer(hs, state_s5_re[l], state_s5_im[l], state_gla[l], 0, p)
        sp_re.append(pr)
        sp_im.append(pi_)
        sp_gla.append(pg)
        ss_re.append(sr)
        ss_im.append(si)
        ss_gla.append(sg)
    y_prompt = _rmsnorm(hp, norm_final_g)[:, N_META:]
    y_sample = _rmsnorm(hs, norm_final_g)
    return (y_prompt, y_sample,
            jnp.stack(sp_re, 0), jnp.stack(sp_im, 0), jnp.stack(sp_gla, 0),
            jnp.stack(ss_re, 0), jnp.stack(ss_im, 0), jnp.stack(ss_gla, 0))
```

```python
import functools

import jax
import jax.numpy as jnp
from jax import lax
from jax.experimental import pallas as pl
from jax.experimental.pallas import tpu as pltpu

F32 = jnp.float32
BF16 = jnp.bfloat16

EPS = 1e-6
GLA_TAU = 16.0
GLA_CHUNK = 64
GLA_SUB = 16
PEER_TOPK = 16
LANES = 128
SUBLANES = 8
TAIL_ROWS = 256
VMEM_LIMIT = 56 * 1024 * 1024


def _cparams(*sem):
    return pltpu.CompilerParams(dimension_semantics=sem, vmem_limit_bytes=VMEM_LIMIT)


def _gelu(x):
    return 0.5 * x * (1.0 + jnp.tanh(0.7978845608028654 * (x + 0.044715 * (x * x * x))))


def _sigmoid(x):
    return 1.0 / (1.0 + jnp.exp(-x))


def _log_sigmoid(x):
    return jnp.minimum(x, 0.0) - jnp.log(1.0 + jnp.exp(-jnp.abs(x)))


def _row_tile(n, cands):
    for c in cands:
        if n % c == 0:
            return c
    raise ValueError(f"no row tile for {n}")


def _norm_matmul_kernel(x_ref, g_ref, w_ref, wg_ref, o_ref, og_ref, xn_ref):
    @pl.when(pl.program_id(1) == 0)
    def _():
        x = x_ref[...]
        s = lax.rsqrt(jnp.mean(x * x, axis=-1, keepdims=True) + EPS)
        xn = (x * s * g_ref[...]).astype(BF16)
        xn_ref[...] = xn
        og_ref[...] = jnp.dot(xn, wg_ref[...], preferred_element_type=F32)

    o_ref[...] = jnp.dot(xn_ref[...], w_ref[...], preferred_element_type=F32)


def _norm_matmul(x, g, w, wg, tm, tn):
    t, d = x.shape
    n = w.shape[1]
    return pl.pallas_call(
        _norm_matmul_kernel,
        grid=(t // tm, n // tn),
        in_specs=[
            pl.BlockSpec((tm, d), lambda i, j: (i, 0)),
            pl.BlockSpec((1, d), lambda i, j: (0, 0)),
            pl.BlockSpec((d, tn), lambda i, j: (0, j)),
            pl.BlockSpec((d, LANES), lambda i, j: (0, 0)),
        ],
        out_specs=[
            pl.BlockSpec((tm, tn), lambda i, j: (i, j)),
            pl.BlockSpec((tm, LANES), lambda i, j: (i, 0)),
        ],
        out_shape=[jax.ShapeDtypeStruct((t, n), F32), jax.ShapeDtypeStruct((t, LANES), F32)],
        scratch_shapes=[pltpu.VMEM((tm, d), BF16)],
        compiler_params=_cparams("parallel", "arbitrary"),
        name="norm_matmul_in",
    )(x, g, w, wg)


def _s5_kernel(u_ref, h0r_ref, h0i_ref, lr_ref, li_ref, ldt_ref, bre_ref, bim_ref,
               cre_ref, cim_ref, d_ref,
               y_ref, pr_ref, pi_ref, sr_ref, si_ref,
               x_scr, init_scr, *, nb, seq, ns, nm):
    sw = lr_ref.shape[1]
    tp = nb * seq
    nseg = SUBLANES
    ls = seq // nseg

    lr = lr_ref[...]
    li = li_ref[...]
    dt = jnp.exp(ldt_ref[...])
    mag = jnp.exp(lr * dt)
    ang = li * dt
    ar = mag * jnp.cos(ang)
    ai = mag * jnp.sin(ang)
    den = lr * lr + li * li
    nr = ar - 1.0
    qr = (nr * lr + ai * li) / den
    qi = (ai * lr - nr * li) / den
    bre = bre_ref[...]
    bim = bim_ref[...]
    bcat = jnp.concatenate([qr * bre - qi * bim, qr * bim + qi * bre], axis=1).astype(BF16)
    ccat = jnp.concatenate([cre_ref[...], -cim_ref[...]], axis=0).astype(BF16)
    dvec = d_ref[...]
    mag_s = jnp.exp(lr * dt * float(ls))
    asr = mag_s * jnp.cos(ang * float(ls))
    asi = mag_s * jnp.sin(ang * float(ls))

    def project_out(h, u):
        return jnp.dot(h.astype(BF16), ccat, preferred_element_type=F32) + dvec * u

    u_s = u_ref[tp:tp + ns, :]
    x_s = jnp.dot(u_s.astype(BF16), bcat, preferred_element_type=F32)
    h0r = h0r_ref[...]
    h0i = h0i_ref[...]
    hr_s = ar * h0r - ai * h0i + x_s[:, :sw]
    hi_s = ar * h0i + ai * h0r + x_s[:, sw:]
    sr_ref[...] = hr_s
    si_ref[...] = hi_s
    y_ref[tp:tp + ns, :] = project_out(jnp.concatenate([hr_s, hi_s], axis=1), u_s)
    y_ref[tp + ns:, :] = jnp.zeros((y_ref.shape[0] - tp - ns, y_ref.shape[1]), F32)

    u_m = u_ref[tp + ns:tp + ns + nm, :]
    x_m = jnp.dot(u_m.astype(BF16), bcat, preferred_element_type=F32)
    mr = jnp.zeros((1, sw), F32)
    mi = jnp.zeros((1, sw), F32)
    for t in range(nm):
        mr, mi = (ar * mr - ai * mi + x_m[t:t + 1, :sw],
                  ar * mi + ai * mr + x_m[t:t + 1, sw:])

    ar8 = jnp.broadcast_to(ar, (nseg, sw))
    ai8 = jnp.broadcast_to(ai, (nseg, sw))
    nct = sw // LANES

    def load_cols(i, first):
        return jnp.concatenate(
            [x_scr[first + c, pl.ds(i, nseg, stride=ls), :] for c in range(nct)], axis=1)

    def scan_pass(store):
        def body(i, c):
            hr, hi = c
            nhr = ar8 * hr - ai8 * hi + load_cols(i, 0)
            nhi = ar8 * hi + ai8 * hr + load_cols(i, nct)
            if store:
                for c in range(nct):
                    x_scr[c, pl.ds(i, nseg, stride=ls), :] = nhr[:, c * LANES:(c + 1) * LANES]
                    x_scr[nct + c, pl.ds(i, nseg, stride=ls), :] = nhi[:, c * LANES:(c + 1) * LANES]
            return nhr, nhi
        return body

    for b in range(nb):
        u_b = u_ref[b * seq:(b + 1) * seq, :]
        x_b = jnp.dot(u_b.astype(BF16), bcat, preferred_element_type=F32)
        for c in range(2 * nct):
            x_scr[c] = x_b[:, c * LANES:(c + 1) * LANES]
        z = jnp.zeros((nseg, sw), F32)
        fr, fi = lax.fori_loop(0, ls, scan_pass(False), (z, z))
        cr, ci = mr, mi
        for k in range(nseg):
            init_scr[k:k + 1, 0:sw] = cr
            init_scr[k:k + 1, sw:2 * sw] = ci
            cr, ci = (asr * cr - asi * ci + fr[k:k + 1, :],
                      asr * ci + asi * cr + fi[k:k + 1, :])
        pr_ref[b:b + 1, :] = cr
        pi_ref[b:b + 1, :] = ci
        lax.fori_loop(0, ls, scan_pass(True), (init_scr[:, 0:sw], init_scr[:, sw:2 * sw]))
        h_b = jnp.concatenate([x_scr[c] for c in range(2 * nct)], axis=1)
        y_ref[b * seq:(b + 1) * seq, :] = project_out(h_b, u_b)


def _s5(proj, h0r, h0i, lam_re, lam_im, logdt, bre_bd, bim_bd, cre_bd, cim_bd, dvec,
        nb, seq, ns, nm):
    t_all = proj.shape[0]
    nslab, cw, sw = bre_bd.shape
    d_ssm = nslab * cw
    kern = functools.partial(_s5_kernel, nb=nb, seq=seq, ns=ns, nm=nm)
    return pl.pallas_call(
        kern,
        grid=(nslab,),
        in_specs=[
            pl.BlockSpec((t_all, cw), lambda s: (0, s)),
            pl.BlockSpec((ns, sw), lambda s: (0, s)),
            pl.BlockSpec((ns, sw), lambda s: (0, s)),
            pl.BlockSpec((1, sw), lambda s: (0, s)),
            pl.BlockSpec((1, sw), lambda s: (0, s)),
            pl.BlockSpec((1, sw), lambda s: (0, s)),
            pl.BlockSpec((None, cw, sw), lambda s: (s, 0, 0)),
            pl.BlockSpec((None, cw, sw), lambda s: (s, 0, 0)),
            pl.BlockSpec((None, sw, cw), lambda s: (s, 0, 0)),
            pl.BlockSpec((None, sw, cw), lambda s: (s, 0, 0)),
            pl.BlockSpec((1, cw), lambda s: (0, s)),
        ],
        out_specs=[
            pl.BlockSpec((t_all, cw), lambda s: (0, s)),
            pl.BlockSpec((nb, sw), lambda s: (0, s)),
            pl.BlockSpec((nb, sw), lambda s: (0, s)),
            pl.BlockSpec((ns, sw), lambda s: (0, s)),
            pl.BlockSpec((ns, sw), lambda s: (0, s)),
        ],
        out_shape=[
            jax.ShapeDtypeStruct((t_all, d_ssm), F32),
            jax.ShapeDtypeStruct((nb, nslab * sw), F32),
            jax.ShapeDtypeStruct((nb, nslab * sw), F32),
            jax.ShapeDtypeStruct((ns, nslab * sw), F32),
            jax.ShapeDtypeStruct((ns, nslab * sw), F32),
        ],
        scratch_shapes=[pltpu.VMEM((2 * sw // LANES, seq, LANES), F32),
                        pltpu.VMEM((SUBLANES, 2 * sw), F32)],
        compiler_params=_cparams("parallel"),
        name="s5_scan",
    )(proj, h0r, h0i, lam_re, lam_im, logdt, bre_bd, bim_bd, cre_bd, cim_bd, dvec)


def _glu_norm_kernel(y_ref, w_ref, b_ref, g_ref, o_ref):
    z = _gelu(y_ref[...])
    gate = jnp.dot(z.astype(BF16), w_ref[...], preferred_element_type=F32) + b_ref[...]
    zz = z * _sigmoid(gate)
    s = lax.rsqrt(jnp.mean(zz * zz, axis=-1, keepdims=True) + EPS)
    o_ref[...] = (zz * s * g_ref[...]).astype(o_ref.dtype)


def _glu_norm(y, w, b, g, tm):
    t, d = y.shape
    return pl.pallas_call(
        _glu_norm_kernel,
        grid=(t // tm,),
        in_specs=[
            pl.BlockSpec((tm, d), lambda i: (i, 0)),
            pl.BlockSpec((d, d), lambda i: (0, 0)),
            pl.BlockSpec((1, d), lambda i: (0, 0)),
            pl.BlockSpec((1, d), lambda i: (0, 0)),
        ],
        out_specs=pl.BlockSpec((tm, d), lambda i: (i, 0)),
        out_shape=jax.ShapeDtypeStruct((t, d), BF16),
        compiler_params=_cparams("parallel"),
        name="s5_glu_norm",
    )(y, w, b, g)


def _cumsum_rows(x):
    n = x.shape[0]
    row = lax.broadcasted_iota(jnp.int32, x.shape, 0)
    s = 1
    while s < n:
        x = x + jnp.where(row >= s, pltpu.roll(x, s, axis=0), 0.0)
        s *= 2
    return x


def _gate(xg, w2, b2):
    pre = jnp.dot(xg.astype(BF16), w2, preferred_element_type=F32) + b2
    return _log_sigmoid(pre) * (1.0 / GLA_TAU)


def _head_out(o, r, g):
    o = o * lax.rsqrt(jnp.mean(o * o, axis=-1, keepdims=True) + EPS)
    return o * g * (r * _sigmoid(r))


def _gla_prompt_kernel(q_ref, k_ref, v_ref, r_ref, xg_ref, kt_ref, vt_ref, xgt_ref,
                       w2_ref, b2_ref, g_ref, o_ref, s_ref, st_scr, *, seq, ns, nm):
    dk = q_ref.shape[1]
    scale = float(dk) ** -0.5
    w2 = w2_ref[...]
    b2 = b2_ref[...]
    g = g_ref[...]

    lgm = _gate(xgt_ref[ns:ns + nm, :], w2, b2)
    bm = _cumsum_rows(lgm)
    kdm = kt_ref[ns:ns + nm, :] * jnp.exp(bm[nm - 1:nm, :] - bm)
    st_scr[...] = lax.dot_general(vt_ref[ns:ns + nm, :].astype(BF16), kdm.astype(BF16),
                                  (((0,), (0,)), ((), ())), preferred_element_type=F32)

    c = GLA_CHUNK
    nsub = c // GLA_SUB

    def chunk(ci, carry):
        r0 = pl.multiple_of(ci * c, c)
        q = q_ref[pl.ds(r0, c), :] * scale
        k = k_ref[pl.ds(r0, c), :]
        v = v_ref[pl.ds(r0, c), :]
        vb = v.astype(BF16)
        lg = _gate(xg_ref[pl.ds(r0, c), :], w2, b2)
        b = _cumsum_rows(lg)
        bl = b[c - 1:c, :]
        st = st_scr[...]
        o_inter = lax.dot_general((q * jnp.exp(b)).astype(BF16), st.astype(BF16),
                                  (((1,), (1,)), ((), ())), preferred_element_type=F32)
        outs = []
        for sb in range(nsub):
            lo = sb * GLA_SUB
            hi = lo + GLA_SUB
            beta = b[lo - 1:lo, :] if sb > 0 else jnp.zeros((1, dk), F32)
            qs = q[lo:hi, :] * jnp.exp(b[lo:hi, :] - beta)
            ks = k[0:hi, :] * jnp.exp(beta - b[0:hi, :])
            sc = lax.dot_general(qs.astype(BF16), ks.astype(BF16),
                                 (((1,), (1,)), ((), ())), preferred_element_type=F32)
            rowi = lax.broadcasted_iota(jnp.int32, (GLA_SUB, hi), 0)
            coli = lax.broadcasted_iota(jnp.int32, (GLA_SUB, hi), 1)
            sc = jnp.where(coli <= rowi + lo, sc, 0.0)
            outs.append(jnp.dot(sc.astype(BF16), vb[0:hi, :], preferred_element_type=F32))
        o = o_inter + jnp.concatenate(outs, axis=0)
        o_ref[pl.ds(r0, c), :] = _head_out(o, r_ref[pl.ds(r0, c), :], g).astype(o_ref.dtype)
        kd = k * jnp.exp(bl - b)
        st_scr[...] = jnp.exp(bl) * st + lax.dot_general(
            vb, kd.astype(BF16), (((0,), (0,)), ((), ())), preferred_element_type=F32)
        return carry

    lax.fori_loop(0, seq // c, chunk, 0)
    s_ref[...] = st_scr[...].T


def _gla_prompt(proj, xg, w2, b2, g, nb, seq, ns, nm, nh, dk, dv, q0, k0, v0, r0):
    tp = nb * seq
    tail_blk = tp // TAIL_ROWS
    rb = seq
    kern = functools.partial(_gla_prompt_kernel, seq=seq, ns=ns, nm=nm)
    return pl.pallas_call(
        kern,
        grid=(nb, nh),
        in_specs=[
            pl.BlockSpec((rb, dk), lambda b, h: (b, q0 // dk + h)),
            pl.BlockSpec((rb, dk), lambda b, h: (b, k0 // dk + h)),
            pl.BlockSpec((rb, dv), lambda b, h: (b, v0 // dv + h)),
            pl.BlockSpec((rb, dv), lambda b, h: (b, r0 // dv + h)),
            pl.BlockSpec((rb, LANES), lambda b, h: (b, 0)),
            pl.BlockSpec((TAIL_ROWS, dk), lambda b, h: (tail_blk, k0 // dk + h)),
            pl.BlockSpec((TAIL_ROWS, dv), lambda b, h: (tail_blk, v0 // dv + h)),
            pl.BlockSpec((TAIL_ROWS, LANES), lambda b, h: (tail_blk, 0)),
            pl.BlockSpec((LANES, dk), lambda b, h: (0, h)),
            pl.BlockSpec((1, dk), lambda b, h: (0, h)),
            pl.BlockSpec((1, dv), lambda b, h: (0, h)),
        ],
        out_specs=[
            pl.BlockSpec((rb, dv), lambda b, h: (b, h)),
            pl.BlockSpec((None, None, dk, dv), lambda b, h: (b, h, 0, 0)),
        ],
        out_shape=[
            jax.ShapeDtypeStruct((tp, nh * dv), BF16),
            jax.ShapeDtypeStruct((nb, nh, dk, dv), F32),
        ],
        scratch_shapes=[pltpu.VMEM((dv, dk), F32)],
        compiler_params=_cparams("parallel", "parallel"),
        name="gla_prompt",
    )(proj, proj, proj, proj, xg, proj, proj, xg, w2, b2, g)


def _gla_sample_kernel(q_ref, k_ref, v_ref, r_ref, xg_ref, w2_ref, b2_ref, g_ref, s0_ref,
                       o_ref, s_ref, *, nh, dk, dv):
    n = pl.program_id(0)
    j = n % SUBLANES
    scale = float(dk) ** -0.5
    lg_all = _gate(xg_ref[pl.ds(j, 1), :], w2_ref[...], b2_ref[...])
    q_all = q_ref[pl.ds(j, 1), :] * scale
    k_all = k_ref[pl.ds(j, 1), :]
    v_all = v_ref[pl.ds(j, 1), :]
    r_all = r_ref[pl.ds(j, 1), :]
    g_all = g_ref[...]
    rows = 2 * SUBLANES
    rk = lax.broadcasted_iota(jnp.int32, (rows, dk), 0)
    rv = lax.broadcasted_iota(jnp.int32, (rows, dv), 0)
    outs = []
    for h in range(nh):
        lg = lg_all[:, h * dk:(h + 1) * dk]
        q = q_all[:, h * dk:(h + 1) * dk]
        k = k_all[:, h * dk:(h + 1) * dk]
        v = v_all[:, h * dv:(h + 1) * dv]
        e = jnp.exp(lg)
        s0 = s0_ref[h]
        e_hi = e.astype(BF16)
        e_mid = (e - e_hi.astype(F32)).astype(BF16)
        e_lo = (e - e_hi.astype(F32) - e_mid.astype(F32)).astype(BF16)
        def rows_of(x, w):
            return jnp.broadcast_to(x.astype(F32), (rows, w))
        lhs = jnp.where(rk == 0, rows_of(e_hi, dk),
              jnp.where(rk == 1, rows_of(e_mid, dk),
              jnp.where(rk == 2, rows_of(e_lo, dk),
              jnp.where(rk == 3, rows_of(k, dk), 0.0)))).astype(BF16)
        ones_part = jnp.where(rv < 3, 1.0, 0.0)
        v_part = jnp.where(rv == 3, rows_of(v, dv), 0.0)
        rhs = jnp.concatenate([ones_part, v_part], axis=1).astype(BF16)
        both = lax.dot_general(lhs, rhs, (((0,), (0,)), ((), ())), preferred_element_type=F32)
        s_ref[h] = both[:, :dv] * s0 + both[:, dv:]
        qe = jnp.broadcast_to((q * e).astype(BF16), (rows, dk))
        o = jnp.dot(qe, s0.astype(BF16), preferred_element_type=F32)[0:1, :]
        o = o + jnp.sum(q * k, axis=-1, keepdims=True) * v
        outs.append(_head_out(o, r_all[:, h * dv:(h + 1) * dv], g_all[:, h * dv:(h + 1) * dv]))
    o_ref[pl.ds(j, 1), :] = jnp.concatenate(outs, axis=1)


def _gla_sample(proj, xg, w2, b2, g, s0, tp, ns, nh, dk, dv, q0, k0, v0, r0):
    rb = SUBLANES
    base = tp // rb
    qk_w = nh * dk
    v_w = nh * dv
    kern = functools.partial(_gla_sample_kernel, nh=nh, dk=dk, dv=dv)
    return pl.pallas_call(
        kern,
        grid=(ns,),
        in_specs=[
            pl.BlockSpec((rb, qk_w), lambda n: (base + n // rb, q0 // qk_w)),
            pl.BlockSpec((rb, qk_w), lambda n: (base + n // rb, k0 // qk_w)),
            pl.BlockSpec((rb, v_w), lambda n: (base + n // rb, v0 // v_w)),
            pl.BlockSpec((rb, v_w), lambda n: (base + n // rb, r0 // v_w)),
            pl.BlockSpec((rb, LANES), lambda n: (base + n // rb, 0)),
            pl.BlockSpec((LANES, qk_w), lambda n: (0, 0)),
            pl.BlockSpec((1, qk_w), lambda n: (0, 0)),
            pl.BlockSpec((1, v_w), lambda n: (0, 0)),
            pl.BlockSpec((None, nh, dk, dv), lambda n: (n, 0, 0, 0)),
        ],
        out_specs=[
            pl.BlockSpec((rb, v_w), lambda n: (n // rb, 0)),
            pl.BlockSpec((None, nh, dk, dv), lambda n: (n, 0, 0, 0)),
        ],
        out_shape=[
            jax.ShapeDtypeStruct((ns, v_w), F32),
            jax.ShapeDtypeStruct((ns, nh, dk, dv), F32),
        ],
        compiler_params=_cparams("arbitrary"),
        name="gla_sample",
    )(proj, proj, proj, proj, xg, w2, b2, g, s0)


def _out_proj_kernel(ys_ref, og_ref, w1_ref, w2_ref, x_ref, h_ref):
    acc = jnp.dot(ys_ref[...], w1_ref[...], preferred_element_type=F32)
    acc = acc + jnp.dot(og_ref[...], w2_ref[...], preferred_element_type=F32)
    h_ref[...] = x_ref[...] + acc


def _out_proj(ys, og, w1, w2, x, tm, tn):
    t, d = x.shape
    dh = ys.shape[1]
    return pl.pallas_call(
        _out_proj_kernel,
        grid=(t // tm, d // tn),
        in_specs=[
            pl.BlockSpec((tm, dh), lambda i, j: (i, 0)),
            pl.BlockSpec((tm, dh), lambda i, j: (i, 0)),
            pl.BlockSpec((dh, tn), lambda i, j: (0, j)),
            pl.BlockSpec((dh, tn), lambda i, j: (0, j)),
            pl.BlockSpec((tm, tn), lambda i, j: (i, j)),
        ],
        out_specs=pl.BlockSpec((tm, tn), lambda i, j: (i, j)),
        out_shape=jax.ShapeDtypeStruct((t, d), F32),
        compiler_params=_cparams("parallel", "parallel"),
        name="out_proj",
    )(ys, og, w1, w2, x)


def _norm_t_kernel(h_ref, g_ref, o_ref):
    h = h_ref[...]
    s = lax.rsqrt(jnp.mean(h * h, axis=-1, keepdims=True) + EPS)
    o_ref[...] = (h * s * g_ref[...]).T.astype(BF16)


def _norm_t(h, g, tm):
    t, d = h.shape
    return pl.pallas_call(
        _norm_t_kernel,
        grid=(t // tm,),
        in_specs=[pl.BlockSpec((tm, d), lambda i: (i, 0)), pl.BlockSpec((1, d), lambda i: (0, 0))],
        out_specs=pl.BlockSpec((d, tm), lambda i: (0, i)),
        out_shape=jax.ShapeDtypeStruct((d, t), BF16),
        compiler_params=_cparams("parallel"),
        name="ffn_norm_t",
    )(h, g)


def _topk_rows(x, k):
    outs = []
    for _ in range(k):
        m = jnp.max(x, axis=0, keepdims=True)
        outs.append(m)
        x = jnp.where(x == m, -jnp.inf, x)
    return outs


def _route_kernel(wq_ref, hn_ref, k1_ref, k2_ref, s1_ref, c1_ref, s2_ref, e2_ref, tau_ref):
    half = k1_ref.shape[1]
    tb = hn_ref.shape[1]
    qt = jnp.dot(wq_ref[...], hn_ref[...], preferred_element_type=F32)
    s1 = jnp.dot(k1_ref[...].astype(BF16), qt[:half, :].astype(BF16), preferred_element_type=F32)
    s2 = jnp.dot(k2_ref[...].astype(BF16), qt[half:, :].astype(BF16), preferred_element_type=F32)
    s1_ref[...] = s1
    s2_ref[...] = s2
    for t in range(tb // LANES):
        sl = slice(t * LANES, (t + 1) * LANES)
        a = s1[:, sl]
        b = s2[:, sl]
        v1 = _topk_rows(a, PEER_TOPK)
        v2 = jnp.concatenate(_topk_rows(b, PEER_TOPK), axis=0)
        cand = jnp.concatenate([v + v2 for v in v1], axis=0)
        top = _topk_rows(cand, PEER_TOPK)
        z = jnp.zeros_like(top[0])
        for c in top:
            z = z + jnp.exp(c - top[0])
        tau_ref[:, sl] = top[PEER_TOPK - 1]
        c1_ref[:, sl] = jnp.exp(a - v1[0]) / z
        e2_ref[:, sl] = jnp.exp(b - v2[0:1, :])


def _route(wq_t, hn_t, k1, k2, tb):
    d, t = hn_t.shape
    nh, nk, half = k1.shape
    qd = 2 * half
    tab = jax.ShapeDtypeStruct((nh, nk, t), F32)
    tab_spec = pl.BlockSpec((None, nk, tb), lambda i, h: (h, 0, i))
    return pl.pallas_call(
        _route_kernel,
        grid=(t // tb, nh),
        in_specs=[
            pl.BlockSpec((qd, d), lambda i, h: (h, 0)),
            pl.BlockSpec((d, tb), lambda i, h: (0, i)),
            pl.BlockSpec((None, nk, half), lambda i, h: (h, 0, 0)),
            pl.BlockSpec((None, nk, half), lambda i, h: (h, 0, 0)),
        ],
        out_specs=[tab_spec, tab_spec, tab_spec, tab_spec,
                   pl.BlockSpec((None, 1, tb), lambda i, h: (h, 0, i))],
        out_shape=[tab, tab, tab, tab, jax.ShapeDtypeStruct((nh, 1, t), F32)],
        compiler_params=_cparams("parallel", "parallel"),
        name="peer_route",
    )(wq_t, hn_t, k1, k2)


def _peer_act_kernel(u_ref, hn_ref, s1_ref, c1_ref, s2_ref, e2_ref, tau_ref, a_ref, act_scr):
    eb = u_ref.shape[0]
    tb = hn_ref.shape[1]
    nh, nk, _ = s2_ref.shape
    act_scr[...] = jnp.dot(u_ref[...], hn_ref[...], preferred_element_type=F32)
    for r in range(eb // nk):
        for t in range(tb // LANES):
            sl = slice(t * LANES, (t + 1) * LANES)
            acc = jnp.zeros((nk, LANES), F32)
            for h in range(nh):
                s1row = s1_ref[r, h:h + 1, sl]
                c1row = c1_ref[r, h:h + 1, sl]
                ssum = s1row + s2_ref[h, :, sl]
                acc = acc + jnp.where(ssum >= tau_ref[h, :, sl], c1row * e2_ref[h, :, sl], 0.0)
            act = act_scr[r * nk:(r + 1) * nk, sl]
            a_ref[r * nk:(r + 1) * nk, sl] = (acc * _gelu(act)).astype(BF16)


def _peer_act(u_bf, hn_t, s1, c1, s2, e2, tau, tb, eb):
    ne, d = u_bf.shape
    t = hn_t.shape[1]
    nh, nk, _ = s2.shape
    tab_spec = pl.BlockSpec((nh, nk, tb), lambda i, e: (0, 0, i))
    row_spec = pl.BlockSpec((eb // nk, nh, tb), lambda i, e: (e, 0, i))
    return pl.pallas_call(
        _peer_act_kernel,
        grid=(t // tb, ne // eb),
        in_specs=[
            pl.BlockSpec((eb, d), lambda i, e: (e, 0)),
            pl.BlockSpec((d, tb), lambda i, e: (0, i)),
            row_spec, row_spec, tab_spec, tab_spec,
            pl.BlockSpec((nh, 1, tb), lambda i, e: (0, 0, i)),
        ],
        out_specs=pl.BlockSpec((eb, tb), lambda i, e: (e, i)),
        out_shape=jax.ShapeDtypeStruct((ne, t), BF16),
        scratch_shapes=[pltpu.VMEM((eb, tb), F32)],
        compiler_params=_cparams("parallel", "arbitrary"),
        name="peer_act",
    )(u_bf, hn_t, s1, c1, s2, e2, tau)


def _peer_out_kernel(vt_ref, a_ref, h_ref, g_ref, y_ref, acc_ref):
    kk = pl.program_id(1)

    @pl.when(kk == 0)
    def _():
        acc_ref[...] = jnp.zeros_like(acc_ref)

    acc_ref[...] += jnp.dot(vt_ref[...], a_ref[...], preferred_element_type=F32)

    @pl.when(kk == pl.num_programs(1) - 1)
    def _():
        hh = h_ref[...] + acc_ref[...].T
        s = lax.rsqrt(jnp.mean(hh * hh, axis=-1, keepdims=True) + EPS)
        y_ref[...] = hh * s * g_ref[...]


def _peer_out(v_t, a_t, h, g, tt, ek):
    d, ne = v_t.shape
    t = a_t.shape[1]
    return pl.pallas_call(
        _peer_out_kernel,
        grid=(t // tt, ne // ek),
        in_specs=[
            pl.BlockSpec((d, ek), lambda i, k: (0, k)),
            pl.BlockSpec((ek, tt), lambda i, k: (k, i)),
            pl.BlockSpec((tt, d), lambda i, k: (i, 0)),
            pl.BlockSpec((1, d), lambda i, k: (0, 0)),
        ],
        out_specs=pl.BlockSpec((tt, d), lambda i, k: (i, 0)),
        out_shape=jax.ShapeDtypeStruct((t, d), F32),
        scratch_shapes=[pltpu.VMEM((d, tt), F32)],
        compiler_params=_cparams("parallel", "arbitrary"),
        name="peer_out",
    )(v_t, a_t, h, g)


def _block_diag(w, ngrp):
    nslab, _, a, b = w.shape
    eye = jnp.eye(ngrp, dtype=w.dtype)
    full = w[:, :, :, None, :] * eye[None, :, None, :, None]
    return full.reshape(nslab, ngrp * a, ngrp * b)


def kernel(x_prompt, x_sample, state_s5_re, state_s5_im, state_gla, meta_tokens, norm_mix_g, w_in, s5_lam_re, s5_lam_im, s5_log_dt, s5_b_re, s5_b_im, s5_c_re, s5_c_im, s5_d, s5_w_glu, s5_b_glu, s5_norm_g, gla_w_gate2, gla_b_gate2, gla_norm_g, w_out, norm_ffn_g, peer_w_q, peer_keys, peer_u, peer_v, norm_final_g):
    nb, seq, d = x_prompt.shape
    ns = x_sample.shape[0]
    nm = meta_tokens.shape[0]
    depth = w_in.shape[0]
    assert depth == 1 and x_sample.shape[1] == 1
    tp = nb * seq
    assert tp % TAIL_ROWS == 0 and ns + nm <= TAIL_ROWS and ns % SUBLANES == 0
    assert seq % GLA_CHUNK == 0 and nm % SUBLANES == 0
    t_all = tp + TAIL_ROWS

    ngrp, nstate, gch = s5_b_re.shape[1:]
    d_ssm = ngrp * gch
    gps = LANES // gch
    nslab = ngrp // gps
    nh, dk, dv = state_gla.shape[2:]
    d_gla = nh * dv
    rank = gla_w_gate2.shape[1]
    q0 = d_ssm
    k0 = q0 + nh * dk
    v0 = k0 + nh * dk
    r0 = v0 + d_gla
    g0 = r0 + d_gla
    assert g0 + rank == w_in.shape[2] and d_ssm + d_gla == w_out.shape[1]

    x_all = jnp.concatenate([
        x_prompt.reshape(tp, d), x_sample.reshape(ns, d), meta_tokens,
        jnp.zeros((TAIL_ROWS - ns - nm, d), F32)], axis=0)
    w_main = w_in[0, :, :g0].astype(BF16)
    w_g1 = jnp.pad(w_in[0, :, g0:], ((0, 0), (0, LANES - rank))).astype(BF16)
    w_g2 = jnp.pad(gla_w_gate2[0], ((0, LANES - rank), (0, 0))).astype(BF16)
    lam_re = s5_lam_re[0].reshape(1, ngrp * nstate)
    lam_im = s5_lam_im[0].reshape(1, ngrp * nstate)
    logdt = jnp.repeat(s5_log_dt[0], nstate).reshape(1, ngrp * nstate)
    b_re4 = jnp.transpose(s5_b_re[0].reshape(nslab, gps, nstate, gch), (0, 1, 3, 2))
    b_im4 = jnp.transpose(s5_b_im[0].reshape(nslab, gps, nstate, gch), (0, 1, 3, 2))
    c_re4 = jnp.transpose(s5_c_re[0].reshape(nslab, gps, gch, nstate), (0, 1, 3, 2))
    c_im4 = jnp.transpose(s5_c_im[0].reshape(nslab, gps, gch, nstate), (0, 1, 3, 2))
    bre_bd = _block_diag(b_re4, gps)
    bim_bd = _block_diag(b_im4, gps)
    cre_bd = _block_diag(c_re4, gps)
    cim_bd = _block_diag(c_im4, gps)
    h0r = state_s5_re[0].reshape(ns, ngrp * nstate)
    h0i = state_s5_im[0].reshape(ns, ngrp * nstate)

    tm = _row_tile(t_all, (768, 512, 256))

    proj, xg = _norm_matmul(x_all, norm_mix_g, w_main, w_g1, tm, 512)

    y_raw, pr, pi_, sr, si = _s5(proj, h0r, h0i, lam_re, lam_im, logdt, bre_bd, bim_bd,
                                 cre_bd, cim_bd, s5_d, nb, seq, ns, nm)
    y_ssm = _glu_norm(y_raw, s5_w_glu[0].astype(BF16), s5_b_glu, s5_norm_g, TAIL_ROWS)

    og_p, gla_p = _gla_prompt(proj, xg, w_g2, gla_b_gate2, gla_norm_g, nb, seq, ns, nm,
                              nh, dk, dv, q0, k0, v0, r0)
    og_s, gla_s = _gla_sample(proj, xg, w_g2, gla_b_gate2, gla_norm_g, state_gla[0],
                              tp, ns, nh, dk, dv, q0, k0, v0, r0)
    og = jnp.concatenate([og_p, og_s.astype(BF16),
                          jnp.zeros((TAIL_ROWS - ns, d_gla), BF16)], axis=0)

    w_o = w_out[0].astype(BF16)
    h = _out_proj(y_ssm, og, w_o[:d_ssm], w_o[d_ssm:], x_all, tm, 512)

    hn_t = _norm_t(h, norm_ffn_g, TAIL_ROWS)
    wq_t = peer_w_q[0].T.astype(BF16)
    s1, c1, s2, e2, tau = _route(wq_t, hn_t, peer_keys[0, :, 0], peer_keys[0, :, 1], tm)
    u_bf = peer_u[0].astype(BF16)
    v_t = peer_v[0].T.astype(BF16)
    a_t = _peer_act(u_bf, hn_t, jnp.transpose(s1, (1, 0, 2)), jnp.transpose(c1, (1, 0, 2)),
                    s2, e2, tau, tm, 512)
    tt = _row_tile(t_all, (384, 256))
    y_all = _peer_out(v_t, a_t, h, norm_final_g.reshape(1, d), tt, 512)

    y_prompt = y_all[:tp].reshape(nb, seq, d)
    y_sample = y_all[tp:tp + ns].reshape(ns, 1, d)
    return (y_prompt, y_sample,
            pr.reshape(1, nb, ngrp, nstate), pi_.reshape(1, nb, ngrp, nstate), gla_p[None],
            sr.reshape(1, ns, ngrp, nstate), si.reshape(1, ns, ngrp, nstate), gla_s[None])
```

```python
import functools

import jax
import jax.numpy as jnp
from jax import lax
from jax.experimental import pallas as pl
from jax.experimental.pallas import tpu as pltpu

F32 = jnp.float32
BF16 = jnp.bfloat16

EPS = 1e-6
GLA_TAU = 16.0
GLA_CHUNK = 64
GLA_SUB = 16
PEER_TOPK = 16
LANES = 128
SUBLANES = 8
TAIL_ROWS = 256
VMEM_LIMIT = 56 * 1024 * 1024
PEER_VMEM_LIMIT = 60 * 1024 * 1024


def _cparams(*sem):
    return pltpu.CompilerParams(dimension_semantics=sem, vmem_limit_bytes=VMEM_LIMIT)


def _gelu(x):
    return 0.5 * x * (1.0 + jnp.tanh(0.7978845608028654 * (x + 0.044715 * (x * x * x))))


def _sigmoid(x):
    return 1.0 / (1.0 + jnp.exp(-x))


def _log_sigmoid(x):
    return jnp.minimum(x, 0.0) - jnp.log(1.0 + jnp.exp(-jnp.abs(x)))


def _row_tile(n, cands):
    for c in cands:
        if n % c == 0:
            return c
    raise ValueError(f"no row tile for {n}")


def _rms(x, g):
    return x * lax.rsqrt(jnp.mean(x * x, axis=-1, keepdims=True) + EPS) * g


def _prenorm_kernel(xp_ref, xt_ref, g_ref, o_ref, *, nprompt):
    i = pl.program_id(0)

    @pl.when(i < nprompt)
    def _():
        o_ref[...] = _rms(xp_ref[...], g_ref[...]).astype(BF16)

    @pl.when(i >= nprompt)
    def _():
        o_ref[...] = _rms(xt_ref[...], g_ref[...]).astype(BF16)


def _prenorm(x_p, x_t, g):
    tp, d = x_p.shape
    tr = x_t.shape[0]
    nprompt = tp // tr
    return pl.pallas_call(
        functools.partial(_prenorm_kernel, nprompt=nprompt),
        grid=(nprompt + 1,),
        in_specs=[
            pl.BlockSpec((tr, d), lambda i: (jnp.minimum(i, nprompt - 1), 0)),
            pl.BlockSpec((tr, d), lambda i: (0, 0)),
            pl.BlockSpec((1, d), lambda i: (0, 0)),
        ],
        out_specs=pl.BlockSpec((tr, d), lambda i: (i, 0)),
        out_shape=jax.ShapeDtypeStruct((tp + tr, d), BF16),
        compiler_params=_cparams("arbitrary"),
        name="mix_norm",
    )(x_p, x_t, g)


def _in_proj_kernel(xn_ref, w_ref, wg_ref, o_ref, og_ref):
    @pl.when(pl.program_id(1) == 0)
    def _():
        og_ref[...] = jnp.dot(xn_ref[...], wg_ref[...], preferred_element_type=F32)

    o_ref[...] = jnp.dot(xn_ref[...], w_ref[...], preferred_element_type=F32)


def _in_proj(xn, w, wg, tm, tn):
    t, d = xn.shape
    n = w.shape[1]
    return pl.pallas_call(
        _in_proj_kernel,
        grid=(t // tm, n // tn),
        in_specs=[
            pl.BlockSpec((tm, d), lambda i, j: (i, 0)),
            pl.BlockSpec((d, tn), lambda i, j: (0, j)),
            pl.BlockSpec((d, LANES), lambda i, j: (0, 0)),
        ],
        out_specs=[
            pl.BlockSpec((tm, tn), lambda i, j: (i, j)),
            pl.BlockSpec((tm, LANES), lambda i, j: (i, 0)),
        ],
        out_shape=[jax.ShapeDtypeStruct((t, n), F32), jax.ShapeDtypeStruct((t, LANES), F32)],
        compiler_params=_cparams("parallel", "arbitrary"),
        name="in_proj",
    )(xn, w, wg)


def _s5_kernel(u_ref, h0r_ref, h0i_ref, lr_ref, li_ref, ldt_ref, bre_ref, bim_ref,
               cre_ref, cim_ref, d_ref,
               y_ref, pr_ref, pi_ref, sr_ref, si_ref,
               x_scr, init_scr, up_scr, yp_scr, *, nb, seq, ns, nm):
    sw = lr_ref.shape[1]
    tp = nb * seq
    nseg = SUBLANES
    ls = seq // nseg

    lr = lr_ref[...]
    li = li_ref[...]
    dt = jnp.exp(ldt_ref[...])
    mag = jnp.exp(lr * dt)
    ang = li * dt
    ar = mag * jnp.cos(ang)
    ai = mag * jnp.sin(ang)
    den = lr * lr + li * li
    nr = ar - 1.0
    qr = (nr * lr + ai * li) / den
    qi = (ai * lr - nr * li) / den
    bre = bre_ref[...]
    bim = bim_ref[...]
    bcat = jnp.concatenate([qr * bre - qi * bim, qr * bim + qi * bre], axis=1).astype(BF16)
    ccat = jnp.concatenate([cre_ref[...], -cim_ref[...]], axis=0).astype(BF16)
    dvec = d_ref[...]
    mag_s = jnp.exp(lr * dt * float(ls))
    asr = mag_s * jnp.cos(ang * float(ls))
    asi = mag_s * jnp.sin(ang * float(ls))

    def project_out(h, u):
        return jnp.dot(h.astype(BF16), ccat, preferred_element_type=F32) + dvec * u

    u_s = u_ref[tp:tp + ns, :]
    x_s = jnp.dot(u_s.astype(BF16), bcat, preferred_element_type=F32)
    h0r = h0r_ref[...]
    h0i = h0i_ref[...]
    hr_s = ar * h0r - ai * h0i + x_s[:, :sw]
    hi_s = ar * h0i + ai * h0r + x_s[:, sw:]
    sr_ref[...] = hr_s
    si_ref[...] = hi_s
    y_ref[tp:tp + ns, :] = project_out(jnp.concatenate([hr_s, hi_s], axis=1), u_s)
    y_ref[tp + ns:, :] = jnp.zeros((y_ref.shape[0] - tp - ns, y_ref.shape[1]), F32)

    u_m = u_ref[tp + ns:tp + ns + nm, :]
    x_m = jnp.dot(u_m.astype(BF16), bcat, preferred_element_type=F32)
    mr = jnp.zeros((1, sw), F32)
    mi = jnp.zeros((1, sw), F32)
    for t in range(nm):
        mr, mi = (ar * mr - ai * mi + x_m[t:t + 1, :sw],
                  ar * mi + ai * mr + x_m[t:t + 1, sw:])

    ar8 = jnp.broadcast_to(ar, (nseg, sw))
    ai8 = jnp.broadcast_to(ai, (nseg, sw))
    unroll = 4 if ls % 4 == 0 else 1

    def scan_pass(store):
        def body(i, c):
            hr, hi = c
            r = pl.multiple_of(i * nseg, nseg)
            nhr = ar8 * hr - ai8 * hi + x_scr[pl.ds(r, nseg), 0:sw]
            nhi = ar8 * hi + ai8 * hr + x_scr[pl.ds(r, nseg), sw:2 * sw]
            if store:
                x_scr[pl.ds(r, nseg), 0:sw] = nhr
                x_scr[pl.ds(r, nseg), sw:2 * sw] = nhi
            return nhr, nhi
        return body

    for b in range(nb):
        def regroup_in(i, c, b=b):
            r = pl.multiple_of(i * nseg, nseg)
            up_scr[pl.ds(r, nseg), :] = u_ref[pl.ds(b * seq + i, nseg, stride=ls), :]
            return c
        lax.fori_loop(0, ls, regroup_in, 0, unroll=unroll)
        u_b = up_scr[...]
        x_scr[...] = jnp.dot(u_b.astype(BF16), bcat, preferred_element_type=F32)
        z = jnp.zeros((nseg, sw), F32)
        fr, fi = lax.fori_loop(0, ls, scan_pass(False), (z, z), unroll=unroll)
        cr, ci = mr, mi
        for k in range(nseg):
            init_scr[k:k + 1, 0:sw] = cr
            init_scr[k:k + 1, sw:2 * sw] = ci
            cr, ci = (asr * cr - asi * ci + fr[k:k + 1, :],
                      asr * ci + asi * cr + fi[k:k + 1, :])
        pr_ref[b:b + 1, :] = cr
        pi_ref[b:b + 1, :] = ci
        lax.fori_loop(0, ls, scan_pass(True), (init_scr[:, 0:sw], init_scr[:, sw:2 * sw]),
                      unroll=unroll)
        yp_scr[...] = project_out(x_scr[...], u_b)

        def regroup_out(i, c, b=b):
            r = pl.multiple_of(i * nseg, nseg)
            y_ref[pl.ds(b * seq + i, nseg, stride=ls), :] = yp_scr[pl.ds(r, nseg), :]
            return c
        lax.fori_loop(0, ls, regroup_out, 0, unroll=unroll)


def _s5(proj, h0r, h0i, lam_re, lam_im, logdt, bre_bd, bim_bd, cre_bd, cim_bd, dvec,
        nb, seq, ns, nm):
    t_all = proj.shape[0]
    nslab, cw, sw = bre_bd.shape
    d_ssm = nslab * cw
    kern = functools.partial(_s5_kernel, nb=nb, seq=seq, ns=ns, nm=nm)
    return pl.pallas_call(
        kern,
        grid=(nslab,),
        in_specs=[
            pl.BlockSpec((t_all, cw), lambda s: (0, s)),
            pl.BlockSpec((ns, sw), lambda s: (0, s)),
            pl.BlockSpec((ns, sw), lambda s: (0, s)),
            pl.BlockSpec((1, sw), lambda s: (0, s)),
            pl.BlockSpec((1, sw), lambda s: (0, s)),
            pl.BlockSpec((1, sw), lambda s: (0, s)),
            pl.BlockSpec((None, cw, sw), lambda s: (s, 0, 0)),
            pl.BlockSpec((None, cw, sw), lambda s: (s, 0, 0)),
            pl.BlockSpec((None, sw, cw), lambda s: (s, 0, 0)),
            pl.BlockSpec((None, sw, cw), lambda s: (s, 0, 0)),
            pl.BlockSpec((1, cw), lambda s: (0, s)),
        ],
        out_specs=[
            pl.BlockSpec((t_all, cw), lambda s: (0, s)),
            pl.BlockSpec((nb, sw), lambda s: (0, s)),
            pl.BlockSpec((nb, sw), lambda s: (0, s)),
            pl.BlockSpec((ns, sw), lambda s: (0, s)),
            pl.BlockSpec((ns, sw), lambda s: (0, s)),
        ],
        out_shape=[
            jax.ShapeDtypeStruct((t_all, d_ssm), F32),
            jax.ShapeDtypeStruct((nb, nslab * sw), F32),
            jax.ShapeDtypeStruct((nb, nslab * sw), F32),
            jax.ShapeDtypeStruct((ns, nslab * sw), F32),
            jax.ShapeDtypeStruct((ns, nslab * sw), F32),
        ],
        scratch_shapes=[pltpu.VMEM((seq, 2 * sw), F32), pltpu.VMEM((SUBLANES, 2 * sw), F32),
                        pltpu.VMEM((seq, cw), F32), pltpu.VMEM((seq, cw), F32)],
        compiler_params=_cparams("parallel"),
        name="s5_scan",
    )(proj, h0r, h0i, lam_re, lam_im, logdt, bre_bd, bim_bd, cre_bd, cim_bd, dvec)


def _glu_norm_kernel(y_ref, w_ref, b_ref, g_ref, o_ref):
    z = _gelu(y_ref[...])
    gate = jnp.dot(z.astype(BF16), w_ref[...], preferred_element_type=F32) + b_ref[...]
    zz = z * _sigmoid(gate)
    s = lax.rsqrt(jnp.mean(zz * zz, axis=-1, keepdims=True) + EPS)
    o_ref[...] = (zz * s * g_ref[...]).astype(o_ref.dtype)


def _glu_norm(y, w, b, g, tm):
    t, d = y.shape
    return pl.pallas_call(
        _glu_norm_kernel,
        grid=(t // tm,),
        in_specs=[
            pl.BlockSpec((tm, d), lambda i: (i, 0)),
            pl.BlockSpec((d, d), lambda i: (0, 0)),
            pl.BlockSpec((1, d), lambda i: (0, 0)),
            pl.BlockSpec((1, d), lambda i: (0, 0)),
        ],
        out_specs=pl.BlockSpec((tm, d), lambda i: (i, 0)),
        out_shape=jax.ShapeDtypeStruct((t, d), BF16),
        compiler_params=_cparams("parallel"),
        name="s5_glu_norm",
    )(y, w, b, g)


def _cumsum_rows(x):
    n = x.shape[0]
    row = lax.broadcasted_iota(jnp.int32, x.shape, 0)
    s = 1
    while s < n:
        x = x + jnp.where(row >= s, pltpu.roll(x, s, axis=0), 0.0)
        s *= 2
    return x


def _gate(xg, w2, b2):
    pre = jnp.dot(xg.astype(BF16), w2, preferred_element_type=F32) + b2
    return _log_sigmoid(pre) * (1.0 / GLA_TAU)


def _head_out(o, r, g):
    o = o * lax.rsqrt(jnp.mean(o * o, axis=-1, keepdims=True) + EPS)
    return o * g * (r * _sigmoid(r))


def _gla_prompt_kernel(q_ref, k_ref, v_ref, r_ref, xg_ref, kt_ref, vt_ref, xgt_ref,
                       w2_ref, b2_ref, g_ref, o_ref, s_ref, st_scr, *, seq, ns, nm):
    dk = q_ref.shape[1]
    scale = float(dk) ** -0.5
    w2 = w2_ref[...]
    b2 = b2_ref[...]
    g = g_ref[...]

    lgm = _gate(xgt_ref[ns:ns + nm, :], w2, b2)
    bm = _cumsum_rows(lgm)
    kdm = kt_ref[ns:ns + nm, :] * jnp.exp(bm[nm - 1:nm, :] - bm)
    st_scr[...] = lax.dot_general(vt_ref[ns:ns + nm, :].astype(BF16), kdm.astype(BF16),
                                  (((0,), (0,)), ((), ())), preferred_element_type=F32)

    c = GLA_CHUNK
    nsub = c // GLA_SUB

    def chunk(ci, carry):
        r0 = pl.multiple_of(ci * c, c)
        q = q_ref[pl.ds(r0, c), :] * scale
        k = k_ref[pl.ds(r0, c), :]
        v = v_ref[pl.ds(r0, c), :]
        vb = v.astype(BF16)
        lg = _gate(xg_ref[pl.ds(r0, c), :], w2, b2)
        b = _cumsum_rows(lg)
        bl = b[c - 1:c, :]
        st = st_scr[...]
        o_inter = lax.dot_general((q * jnp.exp(b)).astype(BF16), st.astype(BF16),
                                  (((1,), (1,)), ((), ())), preferred_element_type=F32)
        outs = []
        for sb in range(nsub):
            lo = sb * GLA_SUB
            hi = lo + GLA_SUB
            beta = b[lo - 1:lo, :] if sb > 0 else jnp.zeros((1, dk), F32)
            qs = q[lo:hi, :] * jnp.exp(b[lo:hi, :] - beta)
            ks = k[0:hi, :] * jnp.exp(beta - b[0:hi, :])
            sc = lax.dot_general(qs.astype(BF16), ks.astype(BF16),
                                 (((1,), (1,)), ((), ())), preferred_element_type=F32)
            rowi = lax.broadcasted_iota(jnp.int32, (GLA_SUB, hi), 0)
            coli = lax.broadcasted_iota(jnp.int32, (GLA_SUB, hi), 1)
            sc = jnp.where(coli <= rowi + lo, sc, 0.0)
            outs.append(jnp.dot(sc.astype(BF16), vb[0:hi, :], preferred_element_type=F32))
        o = o_inter + jnp.concatenate(outs, axis=0)
        o_ref[pl.ds(r0, c), :] = _head_out(o, r_ref[pl.ds(r0, c), :], g).astype(o_ref.dtype)
        kd = k * jnp.exp(bl - b)
        st_scr[...] = jnp.exp(bl) * st + lax.dot_general(
            vb, kd.astype(BF16), (((0,), (0,)), ((), ())), preferred_element_type=F32)
        return carry

    lax.fori_loop(0, seq // c, chunk, 0)
    s_ref[...] = st_scr[...].T


def _gla_prompt(proj, xg, w2, b2, g, nb, seq, ns, nm, nh, dk, dv, q0, k0, v0, r0):
    tp = nb * seq
    tail_blk = tp // TAIL_ROWS
    rb = seq
    kern = functools.partial(_gla_prompt_kernel, seq=seq, ns=ns, nm=nm)
    return pl.pallas_call(
        kern,
        grid=(nb, nh),
        in_specs=[
            pl.BlockSpec((rb, dk), lambda b, h: (b, q0 // dk + h)),
            pl.BlockSpec((rb, dk), lambda b, h: (b, k0 // dk + h)),
            pl.BlockSpec((rb, dv), lambda b, h: (b, v0 // dv + h)),
            pl.BlockSpec((rb, dv), lambda b, h: (b, r0 // dv + h)),
            pl.BlockSpec((rb, LANES), lambda b, h: (b, 0)),
            pl.BlockSpec((TAIL_ROWS, dk), lambda b, h: (tail_blk, k0 // dk + h)),
            pl.BlockSpec((TAIL_ROWS, dv), lambda b, h: (tail_blk, v0 // dv + h)),
            pl.BlockSpec((TAIL_ROWS, LANES), lambda b, h: (tail_blk, 0)),
            pl.BlockSpec((LANES, dk), lambda b, h: (0, h)),
            pl.BlockSpec((1, dk), lambda b, h: (0, h)),
            pl.BlockSpec((1, dv), lambda b, h: (0, h)),
        ],
        out_specs=[
            pl.BlockSpec((rb, dv), lambda b, h: (b, h)),
            pl.BlockSpec((None, None, dk, dv), lambda b, h: (b, h, 0, 0)),
        ],
        out_shape=[
            jax.ShapeDtypeStruct((tp, nh * dv), BF16),
            jax.ShapeDtypeStruct((nb, nh, dk, dv), F32),
        ],
        scratch_shapes=[pltpu.VMEM((dv, dk), F32)],
        compiler_params=_cparams("parallel", "parallel"),
        name="gla_prompt",
    )(proj, proj, proj, proj, xg, proj, proj, xg, w2, b2, g)


def _gla_sample_kernel(q_ref, k_ref, v_ref, r_ref, xg_ref, w2_ref, b2_ref, g_ref, s0_ref,
                       o_ref, s_ref, *, nh, dk, dv):
    n = pl.program_id(0)
    j = n % SUBLANES
    scale = float(dk) ** -0.5
    lg_all = _gate(xg_ref[pl.ds(j, 1), :], w2_ref[...], b2_ref[...])
    q_all = q_ref[pl.ds(j, 1), :] * scale
    k_all = k_ref[pl.ds(j, 1), :]
    v_all = v_ref[pl.ds(j, 1), :]
    r_all = r_ref[pl.ds(j, 1), :]
    g_all = g_ref[...]
    rows = 2 * SUBLANES
    rk = lax.broadcasted_iota(jnp.int32, (rows, dk), 0)
    rv = lax.broadcasted_iota(jnp.int32, (rows, dv), 0)
    outs = []
    for h in range(nh):
        lg = lg_all[:, h * dk:(h + 1) * dk]
        q = q_all[:, h * dk:(h + 1) * dk]
        k = k_all[:, h * dk:(h + 1) * dk]
        v = v_all[:, h * dv:(h + 1) * dv]
        e = jnp.exp(lg)
        s0 = s0_ref[h]
        e_hi = e.astype(BF16)
        e_mid = (e - e_hi.astype(F32)).astype(BF16)
        e_lo = (e - e_hi.astype(F32) - e_mid.astype(F32)).astype(BF16)
        def rows_of(x, w):
            return jnp.broadcast_to(x.astype(F32), (rows, w))
        lhs = jnp.where(rk == 0, rows_of(e_hi, dk),
              jnp.where(rk == 1, rows_of(e_mid, dk),
              jnp.where(rk == 2, rows_of(e_lo, dk),
              jnp.where(rk == 3, rows_of(k, dk), 0.0)))).astype(BF16)
        ones_part = jnp.where(rv < 3, 1.0, 0.0)
        v_part = jnp.where(rv == 3, rows_of(v, dv), 0.0)
        rhs = jnp.concatenate([ones_part, v_part], axis=1).astype(BF16)
        both = lax.dot_general(lhs, rhs, (((0,), (0,)), ((), ())), preferred_element_type=F32)
        s_ref[h] = both[:, :dv] * s0 + both[:, dv:]
        qe = jnp.broadcast_to((q * e).astype(BF16), (rows, dk))
        o = jnp.dot(qe, s0.astype(BF16), preferred_element_type=F32)[0:1, :]
        o = o + jnp.sum(q * k, axis=-1, keepdims=True) * v
        outs.append(_head_out(o, r_all[:, h * dv:(h + 1) * dv], g_all[:, h * dv:(h + 1) * dv]))
    o_ref[pl.ds(j, 1), :] = jnp.concatenate(outs, axis=1)


def _gla_sample(proj, xg, w2, b2, g, s0, tp, ns, nh, dk, dv, q0, k0, v0, r0):
    rb = SUBLANES
    base = tp // rb
    qk_w = nh * dk
    v_w = nh * dv
    kern = functools.partial(_gla_sample_kernel, nh=nh, dk=dk, dv=dv)
    return pl.pallas_call(
        kern,
        grid=(ns,),
        in_specs=[
            pl.BlockSpec((rb, qk_w), lambda n: (base + n // rb, q0 // qk_w)),
            pl.BlockSpec((rb, qk_w), lambda n: (base + n // rb, k0 // qk_w)),
            pl.BlockSpec((rb, v_w), lambda n: (base + n // rb, v0 // v_w)),
            pl.BlockSpec((rb, v_w), lambda n: (base + n // rb, r0 // v_w)),
            pl.BlockSpec((rb, LANES), lambda n: (base + n // rb, 0)),
            pl.BlockSpec((LANES, qk_w), lambda n: (0, 0)),
            pl.BlockSpec((1, qk_w), lambda n: (0, 0)),
            pl.BlockSpec((1, v_w), lambda n: (0, 0)),
            pl.BlockSpec((None, nh, dk, dv), lambda n: (n, 0, 0, 0)),
        ],
        out_specs=[
            pl.BlockSpec((rb, v_w), lambda n: (n // rb, 0)),
            pl.BlockSpec((None, nh, dk, dv), lambda n: (n, 0, 0, 0)),
        ],
        out_shape=[
            jax.ShapeDtypeStruct((ns, v_w), F32),
            jax.ShapeDtypeStruct((ns, nh, dk, dv), F32),
        ],
        compiler_params=_cparams("arbitrary"),
        name="gla_sample",
    )(proj, proj, proj, proj, xg, w2, b2, g, s0)


def _out_proj_kernel(ys_ref, og_ref, w1_ref, w2_ref, m_ref):
    acc = jnp.dot(ys_ref[...], w1_ref[...], preferred_element_type=F32)
    m_ref[...] = acc + jnp.dot(og_ref[...], w2_ref[...], preferred_element_type=F32)


def _out_proj(ys, og, w1, w2, tm, tn):
    t, dh = ys.shape
    d = w1.shape[1]
    return pl.pallas_call(
        _out_proj_kernel,
        grid=(t // tm, d // tn),
        in_specs=[
            pl.BlockSpec((tm, dh), lambda i, j: (i, 0)),
            pl.BlockSpec((tm, dh), lambda i, j: (i, 0)),
            pl.BlockSpec((dh, tn), lambda i, j: (0, j)),
            pl.BlockSpec((dh, tn), lambda i, j: (0, j)),
        ],
        out_specs=pl.BlockSpec((tm, tn), lambda i, j: (i, j)),
        out_shape=jax.ShapeDtypeStruct((t, d), F32),
        compiler_params=_cparams("parallel", "parallel"),
        name="out_proj",
    )(ys, og, w1, w2)


def _resid_norm_t_kernel(m_ref, xp_ref, xt_ref, g_ref, h_ref, o_ref, *, nprompt):
    i = pl.program_id(0)

    def emit(x):
        h = x + m_ref[...]
        h_ref[...] = h
        o_ref[...] = _rms(h, g_ref[...]).T.astype(BF16)

    @pl.when(i < nprompt)
    def _():
        emit(xp_ref[...])

    @pl.when(i >= nprompt)
    def _():
        emit(xt_ref[...])


def _resid_norm_t(mix, x_p, x_t, g):
    t, d = mix.shape
    tr = x_t.shape[0]
    nprompt = x_p.shape[0] // tr
    return pl.pallas_call(
        functools.partial(_resid_norm_t_kernel, nprompt=nprompt),
        grid=(nprompt + 1,),
        in_specs=[
            pl.BlockSpec((tr, d), lambda i: (i, 0)),
            pl.BlockSpec((tr, d), lambda i: (jnp.minimum(i, nprompt - 1), 0)),
            pl.BlockSpec((tr, d), lambda i: (0, 0)),
            pl.BlockSpec((1, d), lambda i: (0, 0)),
        ],
        out_specs=[pl.BlockSpec((tr, d), lambda i: (i, 0)), pl.BlockSpec((d, tr), lambda i: (0, i))],
        out_shape=[jax.ShapeDtypeStruct((t, d), F32), jax.ShapeDtypeStruct((d, t), BF16)],
        compiler_params=_cparams("arbitrary"),
        name="resid_ffn_norm_t",
    )(mix, x_p, x_t, g)


def _topk_rows(x, k):
    outs = []
    for _ in range(k):
        m = jnp.max(x, axis=0, keepdims=True)
        outs.append(m)
        x = jnp.where(x == m, -jnp.inf, x)
    return outs


def _route_kernel(wq_ref, hn_ref, k1_ref, k2_ref, s1_ref, c1_ref, s2_ref, e2_ref, tau_ref):
    half = k1_ref.shape[1]
    tb = hn_ref.shape[1]
    qt = jnp.dot(wq_ref[...], hn_ref[...], preferred_element_type=F32)
    s1 = jnp.dot(k1_ref[...].astype(BF16), qt[:half, :].astype(BF16), preferred_element_type=F32)
    s2 = jnp.dot(k2_ref[...].astype(BF16), qt[half:, :].astype(BF16), preferred_element_type=F32)
    s1_ref[...] = s1
    s2_ref[...] = s2
    for t in range(tb // LANES):
        sl = slice(t * LANES, (t + 1) * LANES)
        a = s1[:, sl]
        b = s2[:, sl]
        v1 = _topk_rows(a, PEER_TOPK)
        v2 = jnp.concatenate(_topk_rows(b, PEER_TOPK), axis=0)
        cand = jnp.concatenate([v + v2 for v in v1], axis=0)
        top = _topk_rows(cand, PEER_TOPK)
        z = jnp.zeros_like(top[0])
        for c in top:
            z = z + jnp.exp(c - top[0])
        tau_ref[:, sl] = top[PEER_TOPK - 1]
        c1_ref[:, sl] = jnp.exp(a - v1[0]) / z
        e2_ref[:, sl] = jnp.exp(b - v2[0:1, :])


def _route(wq_t, hn_t, k1, k2, tb):
    d, t = hn_t.shape
    nh, nk, half = k1.shape
    qd = 2 * half
    tab = jax.ShapeDtypeStruct((nh, nk, t), F32)
    tab_spec = pl.BlockSpec((None, nk, tb), lambda i, h: (h, 0, i))
    return pl.pallas_call(
        _route_kernel,
        grid=(t // tb, nh),
        in_specs=[
            pl.BlockSpec((qd, d), lambda i, h: (h, 0)),
            pl.BlockSpec((d, tb), lambda i, h: (0, i)),
            pl.BlockSpec((None, nk, half), lambda i, h: (h, 0, 0)),
            pl.BlockSpec((None, nk, half), lambda i, h: (h, 0, 0)),
        ],
        out_specs=[tab_spec, tab_spec, tab_spec, tab_spec,
                   pl.BlockSpec((None, 1, tb), lambda i, h: (h, 0, i))],
        out_shape=[tab, tab, tab, tab, jax.ShapeDtypeStruct((nh, 1, t), F32)],
        compiler_params=_cparams("parallel", "parallel"),
        name="peer_route",
    )(wq_t, hn_t, k1, k2)


GATE_ROWS = 32
OUT_ROWS = 512


K_CHUNK = 512


def _peer_kernel(u_ref, vt_ref, hn_ref, s1_ref, c1_ref, s2_ref, e2_ref, tau_ref, o_ref,
                 w_scr, act_scr, a_scr):
    eb, d = u_ref.shape
    tb = hn_ref.shape[1]
    nh, nk, _ = s2_ref.shape
    nr = eb // nk

    @pl.when(pl.program_id(1) == 0)
    def _():
        o_ref[...] = jnp.zeros_like(o_ref)

    def gate_group(t, q):
        sl = slice(t * LANES, (t + 1) * LANES)
        rows = slice(q * GATE_ROWS, (q + 1) * GATE_ROWS)
        accs = [None] * nr
        for h in range(nh):
            s2t = s2_ref[h, rows, sl]
            e2t = e2_ref[h, rows, sl]
            tau = tau_ref[h, :, sl]
            for r in range(nr):
                ssum = s1_ref[r, h:h + 1, sl] + s2t
                term = jnp.where(ssum >= tau, c1_ref[r, h:h + 1, sl] * e2t, 0.0)
                accs[r] = term if accs[r] is None else accs[r] + term
        for r in range(nr):
            lo = r * nk + q * GATE_ROWS
            w_scr[lo:lo + GATE_ROWS, sl] = accs[r]

    groups = [(t, q) for t in range(tb // LANES) for q in range(nk // GATE_ROWS)]
    nkc = d // K_CHUNK
    per = -(-len(groups) // nkc)
    for kc in range(nkc):
        ks = slice(kc * K_CHUNK, (kc + 1) * K_CHUNK)
        part = jnp.dot(u_ref[:, ks], hn_ref[ks, :], preferred_element_type=F32)
        if kc == 0:
            act_scr[...] = part
        else:
            act_scr[...] += part
        for t, q in groups[kc * per:(kc + 1) * per]:
            gate_group(t, q)
    a_scr[...] = (w_scr[...] * _gelu(act_scr[...])).astype(BF16)
    for dc in range(d // OUT_ROWS):
        dr = slice(dc * OUT_ROWS, (dc + 1) * OUT_ROWS)
        o_ref[dr, :] += jnp.dot(vt_ref[dr, :], a_scr[...], preferred_element_type=F32)


def _peer(u_bf, v_t, hn_t, s1, c1, s2, e2, tau, tb, eb):
    ne, d = u_bf.shape
    t = hn_t.shape[1]
    nh, nk, _ = s2.shape
    once = pl.Buffered(1)
    tab_spec = pl.BlockSpec((nh, nk, tb), lambda i, e: (0, 0, i), pipeline_mode=once)
    row_spec = pl.BlockSpec((eb // nk, nh, tb), lambda i, e: (e, 0, i))
    return pl.pallas_call(
        _peer_kernel,
        grid=(t // tb, ne // eb),
        in_specs=[
            pl.BlockSpec((eb, d), lambda i, e: (e, 0)),
            pl.BlockSpec((d, eb), lambda i, e: (0, e)),
            pl.BlockSpec((d, tb), lambda i, e: (0, i), pipeline_mode=once),
            row_spec, row_spec, tab_spec, tab_spec,
            pl.BlockSpec((nh, 1, tb), lambda i, e: (0, 0, i)),
        ],
        out_specs=pl.BlockSpec((d, tb), lambda i, e: (0, i)),
        out_shape=jax.ShapeDtypeStruct((d, t), F32),
        scratch_shapes=[pltpu.VMEM((eb, tb), F32), pltpu.VMEM((eb, tb), F32),
                        pltpu.VMEM((eb, tb), BF16)],
        compiler_params=pltpu.CompilerParams(
            dimension_semantics=("parallel", "arbitrary"), vmem_limit_bytes=PEER_VMEM_LIMIT),
        name="peer_experts",
    )(u_bf, v_t, hn_t, s1, c1, s2, e2, tau)


def _final_kernel(pt_ref, h_ref, g_ref, yp_ref, yt_ref, *, nprompt):
    i = pl.program_id(0)
    y = _rms(h_ref[...] + pt_ref[...].T, g_ref[...])

    @pl.when(i < nprompt)
    def _():
        yp_ref[...] = y

    @pl.when(i >= nprompt)
    def _():
        yt_ref[...] = y


def _final(peer_t, h, g, tp, tr):
    d, t = peer_t.shape
    nprompt = tp // tr
    return pl.pallas_call(
        functools.partial(_final_kernel, nprompt=nprompt),
        grid=(nprompt + 1,),
        in_specs=[
            pl.BlockSpec((d, tr), lambda i: (0, i)),
            pl.BlockSpec((tr, d), lambda i: (i, 0)),
            pl.BlockSpec((1, d), lambda i: (0, 0)),
        ],
        out_specs=[
            pl.BlockSpec((tr, d), lambda i: (jnp.minimum(i, nprompt - 1), 0)),
            pl.BlockSpec((tr, d), lambda i: (0, 0)),
        ],
        out_shape=[jax.ShapeDtypeStruct((tp, d), F32), jax.ShapeDtypeStruct((tr, d), F32)],
        compiler_params=_cparams("arbitrary"),
        name="final_norm",
    )(peer_t, h, g)


def _block_diag(w, ngrp):
    nslab, _, a, b = w.shape
    eye = jnp.eye(ngrp, dtype=w.dtype)
    full = w[:, :, :, None, :] * eye[None, :, None, :, None]
    return full.reshape(nslab, ngrp * a, ngrp * b)


def kernel(x_prompt, x_sample, state_s5_re, state_s5_im, state_gla, meta_tokens, norm_mix_g, w_in, s5_lam_re, s5_lam_im, s5_log_dt, s5_b_re, s5_b_im, s5_c_re, s5_c_im, s5_d, s5_w_glu, s5_b_glu, s5_norm_g, gla_w_gate2, gla_b_gate2, gla_norm_g, w_out, norm_ffn_g, peer_w_q, peer_keys, peer_u, peer_v, norm_final_g):
    nb, seq, d = x_prompt.shape
    ns = x_sample.shape[0]
    nm = meta_tokens.shape[0]
    depth = w_in.shape[0]
    assert depth == 1 and x_sample.shape[1] == 1
    tp = nb * seq
    assert tp % TAIL_ROWS == 0 and ns + nm <= TAIL_ROWS and ns % SUBLANES == 0
    assert seq % GLA_CHUNK == 0 and nm % SUBLANES == 0
    t_all = tp + TAIL_ROWS

    ngrp, nstate, gch = s5_b_re.shape[1:]
    d_ssm = ngrp * gch
    gps = LANES // gch
    nslab = ngrp // gps
    nh, dk, dv = state_gla.shape[2:]
    d_gla = nh * dv
    rank = gla_w_gate2.shape[1]
    q0 = d_ssm
    k0 = q0 + nh * dk
    v0 = k0 + nh * dk
    r0 = v0 + d_gla
    g0 = r0 + d_gla
    assert g0 + rank == w_in.shape[2] and d_ssm + d_gla == w_out.shape[1]

    x_p = x_prompt.reshape(tp, d)
    x_t = jnp.concatenate([x_sample.reshape(ns, d), meta_tokens,
                           jnp.zeros((TAIL_ROWS - ns - nm, d), F32)], axis=0)
    w_main = w_in[0, :, :g0].astype(BF16)
    w_g1 = jnp.pad(w_in[0, :, g0:], ((0, 0), (0, LANES - rank))).astype(BF16)
    w_g2 = jnp.pad(gla_w_gate2[0], ((0, LANES - rank), (0, 0))).astype(BF16)
    lam_re = s5_lam_re[0].reshape(1, ngrp * nstate)
    lam_im = s5_lam_im[0].reshape(1, ngrp * nstate)
    logdt = jnp.repeat(s5_log_dt[0], nstate).reshape(1, ngrp * nstate)
    b_re4 = jnp.transpose(s5_b_re[0].reshape(nslab, gps, nstate, gch), (0, 1, 3, 2))
    b_im4 = jnp.transpose(s5_b_im[0].reshape(nslab, gps, nstate, gch), (0, 1, 3, 2))
    c_re4 = jnp.transpose(s5_c_re[0].reshape(nslab, gps, gch, nstate), (0, 1, 3, 2))
    c_im4 = jnp.transpose(s5_c_im[0].reshape(nslab, gps, gch, nstate), (0, 1, 3, 2))
    bre_bd = _block_diag(b_re4, gps)
    bim_bd = _block_diag(b_im4, gps)
    cre_bd = _block_diag(c_re4, gps)
    cim_bd = _block_diag(c_im4, gps)
    h0r = state_s5_re[0].reshape(ns, ngrp * nstate)
    h0i = state_s5_im[0].reshape(ns, ngrp * nstate)

    tm = _row_tile(t_all, (768, 512, 256))

    xn = _prenorm(x_p, x_t, norm_mix_g)
    proj, xg = _in_proj(xn, w_main, w_g1, tm, 512)

    y_raw, pr, pi_, sr, si = _s5(proj, h0r, h0i, lam_re, lam_im, logdt, bre_bd, bim_bd,
                                 cre_bd, cim_bd, s5_d, nb, seq, ns, nm)
    y_ssm = _glu_norm(y_raw, s5_w_glu[0].astype(BF16), s5_b_glu, s5_norm_g, TAIL_ROWS)

    og_p, gla_p = _gla_prompt(proj, xg, w_g2, gla_b_gate2, gla_norm_g, nb, seq, ns, nm,
                              nh, dk, dv, q0, k0, v0, r0)
    og_s, gla_s = _gla_sample(proj, xg, w_g2, gla_b_gate2, gla_norm_g, state_gla[0],
                              tp, ns, nh, dk, dv, q0, k0, v0, r0)
    og = jnp.concatenate([og_p, og_s.astype(BF16),
                          jnp.zeros((TAIL_ROWS - ns, d_gla), BF16)], axis=0)

    w_o = w_out[0].astype(BF16)
    mix = _out_proj(y_ssm, og, w_o[:d_ssm], w_o[d_ssm:], tm, 512)

    h, hn_t = _resid_norm_t(mix, x_p, x_t, norm_ffn_g)
    wq_t = peer_w_q[0].T.astype(BF16)
    s1, c1, s2, e2, tau = _route(wq_t, hn_t, peer_keys[0, :, 0], peer_keys[0, :, 1], tm)
    u_bf = peer_u[0].astype(BF16)
    v_t = peer_v[0].T.astype(BF16)
    peer_t = _peer(u_bf, v_t, hn_t, jnp.transpose(s1, (1, 0, 2)), jnp.transpose(c1, (1, 0, 2)),
                   s2, e2, tau, tm, 512)
    y_p, y_t = _final(peer_t, h, norm_final_g.reshape(1, d), tp, TAIL_ROWS)

    y_prompt = y_p.reshape(nb, seq, d)
    y_sample = y_t[:ns].reshape(ns, 1, d)
    return (y_prompt, y_sample,
            pr.reshape(1, nb, ngrp, nstate), pi_.reshape(1, nb, ngrp, nstate), gla_p[None],
            sr.reshape(1, ns, ngrp, nstate), si.reshape(1, ns, ngrp, nstate), gla_s[None])
```

```python
import functools

import jax
import jax.numpy as jnp
from jax import lax
from jax.experimental import pallas as pl
from jax.experimental.pallas import tpu as pltpu

F32 = jnp.float32
BF16 = jnp.bfloat16

EPS = 1e-6
GLA_TAU = 16.0
GLA_CHUNK = 64
GLA_SUB = 16
PEER_TOPK = 16
LANES = 128
SUBLANES = 8
TAIL_ROWS = 256
VMEM_LIMIT = 56 * 1024 * 1024
PEER_VMEM_LIMIT = 60 * 1024 * 1024


def _cparams(*sem):
    return pltpu.CompilerParams(dimension_semantics=sem, vmem_limit_bytes=VMEM_LIMIT)


def _gelu(x):
    return 0.5 * x * (1.0 + jnp.tanh(0.7978845608028654 * (x + 0.044715 * (x * x * x))))


def _sigmoid(x):
    return 1.0 / (1.0 + jnp.exp(-x))


def _log_sigmoid(x):
    return jnp.minimum(x, 0.0) - jnp.log(1.0 + jnp.exp(-jnp.abs(x)))


def _row_tile(n, cands):
    for c in cands:
        if n % c == 0:
            return c
    raise ValueError(f"no row tile for {n}")


def _rms(x, g):
    return x * lax.rsqrt(jnp.mean(x * x, axis=-1, keepdims=True) + EPS) * g


def _prenorm_kernel(xp_ref, xt_ref, g_ref, o_ref, *, nprompt):
    i = pl.program_id(0)

    @pl.when(i < nprompt)
    def _():
        o_ref[...] = _rms(xp_ref[...], g_ref[...]).astype(BF16)

    @pl.when(i >= nprompt)
    def _():
        o_ref[...] = _rms(xt_ref[...], g_ref[...]).astype(BF16)


def _prenorm(x_p, x_t, g):
    tp, d = x_p.shape
    tr = x_t.shape[0]
    nprompt = tp // tr
    return pl.pallas_call(
        functools.partial(_prenorm_kernel, nprompt=nprompt),
        grid=(nprompt + 1,),
        in_specs=[
            pl.BlockSpec((tr, d), lambda i: (jnp.minimum(i, nprompt - 1), 0)),
            pl.BlockSpec((tr, d), lambda i: (0, 0)),
            pl.BlockSpec((1, d), lambda i: (0, 0)),
        ],
        out_specs=pl.BlockSpec((tr, d), lambda i: (i, 0)),
        out_shape=jax.ShapeDtypeStruct((tp + tr, d), BF16),
        compiler_params=_cparams("arbitrary"),
        name="mix_norm",
    )(x_p, x_t, g)


def _in_proj_kernel(xn_ref, w_ref, wg_ref, o_ref, og_ref):
    @pl.when(pl.program_id(1) == 0)
    def _():
        og_ref[...] = jnp.dot(xn_ref[...], wg_ref[...], preferred_element_type=F32)

    o_ref[...] = jnp.dot(xn_ref[...], w_ref[...], preferred_element_type=F32)


def _in_proj(xn, w, wg, tm, tn):
    t, d = xn.shape
    n = w.shape[1]
    return pl.pallas_call(
        _in_proj_kernel,
        grid=(t // tm, n // tn),
        in_specs=[
            pl.BlockSpec((tm, d), lambda i, j: (i, 0)),
            pl.BlockSpec((d, tn), lambda i, j: (0, j)),
            pl.BlockSpec((d, LANES), lambda i, j: (0, 0)),
        ],
        out_specs=[
            pl.BlockSpec((tm, tn), lambda i, j: (i, j)),
            pl.BlockSpec((tm, LANES), lambda i, j: (i, 0)),
        ],
        out_shape=[jax.ShapeDtypeStruct((t, n), F32), jax.ShapeDtypeStruct((t, LANES), F32)],
        compiler_params=_cparams("parallel", "arbitrary"),
        name="in_proj",
    )(xn, w, wg)


def _s5_kernel(u_ref, h0r_ref, h0i_ref, lr_ref, li_ref, ldt_ref, bre_ref, bim_ref,
               cre_ref, cim_ref, d_ref,
               y_ref, pr_ref, pi_ref, sr_ref, si_ref,
               x_scr, init_scr, up_scr, yp_scr, *, nb, seq, ns, nm):
    sw = lr_ref.shape[1]
    tp = nb * seq
    nseg = SUBLANES
    ls = seq // nseg

    lr = lr_ref[...]
    li = li_ref[...]
    dt = jnp.exp(ldt_ref[...])
    mag = jnp.exp(lr * dt)
    ang = li * dt
    ar = mag * jnp.cos(ang)
    ai = mag * jnp.sin(ang)
    den = lr * lr + li * li
    nr = ar - 1.0
    qr = (nr * lr + ai * li) / den
    qi = (ai * lr - nr * li) / den
    bre = bre_ref[...]
    bim = bim_ref[...]
    bcat = jnp.concatenate([qr * bre - qi * bim, qr * bim + qi * bre], axis=1).astype(BF16)
    ccat = jnp.concatenate([cre_ref[...], -cim_ref[...]], axis=0).astype(BF16)
    dvec = d_ref[...]
    mag_s = jnp.exp(lr * dt * float(ls))
    asr = mag_s * jnp.cos(ang * float(ls))
    asi = mag_s * jnp.sin(ang * float(ls))

    def project_out(h, u):
        return jnp.dot(h.astype(BF16), ccat, preferred_element_type=F32) + dvec * u

    u_s = u_ref[tp:tp + ns, :]
    x_s = jnp.dot(u_s.astype(BF16), bcat, preferred_element_type=F32)
    h0r = h0r_ref[...]
    h0i = h0i_ref[...]
    hr_s = ar * h0r - ai * h0i + x_s[:, :sw]
    hi_s = ar * h0i + ai * h0r + x_s[:, sw:]
    sr_ref[...] = hr_s
    si_ref[...] = hi_s
    y_ref[tp:tp + ns, :] = project_out(jnp.concatenate([hr_s, hi_s], axis=1), u_s)
    y_ref[tp + ns:, :] = jnp.zeros((y_ref.shape[0] - tp - ns, y_ref.shape[1]), F32)

    u_m = u_ref[tp + ns:tp + ns + nm, :]
    x_m = jnp.dot(u_m.astype(BF16), bcat, preferred_element_type=F32)
    mr = jnp.zeros((1, sw), F32)
    mi = jnp.zeros((1, sw), F32)
    for t in range(nm):
        mr, mi = (ar * mr - ai * mi + x_m[t:t + 1, :sw],
                  ar * mi + ai * mr + x_m[t:t + 1, sw:])

    ar8 = jnp.broadcast_to(ar, (nseg, sw))
    ai8 = jnp.broadcast_to(ai, (nseg, sw))
    unroll = 4 if ls % 4 == 0 else 1

    def scan_pass(store):
        def body(i, c):
            hr, hi = c
            r = pl.multiple_of(i * nseg, nseg)
            nhr = ar8 * hr - ai8 * hi + x_scr[pl.ds(r, nseg), 0:sw]
            nhi = ar8 * hi + ai8 * hr + x_scr[pl.ds(r, nseg), sw:2 * sw]
            if store:
                x_scr[pl.ds(r, nseg), 0:sw] = nhr
                x_scr[pl.ds(r, nseg), sw:2 * sw] = nhi
            return nhr, nhi
        return body

    for b in range(nb):
        def regroup_in(i, c, b=b):
            r = pl.multiple_of(i * nseg, nseg)
            up_scr[pl.ds(r, nseg), :] = u_ref[pl.ds(b * seq + i, nseg, stride=ls), :]
            return c
        lax.fori_loop(0, ls, regroup_in, 0, unroll=unroll)
        u_b = up_scr[...]
        x_scr[...] = jnp.dot(u_b.astype(BF16), bcat, preferred_element_type=F32)
        z = jnp.zeros((nseg, sw), F32)
        fr, fi = lax.fori_loop(0, ls, scan_pass(False), (z, z), unroll=unroll)
        cr, ci = mr, mi
        for k in range(nseg):
            init_scr[k:k + 1, 0:sw] = cr
            init_scr[k:k + 1, sw:2 * sw] = ci
            cr, ci = (asr * cr - asi * ci + fr[k:k + 1, :],
                      asr * ci + asi * cr + fi[k:k + 1, :])
        pr_ref[b:b + 1, :] = cr
        pi_ref[b:b + 1, :] = ci
        lax.fori_loop(0, ls, scan_pass(True), (init_scr[:, 0:sw], init_scr[:, sw:2 * sw]),
                      unroll=unroll)
        yp_scr[...] = project_out(x_scr[...], u_b)

        def regroup_out(i, c, b=b):
            r = pl.multiple_of(i * nseg, nseg)
            y_ref[pl.ds(b * seq + i, nseg, stride=ls), :] = yp_scr[pl.ds(r, nseg), :]
            return c
        lax.fori_loop(0, ls, regroup_out, 0, unroll=unroll)


def _s5(proj, h0r, h0i, lam_re, lam_im, logdt, bre_bd, bim_bd, cre_bd, cim_bd, dvec,
        nb, seq, ns, nm):
    t_all = proj.shape[0]
    nslab, cw, sw = bre_bd.shape
    d_ssm = nslab * cw
    kern = functools.partial(_s5_kernel, nb=nb, seq=seq, ns=ns, nm=nm)
    return pl.pallas_call(
        kern,
        grid=(nslab,),
        in_specs=[
            pl.BlockSpec((t_all, cw), lambda s: (0, s)),
            pl.BlockSpec((ns, sw), lambda s: (0, s)),
            pl.BlockSpec((ns, sw), lambda s: (0, s)),
            pl.BlockSpec((1, sw), lambda s: (0, s)),
            pl.BlockSpec((1, sw), lambda s: (0, s)),
            pl.BlockSpec((1, sw), lambda s: (0, s)),
            pl.BlockSpec((None, cw, sw), lambda s: (s, 0, 0)),
            pl.BlockSpec((None, cw, sw), lambda s: (s, 0, 0)),
            pl.BlockSpec((None, sw, cw), lambda s: (s, 0, 0)),
            pl.BlockSpec((None, sw, cw), lambda s: (s, 0, 0)),
            pl.BlockSpec((1, cw), lambda s: (0, s)),
        ],
        out_specs=[
            pl.BlockSpec((t_all, cw), lambda s: (0, s)),
            pl.BlockSpec((nb, sw), lambda s: (0, s)),
            pl.BlockSpec((nb, sw), lambda s: (0, s)),
            pl.BlockSpec((ns, sw), lambda s: (0, s)),
            pl.BlockSpec((ns, sw), lambda s: (0, s)),
        ],
        out_shape=[
            jax.ShapeDtypeStruct((t_all, d_ssm), F32),
            jax.ShapeDtypeStruct((nb, nslab * sw), F32),
            jax.ShapeDtypeStruct((nb, nslab * sw), F32),
            jax.ShapeDtypeStruct((ns, nslab * sw), F32),
            jax.ShapeDtypeStruct((ns, nslab * sw), F32),
        ],
        scratch_shapes=[pltpu.VMEM((seq, 2 * sw), F32), pltpu.VMEM((SUBLANES, 2 * sw), F32),
                        pltpu.VMEM((seq, cw), F32), pltpu.VMEM((seq, cw), F32)],
        compiler_params=_cparams("parallel"),
        name="s5_scan",
    )(proj, h0r, h0i, lam_re, lam_im, logdt, bre_bd, bim_bd, cre_bd, cim_bd, dvec)


def _glu_norm_kernel(y_ref, w_ref, b_ref, g_ref, o_ref):
    z = _gelu(y_ref[...])
    gate = jnp.dot(z.astype(BF16), w_ref[...], preferred_element_type=F32) + b_ref[...]
    zz = z * _sigmoid(gate)
    s = lax.rsqrt(jnp.mean(zz * zz, axis=-1, keepdims=True) + EPS)
    o_ref[...] = (zz * s * g_ref[...]).astype(o_ref.dtype)


def _glu_norm(y, w, b, g, tm):
    t, d = y.shape
    return pl.pallas_call(
        _glu_norm_kernel,
        grid=(t // tm,),
        in_specs=[
            pl.BlockSpec((tm, d), lambda i: (i, 0)),
            pl.BlockSpec((d, d), lambda i: (0, 0)),
            pl.BlockSpec((1, d), lambda i: (0, 0)),
            pl.BlockSpec((1, d), lambda i: (0, 0)),
        ],
        out_specs=pl.BlockSpec((tm, d), lambda i: (i, 0)),
        out_shape=jax.ShapeDtypeStruct((t, d), BF16),
        compiler_params=_cparams("parallel"),
        name="s5_glu_norm",
    )(y, w, b, g)


def _cumsum_rows(x):
    n = x.shape[0]
    row = lax.broadcasted_iota(jnp.int32, x.shape, 0)
    s = 1
    while s < n:
        x = x + jnp.where(row >= s, pltpu.roll(x, s, axis=0), 0.0)
        s *= 2
    return x


def _gate(xg, w2, b2):
    pre = jnp.dot(xg.astype(BF16), w2, preferred_element_type=F32) + b2
    return _log_sigmoid(pre) * (1.0 / GLA_TAU)


def _head_out(o, r, g):
    o = o * lax.rsqrt(jnp.mean(o * o, axis=-1, keepdims=True) + EPS)
    return o * g * (r * _sigmoid(r))


def _gla_prompt_kernel(q_ref, k_ref, v_ref, r_ref, xg_ref, kt_ref, vt_ref, xgt_ref,
                       w2_ref, b2_ref, g_ref, o_ref, s_ref, st_scr, *, seq, ns, nm):
    dk = q_ref.shape[1]
    scale = float(dk) ** -0.5
    w2 = w2_ref[...]
    b2 = b2_ref[...]
    g = g_ref[...]

    lgm = _gate(xgt_ref[ns:ns + nm, :], w2, b2)
    bm = _cumsum_rows(lgm)
    kdm = kt_ref[ns:ns + nm, :] * jnp.exp(bm[nm - 1:nm, :] - bm)
    st_scr[...] = lax.dot_general(vt_ref[ns:ns + nm, :].astype(BF16), kdm.astype(BF16),
                                  (((0,), (0,)), ((), ())), preferred_element_type=F32)

    c = GLA_CHUNK
    nsub = c // GLA_SUB

    def chunk(ci, carry):
        r0 = pl.multiple_of(ci * c, c)
        q = q_ref[pl.ds(r0, c), :] * scale
        k = k_ref[pl.ds(r0, c), :]
        v = v_ref[pl.ds(r0, c), :]
        vb = v.astype(BF16)
        lg = _gate(xg_ref[pl.ds(r0, c), :], w2, b2)
        b = _cumsum_rows(lg)
        bl = b[c - 1:c, :]
        st = st_scr[...]
        o_inter = lax.dot_general((q * jnp.exp(b)).astype(BF16), st.astype(BF16),
                                  (((1,), (1,)), ((), ())), preferred_element_type=F32)
        outs = []
        for sb in range(nsub):
            lo = sb * GLA_SUB
            hi = lo + GLA_SUB
            beta = b[lo - 1:lo, :] if sb > 0 else jnp.zeros((1, dk), F32)
            qs = q[lo:hi, :] * jnp.exp(b[lo:hi, :] - beta)
            ks = k[0:hi, :] * jnp.exp(beta - b[0:hi, :])
            sc = lax.dot_general(qs.astype(BF16), ks.astype(BF16),
                                 (((1,), (1,)), ((), ())), preferred_element_type=F32)
            rowi = lax.broadcasted_iota(jnp.int32, (GLA_SUB, hi), 0)
            coli = lax.broadcasted_iota(jnp.int32, (GLA_SUB, hi), 1)
            sc = jnp.where(coli <= rowi + lo, sc, 0.0)
            outs.append(jnp.dot(sc.astype(BF16), vb[0:hi, :], preferred_element_type=F32))
        o = o_inter + jnp.concatenate(outs, axis=0)
        o_ref[pl.ds(r0, c), :] = _head_out(o, r_ref[pl.ds(r0, c), :], g).astype(o_ref.dtype)
        kd = k * jnp.exp(bl - b)
        st_scr[...] = jnp.exp(bl) * st + lax.dot_general(
            vb, kd.astype(BF16), (((0,), (0,)), ((), ())), preferred_element_type=F32)
        return carry

    lax.fori_loop(0, seq // c, chunk, 0)
    s_ref[...] = st_scr[...].T


def _gla_prompt(proj, xg, w2, b2, g, nb, seq, ns, nm, nh, dk, dv, q0, k0, v0, r0):
    tp = nb * seq
    tail_blk = tp // TAIL_ROWS
    rb = seq
    kern = functools.partial(_gla_prompt_kernel, seq=seq, ns=ns, nm=nm)
    return pl.pallas_call(
        kern,
        grid=(nb, nh),
        in_specs=[
            pl.BlockSpec((rb, dk), lambda b, h: (b, q0 // dk + h)),
            pl.BlockSpec((rb, dk), lambda b, h: (b, k0 // dk + h)),
            pl.BlockSpec((rb, dv), lambda b, h: (b, v0 // dv + h)),
            pl.BlockSpec((rb, dv), lambda b, h: (b, r0 // dv + h)),
            pl.BlockSpec((rb, LANES), lambda b, h: (b, 0)),
            pl.BlockSpec((TAIL_ROWS, dk), lambda b, h: (tail_blk, k0 // dk + h)),
            pl.BlockSpec((TAIL_ROWS, dv), lambda b, h: (tail_blk, v0 // dv + h)),
            pl.BlockSpec((TAIL_ROWS, LANES), lambda b, h: (tail_blk, 0)),
            pl.BlockSpec((LANES, dk), lambda b, h: (0, h)),
            pl.BlockSpec((1, dk), lambda b, h: (0, h)),
            pl.BlockSpec((1, dv), lambda b, h: (0, h)),
        ],
        out_specs=[
            pl.BlockSpec((rb, dv), lambda b, h: (b, h)),
            pl.BlockSpec((None, None, dk, dv), lambda b, h: (b, h, 0, 0)),
        ],
        out_shape=[
            jax.ShapeDtypeStruct((tp, nh * dv), BF16),
            jax.ShapeDtypeStruct((nb, nh, dk, dv), F32),
        ],
        scratch_shapes=[pltpu.VMEM((dv, dk), F32)],
        compiler_params=_cparams("parallel", "parallel"),
        name="gla_prompt",
    )(proj, proj, proj, proj, xg, proj, proj, xg, w2, b2, g)


def _gla_sample_kernel(q_ref, k_ref, v_ref, r_ref, xg_ref, w2_ref, b2_ref, g_ref, s0_ref,
                       o_ref, s_ref, *, nh, dk, dv):
    n = pl.program_id(0)
    j = n % SUBLANES
    scale = float(dk) ** -0.5
    lg_all = _gate(xg_ref[pl.ds(j, 1), :], w2_ref[...], b2_ref[...])
    q_all = q_ref[pl.ds(j, 1), :] * scale
    k_all = k_ref[pl.ds(j, 1), :]
    v_all = v_ref[pl.ds(j, 1), :]
    r_all = r_ref[pl.ds(j, 1), :]
    g_all = g_ref[...]
    rows = 2 * SUBLANES
    rk = lax.broadcasted_iota(jnp.int32, (rows, dk), 0)
    rv = lax.broadcasted_iota(jnp.int32, (rows, dv), 0)
    outs = []
    for h in range(nh):
        lg = lg_all[:, h * dk:(h + 1) * dk]
        q = q_all[:, h * dk:(h + 1) * dk]
        k = k_all[:, h * dk:(h + 1) * dk]
        v = v_all[:, h * dv:(h + 1) * dv]
        e = jnp.exp(lg)
        s0 = s0_ref[h]
        e_hi = e.astype(BF16)
        e_mid = (e - e_hi.astype(F32)).astype(BF16)
        e_lo = (e - e_hi.astype(F32) - e_mid.astype(F32)).astype(BF16)
        def rows_of(x, w):
            return jnp.broadcast_to(x.astype(F32), (rows, w))
        lhs = jnp.where(rk == 0, rows_of(e_hi, dk),
              jnp.where(rk == 1, rows_of(e_mid, dk),
              jnp.where(rk == 2, rows_of(e_lo, dk),
              jnp.where(rk == 3, rows_of(k, dk), 0.0)))).astype(BF16)
        ones_part = jnp.where(rv < 3, 1.0, 0.0)
        v_part = jnp.where(rv == 3, rows_of(v, dv), 0.0)
        rhs = jnp.concatenate([ones_part, v_part], axis=1).astype(BF16)
        both = lax.dot_general(lhs, rhs, (((0,), (0,)), ((), ())), preferred_element_type=F32)
        s_ref[h] = both[:, :dv] * s0 + both[:, dv:]
        qe = jnp.broadcast_to((q * e).astype(BF16), (rows, dk))
        o = jnp.dot(qe, s0.astype(BF16), preferred_element_type=F32)[0:1, :]
        o = o + jnp.sum(q * k, axis=-1, keepdims=True) * v
        outs.append(_head_out(o, r_all[:, h * dv:(h + 1) * dv], g_all[:, h * dv:(h + 1) * dv]))
    o_ref[pl.ds(j, 1), :] = jnp.concatenate(outs, axis=1)


def _gla_sample(proj, xg, w2, b2, g, s0, tp, ns, nh, dk, dv, q0, k0, v0, r0):
    rb = SUBLANES
    base = tp // rb
    qk_w = nh * dk
    v_w = nh * dv
    kern = functools.partial(_gla_sample_kernel, nh=nh, dk=dk, dv=dv)
    return pl.pallas_call(
        kern,
        grid=(ns,),
        in_specs=[
            pl.BlockSpec((rb, qk_w), lambda n: (base + n // rb, q0 // qk_w)),
            pl.BlockSpec((rb, qk_w), lambda n: (base + n // rb, k0 // qk_w)),
            pl.BlockSpec((rb, v_w), lambda n: (base + n // rb, v0 // v_w)),
            pl.BlockSpec((rb, v_w), lambda n: (base + n // rb, r0 // v_w)),
            pl.BlockSpec((rb, LANES), lambda n: (base + n // rb, 0)),
            pl.BlockSpec((LANES, qk_w), lambda n: (0, 0)),
            pl.BlockSpec((1, qk_w), lambda n: (0, 0)),
            pl.BlockSpec((1, v_w), lambda n: (0, 0)),
            pl.BlockSpec((None, nh, dk, dv), lambda n: (n, 0, 0, 0)),
        ],
        out_specs=[
            pl.BlockSpec((rb, v_w), lambda n: (n // rb, 0)),
            pl.BlockSpec((None, nh, dk, dv), lambda n: (n, 0, 0, 0)),
        ],
        out_shape=[
            jax.ShapeDtypeStruct((ns, v_w), F32),
            jax.ShapeDtypeStruct((ns, nh, dk, dv), F32),
        ],
        compiler_params=_cparams("arbitrary"),
        name="gla_sample",
    )(proj, proj, proj, proj, xg, w2, b2, g, s0)


def _out_proj_kernel(ys_ref, og_ref, w1_ref, w2_ref, m_ref):
    acc = jnp.dot(ys_ref[...], w1_ref[...], preferred_element_type=F32)
    m_ref[...] = acc + jnp.dot(og_ref[...], w2_ref[...], preferred_element_type=F32)


def _out_proj(ys, og, w1, w2, tm, tn):
    t, dh = ys.shape
    d = w1.shape[1]
    return pl.pallas_call(
        _out_proj_kernel,
        grid=(t // tm, d // tn),
        in_specs=[
            pl.BlockSpec((tm, dh), lambda i, j: (i, 0)),
            pl.BlockSpec((tm, dh), lambda i, j: (i, 0)),
            pl.BlockSpec((dh, tn), lambda i, j: (0, j)),
            pl.BlockSpec((dh, tn), lambda i, j: (0, j)),
        ],
        out_specs=pl.BlockSpec((tm, tn), lambda i, j: (i, j)),
        out_shape=jax.ShapeDtypeStruct((t, d), F32),
        compiler_params=_cparams("parallel", "parallel"),
        name="out_proj",
    )(ys, og, w1, w2)


def _resid_norm_t_kernel(m_ref, xp_ref, xt_ref, g_ref, h_ref, o_ref, *, nprompt):
    i = pl.program_id(0)

    def emit(x):
        h = x + m_ref[...]
        h_ref[...] = h
        o_ref[...] = _rms(h, g_ref[...]).T.astype(BF16)

    @pl.when(i < nprompt)
    def _():
        emit(xp_ref[...])

    @pl.when(i >= nprompt)
    def _():
        emit(xt_ref[...])


def _resid_norm_t(mix, x_p, x_t, g):
    t, d = mix.shape
    tr = x_t.shape[0]
    nprompt = x_p.shape[0] // tr
    return pl.pallas_call(
        functools.partial(_resid_norm_t_kernel, nprompt=nprompt),
        grid=(nprompt + 1,),
        in_specs=[
            pl.BlockSpec((tr, d), lambda i: (i, 0)),
            pl.BlockSpec((tr, d), lambda i: (jnp.minimum(i, nprompt - 1), 0)),
            pl.BlockSpec((tr, d), lambda i: (0, 0)),
            pl.BlockSpec((1, d), lambda i: (0, 0)),
        ],
        out_specs=[pl.BlockSpec((tr, d), lambda i: (i, 0)), pl.BlockSpec((d, tr), lambda i: (0, i))],
        out_shape=[jax.ShapeDtypeStruct((t, d), F32), jax.ShapeDtypeStruct((d, t), BF16)],
        compiler_params=_cparams("arbitrary"),
        name="resid_ffn_norm_t",
    )(mix, x_p, x_t, g)


def _topk_rows(x, k):
    outs = []
    for _ in range(k):
        m = jnp.max(x, axis=0, keepdims=True)
        outs.append(m)
        x = jnp.where(x == m, -jnp.inf, x)
    return outs


def _route_kernel(wq_ref, hn_ref, k1_ref, k2_ref, s1_ref, c1_ref, s2_ref, e2_ref, tau_ref):
    half = k1_ref.shape[1]
    tb = hn_ref.shape[1]
    qt = jnp.dot(wq_ref[...], hn_ref[...], preferred_element_type=F32)
    s1 = jnp.dot(k1_ref[...].astype(BF16), qt[:half, :].astype(BF16), preferred_element_type=F32)
    s2 = jnp.dot(k2_ref[...].astype(BF16), qt[half:, :].astype(BF16), preferred_element_type=F32)
    s1_ref[...] = s1
    s2_ref[...] = s2
    for t in range(tb // LANES):
        sl = slice(t * LANES, (t + 1) * LANES)
        a = s1[:, sl]
        b = s2[:, sl]
        v1 = _topk_rows(a, PEER_TOPK)
        v2 = jnp.concatenate(_topk_rows(b, PEER_TOPK), axis=0)
        cand = jnp.concatenate([v + v2 for v in v1], axis=0)
        top = _topk_rows(cand, PEER_TOPK)
        z = jnp.zeros_like(top[0])
        for c in top:
            z = z + jnp.exp(c - top[0])
        tau_ref[:, sl] = top[PEER_TOPK - 1]
        c1_ref[:, sl] = jnp.exp(a - v1[0]) / z
        e2_ref[:, sl] = jnp.exp(b - v2[0:1, :])


def _route(wq_t, hn_t, k1, k2, tb):
    d, t = hn_t.shape
    nh, nk, half = k1.shape
    qd = 2 * half
    tab = jax.ShapeDtypeStruct((nh, nk, t), F32)
    tab_spec = pl.BlockSpec((None, nk, tb), lambda i, h: (h, 0, i))
    return pl.pallas_call(
        _route_kernel,
        grid=(t // tb, nh),
        in_specs=[
            pl.BlockSpec((qd, d), lambda i, h: (h, 0)),
            pl.BlockSpec((d, tb), lambda i, h: (0, i)),
            pl.BlockSpec((None, nk, half), lambda i, h: (h, 0, 0)),
            pl.BlockSpec((None, nk, half), lambda i, h: (h, 0, 0)),
        ],
        out_specs=[tab_spec, tab_spec, tab_spec, tab_spec,
                   pl.BlockSpec((None, 1, tb), lambda i, h: (h, 0, i))],
        out_shape=[tab, tab, tab, tab, jax.ShapeDtypeStruct((nh, 1, t), F32)],
        compiler_params=_cparams("parallel", "parallel"),
        name="peer_route",
    )(wq_t, hn_t, k1, k2)


GATE_ROWS = 32
OUT_ROWS = 512


K_CHUNK = 512


def _peer_kernel(u_ref, vt_ref, hn_ref, s1_ref, c1_ref, s2_ref, e2_ref, tau_ref, o_ref,
                 w_scr, act_scr, a_scr):
    eb, d = u_ref.shape
    tb = hn_ref.shape[1]
    nh, nk, _ = s2_ref.shape
    nr = eb // nk

    @pl.when(pl.program_id(1) == 0)
    def _():
        o_ref[...] = jnp.zeros_like(o_ref)

    def gate_group(t, q):
        sl = slice(t * LANES, (t + 1) * LANES)
        rows = slice(q * GATE_ROWS, (q + 1) * GATE_ROWS)
        accs = [None] * nr
        for h in range(nh):
            s2t = s2_ref[h, rows, sl]
            e2t = e2_ref[h, rows, sl]
            tau = tau_ref[h, :, sl]
            for r in range(nr):
                ssum = s1_ref[r, h:h + 1, sl] + s2t
                term = jnp.where(ssum >= tau, c1_ref[r, h:h + 1, sl] * e2t, 0.0)
                accs[r] = term if accs[r] is None else accs[r] + term
        for r in range(nr):
            lo = r * nk + q * GATE_ROWS
            w_scr[lo:lo + GATE_ROWS, sl] = accs[r]

    groups = [(t, q) for t in range(tb // LANES) for q in range(nk // GATE_ROWS)]
    nkc = d // K_CHUNK
    per = -(-len(groups) // nkc)
    for kc in range(nkc):
        @pl.when(pl.program_id(1) >= 0)
        def _(kc=kc):
            ks = slice(kc * K_CHUNK, (kc + 1) * K_CHUNK)
            part = jnp.dot(u_ref[:, ks], hn_ref[ks, :], preferred_element_type=F32)
            if kc == 0:
                act_scr[...] = part
            else:
                act_scr[...] += part
            for t, q in groups[kc * per:(kc + 1) * per]:
                gate_group(t, q)
    a_scr[...] = (w_scr[...] * _gelu(act_scr[...])).astype(BF16)
    for dc in range(d // OUT_ROWS):
        dr = slice(dc * OUT_ROWS, (dc + 1) * OUT_ROWS)
        o_ref[dr, :] += jnp.dot(vt_ref[dr, :], a_scr[...], preferred_element_type=F32)


def _peer(u_bf, v_t, hn_t, s1, c1, s2, e2, tau, tb, eb):
    ne, d = u_bf.shape
    t = hn_t.shape[1]
    nh, nk, _ = s2.shape
    once = pl.Buffered(1)
    tab_spec = pl.BlockSpec((nh, nk, tb), lambda i, e: (0, 0, i), pipeline_mode=once)
    row_spec = pl.BlockSpec((eb // nk, nh, tb), lambda i, e: (e, 0, i))
    return pl.pallas_call(
        _peer_kernel,
        grid=(t // tb, ne // eb),
        in_specs=[
            pl.BlockSpec((eb, d), lambda i, e: (e, 0)),
            pl.BlockSpec((d, eb), lambda i, e: (0, e)),
            pl.BlockSpec((d, tb), lambda i, e: (0, i), pipeline_mode=once),
            row_spec, row_spec, tab_spec, tab_spec,
            pl.BlockSpec((nh, 1, tb), lambda i, e: (0, 0, i)),
        ],
        out_specs=pl.BlockSpec((d, tb), lambda i, e: (0, i)),
        out_shape=jax.ShapeDtypeStruct((d, t), F32),
        scratch_shapes=[pltpu.VMEM((eb, tb), F32), pltpu.VMEM((eb, tb), F32),
                        pltpu.VMEM((eb, tb), BF16)],
        compiler_params=pltpu.CompilerParams(
            dimension_semantics=("parallel", "arbitrary"), vmem_limit_bytes=PEER_VMEM_LIMIT),
        name="peer_experts",
    )(u_bf, v_t, hn_t, s1, c1, s2, e2, tau)


def _final_kernel(pt_ref, h_ref, g_ref, yp_ref, yt_ref, *, nprompt):
    i = pl.program_id(0)
    y = _rms(h_ref[...] + pt_ref[...].T, g_ref[...])

    @pl.when(i < nprompt)
    def _():
        yp_ref[...] = y

    @pl.when(i >= nprompt)
    def _():
        yt_ref[...] = y


def _final(peer_t, h, g, tp, tr):
    d, t = peer_t.shape
    nprompt = tp // tr
    return pl.pallas_call(
        functools.partial(_final_kernel, nprompt=nprompt),
        grid=(nprompt + 1,),
        in_specs=[
            pl.BlockSpec((d, tr), lambda i: (0, i)),
            pl.BlockSpec((tr, d), lambda i: (i, 0)),
            pl.BlockSpec((1, d), lambda i: (0, 0)),
        ],
        out_specs=[
            pl.BlockSpec((tr, d), lambda i: (jnp.minimum(i, nprompt - 1), 0)),
            pl.BlockSpec((tr, d), lambda i: (0, 0)),
        ],
        out_shape=[jax.ShapeDtypeStruct((tp, d), F32), jax.ShapeDtypeStruct((tr, d), F32)],
        compiler_params=_cparams("arbitrary"),
        name="final_norm",
    )(peer_t, h, g)


def _block_diag(w, ngrp):
    nslab, _, a, b = w.shape
    eye = jnp.eye(ngrp, dtype=w.dtype)
    full = w[:, :, :, None, :] * eye[None, :, None, :, None]
    return full.reshape(nslab, ngrp * a, ngrp * b)


def kernel(x_prompt, x_sample, state_s5_re, state_s5_im, state_gla, meta_tokens, norm_mix_g, w_in, s5_lam_re, s5_lam_im, s5_log_dt, s5_b_re, s5_b_im, s5_c_re, s5_c_im, s5_d, s5_w_glu, s5_b_glu, s5_norm_g, gla_w_gate2, gla_b_gate2, gla_norm_g, w_out, norm_ffn_g, peer_w_q, peer_keys, peer_u, peer_v, norm_final_g):
    nb, seq, d = x_prompt.shape
    ns = x_sample.shape[0]
    nm = meta_tokens.shape[0]
    depth = w_in.shape[0]
    assert depth == 1 and x_sample.shape[1] == 1
    tp = nb * seq
    assert tp % TAIL_ROWS == 0 and ns + nm <= TAIL_ROWS and ns % SUBLANES == 0
    assert seq % GLA_CHUNK == 0 and nm % SUBLANES == 0
    t_all = tp + TAIL_ROWS

    ngrp, nstate, gch = s5_b_re.shape[1:]
    d_ssm = ngrp * gch
    gps = LANES // gch
    nslab = ngrp // gps
    nh, dk, dv = state_gla.shape[2:]
    d_gla = nh * dv
    rank = gla_w_gate2.shape[1]
    q0 = d_ssm
    k0 = q0 + nh * dk
    v0 = k0 + nh * dk
    r0 = v0 + d_gla
    g0 = r0 + d_gla
    assert g0 + rank == w_in.shape[2] and d_ssm + d_gla == w_out.shape[1]

    x_p = x_prompt.reshape(tp, d)
    x_t = jnp.concatenate([x_sample.reshape(ns, d), meta_tokens,
                           jnp.zeros((TAIL_ROWS - ns - nm, d), F32)], axis=0)
    w_main = w_in[0, :, :g0].astype(BF16)
    w_g1 = jnp.pad(w_in[0, :, g0:], ((0, 0), (0, LANES - rank))).astype(BF16)
    w_g2 = jnp.pad(gla_w_gate2[0], ((0, LANES - rank), (0, 0))).astype(BF16)
    lam_re = s5_lam_re[0].reshape(1, ngrp * nstate)
    lam_im = s5_lam_im[0].reshape(1, ngrp * nstate)
    logdt = jnp.repeat(s5_log_dt[0], nstate).reshape(1, ngrp * nstate)
    b_re4 = jnp.transpose(s5_b_re[0].reshape(nslab, gps, nstate, gch), (0, 1, 3, 2))
    b_im4 = jnp.transpose(s5_b_im[0].reshape(nslab, gps, nstate, gch), (0, 1, 3, 2))
    c_re4 = jnp.transpose(s5_c_re[0].reshape(nslab, gps, gch, nstate), (0, 1, 3, 2))
    c_im4 = jnp.transpose(s5_c_im[0].reshape(nslab, gps, gch, nstate), (0, 1, 3, 2))
    bre_bd = _block_diag(b_re4, gps)
    bim_bd = _block_diag(b_im4, gps)
    cre_bd = _block_diag(c_re4, gps)
    cim_bd = _block_diag(c_im4, gps)
    h0r = state_s5_re[0].reshape(ns, ngrp * nstate)
    h0i = state_s5_im[0].reshape(ns, ngrp * nstate)

    tm = _row_tile(t_all, (768, 512, 256))

    xn = _prenorm(x_p, x_t, norm_mix_g)
    proj, xg = _in_proj(xn, w_main, w_g1, tm, 512)

    y_raw, pr, pi_, sr, si = _s5(proj, h0r, h0i, lam_re, lam_im, logdt, bre_bd, bim_bd,
                                 cre_bd, cim_bd, s5_d, nb, seq, ns, nm)
    y_ssm = _glu_norm(y_raw, s5_w_glu[0].astype(BF16), s5_b_glu, s5_norm_g, TAIL_ROWS)

    og_p, gla_p = _gla_prompt(proj, xg, w_g2, gla_b_gate2, gla_norm_g, nb, seq, ns, nm,
                              nh, dk, dv, q0, k0, v0, r0)
    og_s, gla_s = _gla_sample(proj, xg, w_g2, gla_b_gate2, gla_norm_g, state_gla[0],
                              tp, ns, nh, dk, dv, q0, k0, v0, r0)
    og = jnp.concatenate([og_p, og_s.astype(BF16),
                          jnp.zeros((TAIL_ROWS - ns, d_gla), BF16)], axis=0)

    w_o = w_out[0].astype(BF16)
    mix = _out_proj(y_ssm, og, w_o[:d_ssm], w_o[d_ssm:], tm, 512)

    h, hn_t = _resid_norm_t(mix, x_p, x_t, norm_ffn_g)
    wq_t = peer_w_q[0].T.astype(BF16)
    s1, c1, s2, e2, tau = _route(wq_t, hn_t, peer_keys[0, :, 0], peer_keys[0, :, 1], tm)
    u_bf = peer_u[0].astype(BF16)
    v_t = peer_v[0].astype(BF16).T
    peer_t = _peer(u_bf, v_t, hn_t, jnp.transpose(s1, (1, 0, 2)), jnp.transpose(c1, (1, 0, 2)),
                   s2, e2, tau, tm, 512)
    y_p, y_t = _final(peer_t, h, norm_final_g.reshape(1, d), tp, TAIL_ROWS)

    y_prompt = y_p.reshape(nb, seq, d)
    y_sample = y_t[:ns].reshape(ns, 1, d)
    return (y_prompt, y_sample,
            pr.reshape(1, nb, ngrp, nstate), pi_.reshape(1, nb, ngrp, nstate), gla_p[None],
            sr.reshape(1, ns, ngrp, nstate), si.reshape(1, ns, ngrp, nstate), gla_s[None])
```

```python
import functools

import jax
import jax.numpy as jnp
from jax import lax
from jax.experimental import pallas as pl
from jax.experimental.pallas import tpu as pltpu

F32 = jnp.float32
BF16 = jnp.bfloat16

EPS = 1e-6
GLA_TAU = 16.0
GLA_CHUNK = 64
GLA_SUB = 16
PEER_TOPK = 16
LANES = 128
SUBLANES = 8
TAIL_ROWS = 256
PROJ_COLS = 1024
VMEM_LIMIT = 56 * 1024 * 1024
PEER_VMEM_LIMIT = 60 * 1024 * 1024


def _cparams(*sem):
    return pltpu.CompilerParams(dimension_semantics=sem, vmem_limit_bytes=VMEM_LIMIT)


def _gelu(x):
    return 0.5 * x * (1.0 + jnp.tanh(0.7978845608028654 * (x + 0.044715 * (x * x * x))))


def _sigmoid(x):
    return 1.0 / (1.0 + jnp.exp(-x))


def _log_sigmoid(x):
    return jnp.minimum(x, 0.0) - jnp.log(1.0 + jnp.exp(-jnp.abs(x)))


def _row_tile(n, cands):
    for c in cands:
        if n % c == 0:
            return c
    raise ValueError(f"no row tile for {n}")


def _rms(x, g):
    return x * lax.rsqrt(jnp.mean(x * x, axis=-1, keepdims=True) + EPS) * g


def _prenorm_kernel(xp_ref, xt_ref, g_ref, o_ref, *, nprompt):
    i = pl.program_id(0)

    @pl.when(i < nprompt)
    def _():
        o_ref[...] = _rms(xp_ref[...], g_ref[...]).astype(BF16)

    @pl.when(i >= nprompt)
    def _():
        o_ref[...] = _rms(xt_ref[...], g_ref[...]).astype(BF16)


def _prenorm(x_p, x_t, g):
    tp, d = x_p.shape
    tr = x_t.shape[0]
    nprompt = tp // tr
    return pl.pallas_call(
        functools.partial(_prenorm_kernel, nprompt=nprompt),
        grid=(nprompt + 1,),
        in_specs=[
            pl.BlockSpec((tr, d), lambda i: (jnp.minimum(i, nprompt - 1), 0)),
            pl.BlockSpec((tr, d), lambda i: (0, 0)),
            pl.BlockSpec((1, d), lambda i: (0, 0)),
        ],
        out_specs=pl.BlockSpec((tr, d), lambda i: (i, 0)),
        out_shape=jax.ShapeDtypeStruct((tp + tr, d), BF16),
        compiler_params=_cparams("arbitrary"),
        name="mix_norm",
    )(x_p, x_t, g)


def _in_proj_kernel(xn_ref, w_ref, wg_ref, o_ref, og_ref):
    @pl.when(pl.program_id(1) == 0)
    def _():
        og_ref[...] = jnp.dot(xn_ref[...], wg_ref[...], preferred_element_type=F32)

    o_ref[...] = jnp.dot(xn_ref[...], w_ref[...], preferred_element_type=F32)


def _in_proj(xn, w, wg, tm, tn, n):
    t, d = xn.shape
    return pl.pallas_call(
        _in_proj_kernel,
        grid=(t // tm, n // tn),
        in_specs=[
            pl.BlockSpec((tm, d), lambda i, j: (i, 0)),
            pl.BlockSpec((d, tn), lambda i, j: (0, j)),
            pl.BlockSpec((d, LANES), lambda i, j: (0, 0)),
        ],
        out_specs=[
            pl.BlockSpec((tm, tn), lambda i, j: (i, j)),
            pl.BlockSpec((tm, LANES), lambda i, j: (i, 0)),
        ],
        out_shape=[jax.ShapeDtypeStruct((t, n), F32), jax.ShapeDtypeStruct((t, LANES), F32)],
        compiler_params=_cparams("parallel", "arbitrary"),
        name="in_proj",
    )(xn, w, wg)


def _s5_kernel(u_ref, h0r_ref, h0i_ref, lr_ref, li_ref, ldt_ref, bre_ref, bim_ref,
               cre_ref, cim_ref, d_ref,
               y_ref, pr_ref, pi_ref, sr_ref, si_ref,
               x_scr, init_scr, up_scr, yp_scr, *, nb, seq, ns, nm):
    sw = lr_ref.shape[1]
    tp = nb * seq
    nseg = SUBLANES
    ls = seq // nseg

    lr = lr_ref[...]
    li = li_ref[...]
    dt = jnp.exp(ldt_ref[...])
    mag = jnp.exp(lr * dt)
    ang = li * dt
    ar = mag * jnp.cos(ang)
    ai = mag * jnp.sin(ang)
    den = lr * lr + li * li
    nr = ar - 1.0
    qr = (nr * lr + ai * li) / den
    qi = (ai * lr - nr * li) / den
    bre = bre_ref[...]
    bim = bim_ref[...]
    bcat = jnp.concatenate([qr * bre - qi * bim, qr * bim + qi * bre], axis=1).astype(BF16)
    ccat = jnp.concatenate([cre_ref[...], -cim_ref[...]], axis=0).astype(BF16)
    dvec = d_ref[...]
    mag_s = jnp.exp(lr * dt * float(ls))
    asr = mag_s * jnp.cos(ang * float(ls))
    asi = mag_s * jnp.sin(ang * float(ls))

    def project_out(h, u):
        return jnp.dot(h.astype(BF16), ccat, preferred_element_type=F32) + dvec * u

    u_s = u_ref[tp:tp + ns, :]
    x_s = jnp.dot(u_s.astype(BF16), bcat, preferred_element_type=F32)
    h0r = h0r_ref[...]
    h0i = h0i_ref[...]
    hr_s = ar * h0r - ai * h0i + x_s[:, :sw]
    hi_s = ar * h0i + ai * h0r + x_s[:, sw:]
    sr_ref[...] = hr_s
    si_ref[...] = hi_s
    y_ref[tp:tp + ns, :] = project_out(jnp.concatenate([hr_s, hi_s], axis=1), u_s)
    y_ref[tp + ns:, :] = jnp.zeros((y_ref.shape[0] - tp - ns, y_ref.shape[1]), F32)

    u_m = u_ref[tp + ns:tp + ns + nm, :]
    x_m = jnp.dot(u_m.astype(BF16), bcat, preferred_element_type=F32)
    mr = jnp.zeros((1, sw), F32)
    mi = jnp.zeros((1, sw), F32)
    for t in range(nm):
        mr, mi = (ar * mr - ai * mi + x_m[t:t + 1, :sw],
                  ar * mi + ai * mr + x_m[t:t + 1, sw:])

    ar8 = jnp.broadcast_to(ar, (nseg, sw))
    ai8 = jnp.broadcast_to(ai, (nseg, sw))
    unroll = 4 if ls % 4 == 0 else 1

    def scan_pass(store):
        def body(i, c):
            hr, hi = c
            r = pl.multiple_of(i * nseg, nseg)
            nhr = ar8 * hr - ai8 * hi + x_scr[pl.ds(r, nseg), 0:sw]
            nhi = ar8 * hi + ai8 * hr + x_scr[pl.ds(r, nseg), sw:2 * sw]
            if store:
                x_scr[pl.ds(r, nseg), 0:sw] = nhr
                x_scr[pl.ds(r, nseg), sw:2 * sw] = nhi
            return nhr, nhi
        return body

    for b in range(nb):
        def regroup_in(i, c, b=b):
            r = pl.multiple_of(i * nseg, nseg)
            up_scr[pl.ds(r, nseg), :] = u_ref[pl.ds(b * seq + i, nseg, stride=ls), :]
            return c
        lax.fori_loop(0, ls, regroup_in, 0, unroll=unroll)
        u_b = up_scr[...]
        x_scr[...] = jnp.dot(u_b.astype(BF16), bcat, preferred_element_type=F32)
        z = jnp.zeros((nseg, sw), F32)
        fr, fi = lax.fori_loop(0, ls, scan_pass(False), (z, z), unroll=unroll)
        cr, ci = mr, mi
        for k in range(nseg):
            init_scr[k:k + 1, 0:sw] = cr
            init_scr[k:k + 1, sw:2 * sw] = ci
            cr, ci = (asr * cr - asi * ci + fr[k:k + 1, :],
                      asr * ci + asi * cr + fi[k:k + 1, :])
        pr_ref[b:b + 1, :] = cr
        pi_ref[b:b + 1, :] = ci
        lax.fori_loop(0, ls, scan_pass(True), (init_scr[:, 0:sw], init_scr[:, sw:2 * sw]),
                      unroll=unroll)
        yp_scr[...] = project_out(x_scr[...], u_b)

        def regroup_out(i, c, b=b):
            r = pl.multiple_of(i * nseg, nseg)
            y_ref[pl.ds(b * seq + i, nseg, stride=ls), :] = yp_scr[pl.ds(r, nseg), :]
            return c
        lax.fori_loop(0, ls, regroup_out, 0, unroll=unroll)


def _s5(proj, h0r, h0i, lam_re, lam_im, logdt, bre_bd, bim_bd, cre_bd, cim_bd, dvec,
        nb, seq, ns, nm):
    t_all = proj.shape[0]
    nslab, cw, sw = bre_bd.shape
    d_ssm = nslab * cw
    kern = functools.partial(_s5_kernel, nb=nb, seq=seq, ns=ns, nm=nm)
    return pl.pallas_call(
        kern,
        grid=(nslab,),
        in_specs=[
            pl.BlockSpec((t_all, cw), lambda s: (0, s)),
            pl.BlockSpec((ns, sw), lambda s: (0, s)),
            pl.BlockSpec((ns, sw), lambda s: (0, s)),
            pl.BlockSpec((1, sw), lambda s: (0, s)),
            pl.BlockSpec((1, sw), lambda s: (0, s)),
            pl.BlockSpec((1, sw), lambda s: (0, s)),
            pl.BlockSpec((None, cw, sw), lambda s: (s, 0, 0)),
            pl.BlockSpec((None, cw, sw), lambda s: (s, 0, 0)),
            pl.BlockSpec((None, sw, cw), lambda s: (s, 0, 0)),
            pl.BlockSpec((None, sw, cw), lambda s: (s, 0, 0)),
            pl.BlockSpec((1, cw), lambda s: (0, s)),
        ],
        out_specs=[
            pl.BlockSpec((t_all, cw), lambda s: (0, s)),
            pl.BlockSpec((nb, sw), lambda s: (0, s)),
            pl.BlockSpec((nb, sw), lambda s: (0, s)),
            pl.BlockSpec((ns, sw), lambda s: (0, s)),
            pl.BlockSpec((ns, sw), lambda s: (0, s)),
        ],
        out_shape=[
            jax.ShapeDtypeStruct((t_all, d_ssm), F32),
            jax.ShapeDtypeStruct((nb, nslab * sw), F32),
            jax.ShapeDtypeStruct((nb, nslab * sw), F32),
            jax.ShapeDtypeStruct((ns, nslab * sw), F32),
            jax.ShapeDtypeStruct((ns, nslab * sw), F32),
        ],
        scratch_shapes=[pltpu.VMEM((seq, 2 * sw), F32), pltpu.VMEM((SUBLANES, 2 * sw), F32),
                        pltpu.VMEM((seq, cw), F32), pltpu.VMEM((seq, cw), F32)],
        compiler_params=_cparams("parallel"),
        name="s5_scan",
    )(proj, h0r, h0i, lam_re, lam_im, logdt, bre_bd, bim_bd, cre_bd, cim_bd, dvec)


def _glu_norm_kernel(y_ref, w_ref, b_ref, g_ref, o_ref):
    z = _gelu(y_ref[...])
    gate = jnp.dot(z.astype(BF16), w_ref[...], preferred_element_type=F32) + b_ref[...]
    zz = z * _sigmoid(gate)
    s = lax.rsqrt(jnp.mean(zz * zz, axis=-1, keepdims=True) + EPS)
    o_ref[...] = (zz * s * g_ref[...]).astype(o_ref.dtype)


def _glu_norm(y, w, b, g, tm):
    t, d = y.shape
    return pl.pallas_call(
        _glu_norm_kernel,
        grid=(t // tm,),
        in_specs=[
            pl.BlockSpec((tm, d), lambda i: (i, 0)),
            pl.BlockSpec((d, d), lambda i: (0, 0)),
            pl.BlockSpec((1, d), lambda i: (0, 0)),
            pl.BlockSpec((1, d), lambda i: (0, 0)),
        ],
        out_specs=pl.BlockSpec((tm, d), lambda i: (i, 0)),
        out_shape=jax.ShapeDtypeStruct((t, d), BF16),
        compiler_params=_cparams("parallel"),
        name="s5_glu_norm",
    )(y, w, b, g)


def _cumsum_rows(x):
    n = x.shape[0]
    row = lax.broadcasted_iota(jnp.int32, x.shape, 0)
    s = 1
    while s < n:
        x = x + jnp.where(row >= s, pltpu.roll(x, s, axis=0), 0.0)
        s *= 2
    return x


def _gate(xg, w2, b2):
    pre = jnp.dot(xg.astype(BF16), w2, preferred_element_type=F32) + b2
    return _log_sigmoid(pre) * (1.0 / GLA_TAU)


def _head_out(o, r, g):
    o = o * lax.rsqrt(jnp.mean(o * o, axis=-1, keepdims=True) + EPS)
    return o * g * (r * _sigmoid(r))


def _gla_prompt_kernel(q_ref, k_ref, v_ref, r_ref, xg_ref, kt_ref, vt_ref, xgt_ref,
                       w2_ref, b2_ref, g_ref, o_ref, s_ref, st_scr, *, seq, ns, nm):
    dk = q_ref.shape[1]
    scale = float(dk) ** -0.5
    w2 = w2_ref[...]
    b2 = b2_ref[...]
    g = g_ref[...]

    lgm = _gate(xgt_ref[ns:ns + nm, :], w2, b2)
    bm = _cumsum_rows(lgm)
    kdm = kt_ref[ns:ns + nm, :] * jnp.exp(bm[nm - 1:nm, :] - bm)
    st_scr[...] = lax.dot_general(vt_ref[ns:ns + nm, :].astype(BF16), kdm.astype(BF16),
                                  (((0,), (0,)), ((), ())), preferred_element_type=F32)

    c = GLA_CHUNK
    nsub = c // GLA_SUB

    def chunk(ci, carry):
        r0 = pl.multiple_of(ci * c, c)
        q = q_ref[pl.ds(r0, c), :] * scale
        k = k_ref[pl.ds(r0, c), :]
        v = v_ref[pl.ds(r0, c), :]
        vb = v.astype(BF16)
        lg = _gate(xg_ref[pl.ds(r0, c), :], w2, b2)
        b = _cumsum_rows(lg)
        bl = b[c - 1:c, :]
        st = st_scr[...]
        o_inter = lax.dot_general((q * jnp.exp(b)).astype(BF16), st.astype(BF16),
                                  (((1,), (1,)), ((), ())), preferred_element_type=F32)
        outs = []
        for sb in range(nsub):
            lo = sb * GLA_SUB
            hi = lo + GLA_SUB
            beta = b[lo - 1:lo, :] if sb > 0 else jnp.zeros((1, dk), F32)
            qs = q[lo:hi, :] * jnp.exp(b[lo:hi, :] - beta)
            ks = k[0:hi, :] * jnp.exp(beta - b[0:hi, :])
            sc = lax.dot_general(qs.astype(BF16), ks.astype(BF16),
                                 (((1,), (1,)), ((), ())), preferred_element_type=F32)
            rowi = lax.broadcasted_iota(jnp.int32, (GLA_SUB, hi), 0)
            coli = lax.broadcasted_iota(jnp.int32, (GLA_SUB, hi), 1)
            sc = jnp.where(coli <= rowi + lo, sc, 0.0)
            outs.append(jnp.dot(sc.astype(BF16), vb[0:hi, :], preferred_element_type=F32))
        o = o_inter + jnp.concatenate(outs, axis=0)
        o_ref[pl.ds(r0, c), :] = _head_out(o, r_ref[pl.ds(r0, c), :], g).astype(o_ref.dtype)
        kd = k * jnp.exp(bl - b)
        st_scr[...] = jnp.exp(bl) * st + lax.dot_general(
            vb, kd.astype(BF16), (((0,), (0,)), ((), ())), preferred_element_type=F32)
        return carry

    lax.fori_loop(0, seq // c, chunk, 0, unroll=min(8, seq // c))
    s_ref[...] = st_scr[...].T


def _gla_prompt(proj, xg, w2, b2, g, nb, seq, ns, nm, nh, dk, dv, q0, k0, v0, r0):
    tp = nb * seq
    tail_blk = tp // TAIL_ROWS
    rb = seq
    kern = functools.partial(_gla_prompt_kernel, seq=seq, ns=ns, nm=nm)
    return pl.pallas_call(
        kern,
        grid=(nb, nh),
        in_specs=[
            pl.BlockSpec((rb, dk), lambda b, h: (b, q0 // dk + h)),
            pl.BlockSpec((rb, dk), lambda b, h: (b, k0 // dk + h)),
            pl.BlockSpec((rb, dv), lambda b, h: (b, v0 // dv + h)),
            pl.BlockSpec((rb, dv), lambda b, h: (b, r0 // dv + h)),
            pl.BlockSpec((rb, LANES), lambda b, h: (b, 0)),
            pl.BlockSpec((TAIL_ROWS, dk), lambda b, h: (tail_blk, k0 // dk + h)),
            pl.BlockSpec((TAIL_ROWS, dv), lambda b, h: (tail_blk, v0 // dv + h)),
            pl.BlockSpec((TAIL_ROWS, LANES), lambda b, h: (tail_blk, 0)),
            pl.BlockSpec((LANES, dk), lambda b, h: (0, h)),
            pl.BlockSpec((1, dk), lambda b, h: (0, h)),
            pl.BlockSpec((1, dv), lambda b, h: (0, h)),
        ],
        out_specs=[
            pl.BlockSpec((rb, dv), lambda b, h: (b, h)),
            pl.BlockSpec((None, None, dk, dv), lambda b, h: (b, h, 0, 0)),
        ],
        out_shape=[
            jax.ShapeDtypeStruct((tp, nh * dv), BF16),
            jax.ShapeDtypeStruct((nb, nh, dk, dv), F32),
        ],
        scratch_shapes=[pltpu.VMEM((dv, dk), F32)],
        compiler_params=_cparams("parallel", "parallel"),
        name="gla_prompt",
    )(proj, proj, proj, proj, xg, proj, proj, xg, w2, b2, g)


def _gla_sample_kernel(q_ref, k_ref, v_ref, r_ref, xg_ref, w2_ref, b2_ref, g_ref, s0_ref,
                       o_ref, s_ref, *, nh, dk, dv):
    n = pl.program_id(0)
    j = n % SUBLANES
    scale = float(dk) ** -0.5
    lg_all = _gate(xg_ref[pl.ds(j, 1), :], w2_ref[...], b2_ref[...])
    q_all = q_ref[pl.ds(j, 1), :] * scale
    k_all = k_ref[pl.ds(j, 1), :]
    v_all = v_ref[pl.ds(j, 1), :]
    r_all = r_ref[pl.ds(j, 1), :]
    g_all = g_ref[...]
    rows = 2 * SUBLANES
    rk = lax.broadcasted_iota(jnp.int32, (rows, dk), 0)
    rv = lax.broadcasted_iota(jnp.int32, (rows, dv), 0)
    outs = []
    for h in range(nh):
        lg = lg_all[:, h * dk:(h + 1) * dk]
        q = q_all[:, h * dk:(h + 1) * dk]
        k = k_all[:, h * dk:(h + 1) * dk]
        v = v_all[:, h * dv:(h + 1) * dv]
        e = jnp.exp(lg)
        s0 = s0_ref[h]
        e_hi = e.astype(BF16)
        e_mid = (e - e_hi.astype(F32)).astype(BF16)
        e_lo = (e - e_hi.astype(F32) - e_mid.astype(F32)).astype(BF16)
        def rows_of(x, w):
            return jnp.broadcast_to(x.astype(F32), (rows, w))
        lhs = jnp.where(rk == 0, rows_of(e_hi, dk),
              jnp.where(rk == 1, rows_of(e_mid, dk),
              jnp.where(rk == 2, rows_of(e_lo, dk),
              jnp.where(rk == 3, rows_of(k, dk), 0.0)))).astype(BF16)
        ones_part = jnp.where(rv < 3, 1.0, 0.0)
        v_part = jnp.where(rv == 3, rows_of(v, dv), 0.0)
        rhs = jnp.concatenate([ones_part, v_part], axis=1).astype(BF16)
        both = lax.dot_general(lhs, rhs, (((0,), (0,)), ((), ())), preferred_element_type=F32)
        s_ref[h] = both[:, :dv] * s0 + both[:, dv:]
        qe = jnp.broadcast_to((q * e).astype(BF16), (rows, dk))
        o = jnp.dot(qe, s0.astype(BF16), preferred_element_type=F32)[0:1, :]
        o = o + jnp.sum(q * k, axis=-1, keepdims=True) * v
        outs.append(_head_out(o, r_all[:, h * dv:(h + 1) * dv], g_all[:, h * dv:(h + 1) * dv]))
    o_ref[pl.ds(j, 1), :] = jnp.concatenate(outs, axis=1)


def _gla_sample(proj, xg, w2, b2, g, s0, tp, ns, nh, dk, dv, q0, k0, v0, r0):
    rb = SUBLANES
    base = tp // rb
    qk_w = nh * dk
    v_w = nh * dv
    kern = functools.partial(_gla_sample_kernel, nh=nh, dk=dk, dv=dv)
    return pl.pallas_call(
        kern,
        grid=(ns,),
        in_specs=[
            pl.BlockSpec((rb, qk_w), lambda n: (base + n // rb, q0 // qk_w)),
            pl.BlockSpec((rb, qk_w), lambda n: (base + n // rb, k0 // qk_w)),
            pl.BlockSpec((rb, v_w), lambda n: (base + n // rb, v0 // v_w)),
            pl.BlockSpec((rb, v_w), lambda n: (base + n // rb, r0 // v_w)),
            pl.BlockSpec((rb, LANES), lambda n: (base + n // rb, 0)),
            pl.BlockSpec((LANES, qk_w), lambda n: (0, 0)),
            pl.BlockSpec((1, qk_w), lambda n: (0, 0)),
            pl.BlockSpec((1, v_w), lambda n: (0, 0)),
            pl.BlockSpec((None, nh, dk, dv), lambda n: (n, 0, 0, 0)),
        ],
        out_specs=[
            pl.BlockSpec((rb, v_w), lambda n: (n // rb, 0)),
            pl.BlockSpec((None, nh, dk, dv), lambda n: (n, 0, 0, 0)),
        ],
        out_shape=[
            jax.ShapeDtypeStruct((ns, v_w), F32),
            jax.ShapeDtypeStruct((ns, nh, dk, dv), F32),
        ],
        compiler_params=_cparams("arbitrary"),
        name="gla_sample",
    )(proj, proj, proj, proj, xg, w2, b2, g, s0)


def _out_proj_kernel(ys_ref, og_ref, w1_ref, w2_ref, m_ref):
    acc = jnp.dot(ys_ref[...], w1_ref[...], preferred_element_type=F32)
    m_ref[...] = acc + jnp.dot(og_ref[...], w2_ref[...], preferred_element_type=F32)


def _out_proj(ys, og, w, tm, tn):
    t, dh = ys.shape
    d = w.shape[1]
    return pl.pallas_call(
        _out_proj_kernel,
        grid=(t // tm, d // tn),
        in_specs=[
            pl.BlockSpec((tm, dh), lambda i, j: (i, 0)),
            pl.BlockSpec((tm, dh), lambda i, j: (i, 0)),
            pl.BlockSpec((dh, tn), lambda i, j: (0, j)),
            pl.BlockSpec((dh, tn), lambda i, j: (1, j)),
        ],
        out_specs=pl.BlockSpec((tm, tn), lambda i, j: (i, j)),
        out_shape=jax.ShapeDtypeStruct((t, d), F32),
        compiler_params=_cparams("parallel", "parallel"),
        name="out_proj",
    )(ys, og, w, w)


def _resid_norm_t_kernel(m_ref, xp_ref, xt_ref, g_ref, h_ref, o_ref, *, nprompt):
    i = pl.program_id(0)

    def emit(x):
        h = x + m_ref[...]
        h_ref[...] = h
        o_ref[...] = _rms(h, g_ref[...]).T.astype(BF16)

    @pl.when(i < nprompt)
    def _():
        emit(xp_ref[...])

    @pl.when(i >= nprompt)
    def _():
        emit(xt_ref[...])


def _resid_norm_t(mix, x_p, x_t, g):
    t, d = mix.shape
    tr = x_t.shape[0]
    nprompt = x_p.shape[0] // tr
    return pl.pallas_call(
        functools.partial(_resid_norm_t_kernel, nprompt=nprompt),
        grid=(nprompt + 1,),
        in_specs=[
            pl.BlockSpec((tr, d), lambda i: (i, 0)),
            pl.BlockSpec((tr, d), lambda i: (jnp.minimum(i, nprompt - 1), 0)),
            pl.BlockSpec((tr, d), lambda i: (0, 0)),
            pl.BlockSpec((1, d), lambda i: (0, 0)),
        ],
        out_specs=[pl.BlockSpec((tr, d), lambda i: (i, 0)), pl.BlockSpec((d, tr), lambda i: (0, i))],
        out_shape=[jax.ShapeDtypeStruct((t, d), F32), jax.ShapeDtypeStruct((d, t), BF16)],
        compiler_params=_cparams("arbitrary"),
        name="resid_ffn_norm_t",
    )(mix, x_p, x_t, g)


def _topk_rows(x, k):
    outs = []
    for _ in range(k):
        m = jnp.max(x, axis=0, keepdims=True)
        outs.append(m)
        x = jnp.where(x == m, -jnp.inf, x)
    return outs


def _route_kernel(wq_ref, hn_ref, k1_ref, k2_ref, s1_ref, c1_ref, s2_ref, e2_ref, tau_ref):
    half = k1_ref.shape[1]
    tb = hn_ref.shape[1]
    qt = jnp.dot(wq_ref[...], hn_ref[...], preferred_element_type=F32)
    s1 = jnp.dot(k1_ref[...].astype(BF16), qt[:half, :].astype(BF16), preferred_element_type=F32)
    s2 = jnp.dot(k2_ref[...].astype(BF16), qt[half:, :].astype(BF16), preferred_element_type=F32)
    s1_ref[...] = s1
    s2_ref[...] = s2
    for t in range(tb // LANES):
        sl = slice(t * LANES, (t + 1) * LANES)
        a = s1[:, sl]
        b = s2[:, sl]
        kk = PEER_TOPK
        v1 = _topk_rows(a, kk)
        v2l = _topk_rows(b, kk)
        v2 = jnp.concatenate(v2l, axis=0)
        cand = jnp.concatenate(
            [v1[0] + v2]
            + [v1[i] + v2[:kk // 2] for i in range(1, kk // 2)]
            + [jnp.concatenate(v1[kk // 2:], axis=0) + v2l[0]], axis=0)
        top = _topk_rows(cand, kk)
        z = jnp.zeros_like(top[0])
        for c in top:
            z = z + jnp.exp(c - top[0])
        tau_ref[:, sl] = top[PEER_TOPK - 1]
        c1_ref[:, sl] = jnp.exp(a - v1[0]) / z
        e2_ref[:, sl] = jnp.exp(b - v2[0:1, :])


def _route(wq_t, hn_t, k1, k2, tb):
    d, t = hn_t.shape
    nh, nk, half = k1.shape
    qd = 2 * half
    tab = jax.ShapeDtypeStruct((nh, nk, t), F32)
    tab_spec = pl.BlockSpec((None, nk, tb), lambda i, h: (h, 0, i))
    return pl.pallas_call(
        _route_kernel,
        grid=(t // tb, nh),
        in_specs=[
            pl.BlockSpec((qd, d), lambda i, h: (h, 0)),
            pl.BlockSpec((d, tb), lambda i, h: (0, i)),
            pl.BlockSpec((None, nk, half), lambda i, h: (h, 0, 0)),
            pl.BlockSpec((None, nk, half), lambda i, h: (h, 0, 0)),
        ],
        out_specs=[tab_spec, tab_spec, tab_spec, tab_spec,
                   pl.BlockSpec((None, 1, tb), lambda i, h: (h, 0, i))],
        out_shape=[tab, tab, tab, tab, jax.ShapeDtypeStruct((nh, 1, t), F32)],
        compiler_params=_cparams("parallel", "parallel"),
        name="peer_route",
    )(wq_t, hn_t, k1, k2)


GATE_ROWS = 32
OUT_ROWS = 512


K_CHUNK = 512


def _peer_kernel(u_ref, vt_ref, hn_ref, s1_ref, c1_ref, s2_ref, e2_ref, tau_ref, o_ref,
                 w_scr, act_scr, a_scr):
    eb, d = u_ref.shape
    tb = hn_ref.shape[1]
    nh, nk, _ = s2_ref.shape
    nr = eb // nk

    @pl.when(pl.program_id(1) == 0)
    def _():
        o_ref[...] = jnp.zeros_like(o_ref)

    def gate_group(t, q):
        sl = slice(t * LANES, (t + 1) * LANES)
        rows = slice(q * GATE_ROWS, (q + 1) * GATE_ROWS)
        accs = [None] * nr
        for h in range(nh):
            s2t = s2_ref[h, rows, sl]
            e2t = e2_ref[h, rows, sl]
            tau = tau_ref[h, :, sl]
            for r in range(nr):
                ssum = s1_ref[r, h:h + 1, sl] + s2t
                term = jnp.where(ssum >= tau, c1_ref[r, h:h + 1, sl] * e2t, 0.0)
                accs[r] = term if accs[r] is None else accs[r] + term
        for r in range(nr):
            lo = r * nk + q * GATE_ROWS
            w_scr[lo:lo + GATE_ROWS, sl] = accs[r]

    groups = [(t, q) for t in range(tb // LANES) for q in range(nk // GATE_ROWS)]
    nkc = d // K_CHUNK
    per = -(-len(groups) // nkc)
    for kc in range(nkc):
        @pl.when(pl.program_id(1) >= 0)
        def _(kc=kc):
            ks = slice(kc * K_CHUNK, (kc + 1) * K_CHUNK)
            part = jnp.dot(u_ref[:, ks], hn_ref[ks, :], preferred_element_type=F32)
            if kc == 0:
                act_scr[...] = part
            else:
                act_scr[...] += part
            for t, q in groups[kc * per:(kc + 1) * per]:
                gate_group(t, q)
    a_scr[...] = (w_scr[...] * _gelu(act_scr[...])).astype(BF16)
    for dc in range(d // OUT_ROWS):
        dr = slice(dc * OUT_ROWS, (dc + 1) * OUT_ROWS)
        o_ref[dr, :] += jnp.dot(vt_ref[dr, :], a_scr[...], preferred_element_type=F32)


def _peer(u_bf, v_t, hn_t, s1, c1, s2, e2, tau, tb, eb):
    ne, d = u_bf.shape
    t = hn_t.shape[1]
    nh, nk, _ = s2.shape
    once = pl.Buffered(1)
    tab_spec = pl.BlockSpec((nh, nk, tb), lambda i, e: (0, 0, i), pipeline_mode=once)
    row_spec = pl.BlockSpec((eb // nk, nh, tb), lambda i, e: (e, 0, i))
    return pl.pallas_call(
        _peer_kernel,
        grid=(t // tb, ne // eb),
        in_specs=[
            pl.BlockSpec((eb, d), lambda i, e: (e, 0)),
            pl.BlockSpec((d, eb), lambda i, e: (0, e)),
            pl.BlockSpec((d, tb), lambda i, e: (0, i), pipeline_mode=once),
            row_spec, row_spec, tab_spec, tab_spec,
            pl.BlockSpec((nh, 1, tb), lambda i, e: (0, 0, i)),
        ],
        out_specs=pl.BlockSpec((d, tb), lambda i, e: (0, i)),
        out_shape=jax.ShapeDtypeStruct((d, t), F32),
        scratch_shapes=[pltpu.VMEM((eb, tb), F32), pltpu.VMEM((eb, tb), F32),
                        pltpu.VMEM((eb, tb), BF16)],
        compiler_params=pltpu.CompilerParams(
            dimension_semantics=("parallel", "arbitrary"), vmem_limit_bytes=PEER_VMEM_LIMIT),
        name="peer_experts",
    )(u_bf, v_t, hn_t, s1, c1, s2, e2, tau)


def _final_kernel(pt_ref, h_ref, g_ref, yp_ref, yt_ref, *, nprompt):
    i = pl.program_id(0)
    y = _rms(h_ref[...] + pt_ref[...].T, g_ref[...])

    @pl.when(i < nprompt)
    def _():
        yp_ref[...] = y

    @pl.when(i >= nprompt)
    def _():
        yt_ref[...] = y


def _final(peer_t, h, g, tp, tr):
    d, t = peer_t.shape
    nprompt = tp // tr
    return pl.pallas_call(
        functools.partial(_final_kernel, nprompt=nprompt),
        grid=(nprompt + 1,),
        in_specs=[
            pl.BlockSpec((d, tr), lambda i: (0, i)),
            pl.BlockSpec((tr, d), lambda i: (i, 0)),
            pl.BlockSpec((1, d), lambda i: (0, 0)),
        ],
        out_specs=[
            pl.BlockSpec((tr, d), lambda i: (jnp.minimum(i, nprompt - 1), 0)),
            pl.BlockSpec((tr, d), lambda i: (0, 0)),
        ],
        out_shape=[jax.ShapeDtypeStruct((tp, d), F32), jax.ShapeDtypeStruct((tr, d), F32)],
        compiler_params=_cparams("arbitrary"),
        name="final_norm",
    )(peer_t, h, g)


def _block_diag(w, ngrp):
    nslab, _, a, b = w.shape
    eye = jnp.eye(ngrp, dtype=w.dtype)
    full = w[:, :, :, None, :] * eye[None, :, None, :, None]
    return full.reshape(nslab, ngrp * a, ngrp * b)


def kernel(x_prompt, x_sample, state_s5_re, state_s5_im, state_gla, meta_tokens, norm_mix_g, w_in, s5_lam_re, s5_lam_im, s5_log_dt, s5_b_re, s5_b_im, s5_c_re, s5_c_im, s5_d, s5_w_glu, s5_b_glu, s5_norm_g, gla_w_gate2, gla_b_gate2, gla_norm_g, w_out, norm_ffn_g, peer_w_q, peer_keys, peer_u, peer_v, norm_final_g):
    nb, seq, d = x_prompt.shape
    ns = x_sample.shape[0]
    nm = meta_tokens.shape[0]
    depth = w_in.shape[0]
    assert depth == 1 and x_sample.shape[1] == 1
    tp = nb * seq
    assert tp % TAIL_ROWS == 0 and ns + nm <= TAIL_ROWS and ns % SUBLANES == 0
    assert seq % GLA_CHUNK == 0 and nm % SUBLANES == 0
    t_all = tp + TAIL_ROWS

    ngrp, nstate, gch = s5_b_re.shape[1:]
    d_ssm = ngrp * gch
    gps = LANES // gch
    nslab = ngrp // gps
    nh, dk, dv = state_gla.shape[2:]
    d_gla = nh * dv
    rank = gla_w_gate2.shape[1]
    q0 = d_ssm
    k0 = q0 + nh * dk
    v0 = k0 + nh * dk
    r0 = v0 + d_gla
    g0 = r0 + d_gla
    assert g0 + rank == w_in.shape[2] and d_ssm + d_gla == w_out.shape[1]

    x_p = x_prompt.reshape(tp, d)
    x_t = jnp.concatenate([x_sample.reshape(ns, d), meta_tokens,
                           jnp.zeros((TAIL_ROWS - ns - nm, d), F32)], axis=0)
    w_in_bf = w_in[0].astype(BF16)
    w_g1 = jnp.pad(w_in[0, :, g0:], ((0, 0), (0, LANES - rank))).astype(BF16)
    w_g2 = jnp.pad(gla_w_gate2[0], ((0, LANES - rank), (0, 0))).astype(BF16)
    lam_re = s5_lam_re[0].reshape(1, ngrp * nstate)
    lam_im = s5_lam_im[0].reshape(1, ngrp * nstate)
    logdt = jnp.repeat(s5_log_dt[0], nstate).reshape(1, ngrp * nstate)
    b_re4 = jnp.transpose(s5_b_re[0].reshape(nslab, gps, nstate, gch), (0, 1, 3, 2))
    b_im4 = jnp.transpose(s5_b_im[0].reshape(nslab, gps, nstate, gch), (0, 1, 3, 2))
    c_re4 = jnp.transpose(s5_c_re[0].reshape(nslab, gps, gch, nstate), (0, 1, 3, 2))
    c_im4 = jnp.transpose(s5_c_im[0].reshape(nslab, gps, gch, nstate), (0, 1, 3, 2))
    bre_bd = _block_diag(b_re4, gps)
    bim_bd = _block_diag(b_im4, gps)
    cre_bd = _block_diag(c_re4, gps)
    cim_bd = _block_diag(c_im4, gps)
    h0r = state_s5_re[0].reshape(ns, ngrp * nstate)
    h0i = state_s5_im[0].reshape(ns, ngrp * nstate)

    tm = _row_tile(t_all, (768, 512, 256))

    xn = _prenorm(x_p, x_t, norm_mix_g)
    proj, xg = _in_proj(xn, w_in_bf, w_g1, tm, PROJ_COLS, g0)

    y_raw, pr, pi_, sr, si = _s5(proj, h0r, h0i, lam_re, lam_im, logdt, bre_bd, bim_bd,
                                 cre_bd, cim_bd, s5_d, nb, seq, ns, nm)
    y_ssm = _glu_norm(y_raw, s5_w_glu[0].astype(BF16), s5_b_glu, s5_norm_g, TAIL_ROWS)

    og_p, gla_p = _gla_prompt(proj, xg, w_g2, gla_b_gate2, gla_norm_g, nb, seq, ns, nm,
                              nh, dk, dv, q0, k0, v0, r0)
    og_s, gla_s = _gla_sample(proj, xg, w_g2, gla_b_gate2, gla_norm_g, state_gla[0],
                              tp, ns, nh, dk, dv, q0, k0, v0, r0)
    og = jnp.concatenate([og_p, og_s.astype(BF16),
                          jnp.zeros((TAIL_ROWS - ns, d_gla), BF16)], axis=0)

    w_o = w_out[0].astype(BF16)
    assert d_ssm == d_gla
    mix = _out_proj(y_ssm, og, w_o, tm, PROJ_COLS)

    h, hn_t = _resid_norm_t(mix, x_p, x_t, norm_ffn_g)
    wq_t = peer_w_q[0].T.astype(BF16)
    s1, c1, s2, e2, tau = _route(wq_t, hn_t, peer_keys[0, :, 0], peer_keys[0, :, 1], tm)
    u_bf = peer_u[0].astype(BF16)
    v_t = peer_v[0].astype(BF16).T
    peer_t = _peer(u_bf, v_t, hn_t, jnp.transpose(s1, (1, 0, 2)), jnp.transpose(c1, (1, 0, 2)),
                   s2, e2, tau, tm, 512)
    y_p, y_t = _final(peer_t, h, norm_final_g.reshape(1, d), tp, TAIL_ROWS)

    y_prompt = y_p.reshape(nb, seq, d)
    y_sample = y_t[:ns].reshape(ns, 1, d)
    return (y_prompt, y_sample,
            pr.reshape(1, nb, ngrp, nstate), pi_.reshape(1, nb, ngrp, nstate), gla_p[None],
            sr.reshape(1, ns, ngrp, nstate), si.reshape(1, ns, ngrp, nstate), gla_s[None])
```

```python
import functools

import jax
import jax.numpy as jnp
from jax import lax
from jax.experimental import pallas as pl
from jax.experimental.pallas import tpu as pltpu

F32 = jnp.float32
BF16 = jnp.bfloat16

EPS = 1e-6
GLA_TAU = 16.0
GLA_CHUNK = 64
GLA_SUB = 16
PEER_TOPK = 16
LANES = 128
SUBLANES = 8
TAIL_ROWS = 256
PROJ_COLS = 1024
VMEM_LIMIT = 56 * 1024 * 1024
PEER_VMEM_LIMIT = 60 * 1024 * 1024


def _cparams(*sem):
    return pltpu.CompilerParams(dimension_semantics=sem, vmem_limit_bytes=VMEM_LIMIT)


def _gelu(x):
    return 0.5 * x * (1.0 + jnp.tanh(0.7978845608028654 * (x + 0.044715 * (x * x * x))))


def _sigmoid(x):
    return 1.0 / (1.0 + jnp.exp(-x))


def _log_sigmoid(x):
    return jnp.minimum(x, 0.0) - jnp.log(1.0 + jnp.exp(-jnp.abs(x)))


def _row_tile(n, cands):
    for c in cands:
        if n % c == 0:
            return c
    raise ValueError(f"no row tile for {n}")


def _rms(x, g):
    return x * lax.rsqrt(jnp.mean(x * x, axis=-1, keepdims=True) + EPS) * g


def _prenorm_kernel(xp_ref, xt_ref, g_ref, o_ref, *, nprompt):
    i = pl.program_id(0)

    @pl.when(i < nprompt)
    def _():
        o_ref[...] = _rms(xp_ref[...], g_ref[...]).astype(BF16)

    @pl.when(i >= nprompt)
    def _():
        o_ref[...] = _rms(xt_ref[...], g_ref[...]).astype(BF16)


def _prenorm(x_p, x_t, g):
    tp, d = x_p.shape
    tr = x_t.shape[0]
    nprompt = tp // tr
    return pl.pallas_call(
        functools.partial(_prenorm_kernel, nprompt=nprompt),
        grid=(nprompt + 1,),
        in_specs=[
            pl.BlockSpec((tr, d), lambda i: (jnp.minimum(i, nprompt - 1), 0)),
            pl.BlockSpec((tr, d), lambda i: (0, 0)),
            pl.BlockSpec((1, d), lambda i: (0, 0)),
        ],
        out_specs=pl.BlockSpec((tr, d), lambda i: (i, 0)),
        out_shape=jax.ShapeDtypeStruct((tp + tr, d), BF16),
        compiler_params=_cparams("arbitrary"),
        name="mix_norm",
    )(x_p, x_t, g)


def _in_proj_kernel(xn_ref, w_ref, wg_ref, o_ref, og_ref):
    @pl.when(pl.program_id(1) == 0)
    def _():
        og_ref[...] = jnp.dot(xn_ref[...], wg_ref[...], preferred_element_type=F32)

    o_ref[...] = jnp.dot(xn_ref[...], w_ref[...], preferred_element_type=F32)


def _in_proj(xn, w, wg, tm, tn, n):
    t, d = xn.shape
    return pl.pallas_call(
        _in_proj_kernel,
        grid=(t // tm, n // tn),
        in_specs=[
            pl.BlockSpec((tm, d), lambda i, j: (i, 0)),
            pl.BlockSpec((d, tn), lambda i, j: (0, j)),
            pl.BlockSpec((d, LANES), lambda i, j: (0, 0)),
        ],
        out_specs=[
            pl.BlockSpec((tm, tn), lambda i, j: (i, j)),
            pl.BlockSpec((tm, LANES), lambda i, j: (i, 0)),
        ],
        out_shape=[jax.ShapeDtypeStruct((t, n), F32), jax.ShapeDtypeStruct((t, LANES), F32)],
        compiler_params=_cparams("parallel", "arbitrary"),
        name="in_proj",
    )(xn, w, wg)


def _s5_kernel(u_ref, h0r_ref, h0i_ref, lr_ref, li_ref, ldt_ref, bre_ref, bim_ref,
               cre_ref, cim_ref, d_ref,
               y_ref, pr_ref, pi_ref, sr_ref, si_ref,
               x_scr, init_scr, up_scr, yp_scr, *, nb, seq, ns, nm):
    sw = lr_ref.shape[1]
    tp = nb * seq
    nseg = SUBLANES
    ls = seq // nseg

    lr = lr_ref[...]
    li = li_ref[...]
    dt = jnp.exp(ldt_ref[...])
    mag = jnp.exp(lr * dt)
    ang = li * dt
    ar = mag * jnp.cos(ang)
    ai = mag * jnp.sin(ang)
    den = lr * lr + li * li
    nr = ar - 1.0
    qr = (nr * lr + ai * li) / den
    qi = (ai * lr - nr * li) / den
    bre = bre_ref[...]
    bim = bim_ref[...]
    bcat = jnp.concatenate([qr * bre - qi * bim, qr * bim + qi * bre], axis=1).astype(BF16)
    ccat = jnp.concatenate([cre_ref[...], -cim_ref[...]], axis=0).astype(BF16)
    dvec = d_ref[...]
    mag_s = jnp.exp(lr * dt * float(ls))
    asr = mag_s * jnp.cos(ang * float(ls))
    asi = mag_s * jnp.sin(ang * float(ls))

    def project_out(h, u):
        return jnp.dot(h.astype(BF16), ccat, preferred_element_type=F32) + dvec * u

    u_s = u_ref[tp:tp + ns, :]
    x_s = jnp.dot(u_s.astype(BF16), bcat, preferred_element_type=F32)
    h0r = h0r_ref[...]
    h0i = h0i_ref[...]
    hr_s = ar * h0r - ai * h0i + x_s[:, :sw]
    hi_s = ar * h0i + ai * h0r + x_s[:, sw:]
    sr_ref[...] = hr_s
    si_ref[...] = hi_s
    y_ref[tp:tp + ns, :] = project_out(jnp.concatenate([hr_s, hi_s], axis=1), u_s)
    y_ref[tp + ns:, :] = jnp.zeros((y_ref.shape[0] - tp - ns, y_ref.shape[1]), F32)

    u_m = u_ref[tp + ns:tp + ns + nm, :]
    x_m = jnp.dot(u_m.astype(BF16), bcat, preferred_element_type=F32)
    mr = jnp.zeros((1, sw), F32)
    mi = jnp.zeros((1, sw), F32)
    for t in range(nm):
        mr, mi = (ar * mr - ai * mi + x_m[t:t + 1, :sw],
                  ar * mi + ai * mr + x_m[t:t + 1, sw:])

    ar8 = jnp.broadcast_to(ar, (nseg, sw))
    ai8 = jnp.broadcast_to(ai, (nseg, sw))
    unroll = 4 if ls % 4 == 0 else 1

    def scan_pass(store):
        def body(i, c):
            hr, hi = c
            r = pl.multiple_of(i * nseg, nseg)
            nhr = ar8 * hr - ai8 * hi + x_scr[pl.ds(r, nseg), 0:sw]
            nhi = ar8 * hi + ai8 * hr + x_scr[pl.ds(r, nseg), sw:2 * sw]
            if store:
                x_scr[pl.ds(r, nseg), 0:sw] = nhr
                x_scr[pl.ds(r, nseg), sw:2 * sw] = nhi
            return nhr, nhi
        return body

    for b in range(nb):
        def regroup_in(i, c, b=b):
            r = pl.multiple_of(i * nseg, nseg)
            up_scr[pl.ds(r, nseg), :] = u_ref[pl.ds(b * seq + i, nseg, stride=ls), :]
            return c
        lax.fori_loop(0, ls, regroup_in, 0, unroll=unroll)
        u_b = up_scr[...]
        x_scr[...] = jnp.dot(u_b.astype(BF16), bcat, preferred_element_type=F32)
        z = jnp.zeros((nseg, sw), F32)
        fr, fi = lax.fori_loop(0, ls, scan_pass(False), (z, z), unroll=unroll)
        cr, ci = mr, mi
        for k in range(nseg):
            init_scr[k:k + 1, 0:sw] = cr
            init_scr[k:k + 1, sw:2 * sw] = ci
            cr, ci = (asr * cr - asi * ci + fr[k:k + 1, :],
                      asr * ci + asi * cr + fi[k:k + 1, :])
        pr_ref[b:b + 1, :] = cr
        pi_ref[b:b + 1, :] = ci
        lax.fori_loop(0, ls, scan_pass(True), (init_scr[:, 0:sw], init_scr[:, sw:2 * sw]),
                      unroll=unroll)
        yp_scr[...] = project_out(x_scr[...], u_b)

        def regroup_out(i, c, b=b):
            r = pl.multiple_of(i * nseg, nseg)
            y_ref[pl.ds(b * seq + i, nseg, stride=ls), :] = yp_scr[pl.ds(r, nseg), :]
            return c
        lax.fori_loop(0, ls, regroup_out, 0, unroll=unroll)


def _s5(proj, h0r, h0i, lam_re, lam_im, logdt, bre_bd, bim_bd, cre_bd, cim_bd, dvec,
        nb, seq, ns, nm):
    t_all = proj.shape[0]
    nslab, cw, sw = bre_bd.shape
    d_ssm = nslab * cw
    kern = functools.partial(_s5_kernel, nb=nb, seq=seq, ns=ns, nm=nm)
    return pl.pallas_call(
        kern,
        grid=(nslab,),
        in_specs=[
            pl.BlockSpec((t_all, cw), lambda s: (0, s)),
            pl.BlockSpec((ns, sw), lambda s: (0, s)),
            pl.BlockSpec((ns, sw), lambda s: (0, s)),
            pl.BlockSpec((1, sw), lambda s: (0, s)),
            pl.BlockSpec((1, sw), lambda s: (0, s)),
            pl.BlockSpec((1, sw), lambda s: (0, s)),
            pl.BlockSpec((None, cw, sw), lambda s: (s, 0, 0)),
            pl.BlockSpec((None, cw, sw), lambda s: (s, 0, 0)),
            pl.BlockSpec((None, sw, cw), lambda s: (s, 0, 0)),
            pl.BlockSpec((None, sw, cw), lambda s: (s, 0, 0)),
            pl.BlockSpec((1, cw), lambda s: (0, s)),
        ],
        out_specs=[
            pl.BlockSpec((t_all, cw), lambda s: (0, s)),
            pl.BlockSpec((nb, sw), lambda s: (0, s)),
            pl.BlockSpec((nb, sw), lambda s: (0, s)),
            pl.BlockSpec((ns, sw), lambda s: (0, s)),
            pl.BlockSpec((ns, sw), lambda s: (0, s)),
        ],
        out_shape=[
            jax.ShapeDtypeStruct((t_all, d_ssm), F32),
            jax.ShapeDtypeStruct((nb, nslab * sw), F32),
            jax.ShapeDtypeStruct((nb, nslab * sw), F32),
            jax.ShapeDtypeStruct((ns, nslab * sw), F32),
            jax.ShapeDtypeStruct((ns, nslab * sw), F32),
        ],
        scratch_shapes=[pltpu.VMEM((seq, 2 * sw), F32), pltpu.VMEM((SUBLANES, 2 * sw), F32),
                        pltpu.VMEM((seq, cw), F32), pltpu.VMEM((seq, cw), F32)],
        compiler_params=_cparams("parallel"),
        name="s5_scan",
    )(proj, h0r, h0i, lam_re, lam_im, logdt, bre_bd, bim_bd, cre_bd, cim_bd, dvec)


def _glu_norm_kernel(y_ref, w_ref, b_ref, g_ref, o_ref):
    z = _gelu(y_ref[...])
    gate = jnp.dot(z.astype(BF16), w_ref[...], preferred_element_type=F32) + b_ref[...]
    zz = z * _sigmoid(gate)
    s = lax.rsqrt(jnp.mean(zz * zz, axis=-1, keepdims=True) + EPS)
    o_ref[...] = (zz * s * g_ref[...]).astype(o_ref.dtype)


def _glu_norm(y, w, b, g, tm):
    t, d = y.shape
    return pl.pallas_call(
        _glu_norm_kernel,
        grid=(t // tm,),
        in_specs=[
            pl.BlockSpec((tm, d), lambda i: (i, 0)),
            pl.BlockSpec((d, d), lambda i: (0, 0)),
            pl.BlockSpec((1, d), lambda i: (0, 0)),
            pl.BlockSpec((1, d), lambda i: (0, 0)),
        ],
        out_specs=pl.BlockSpec((tm, d), lambda i: (i, 0)),
        out_shape=jax.ShapeDtypeStruct((t, d), BF16),
        compiler_params=_cparams("parallel"),
        name="s5_glu_norm",
    )(y, w, b, g)


def _cumsum_rows(x):
    n = x.shape[0]
    row = lax.broadcasted_iota(jnp.int32, x.shape, 0)
    s = 1
    while s < n:
        x = x + jnp.where(row >= s, pltpu.roll(x, s, axis=0), 0.0)
        s *= 2
    return x


def _gate(xg, w2, b2):
    pre = jnp.dot(xg.astype(BF16), w2, preferred_element_type=F32) + b2
    return _log_sigmoid(pre) * (1.0 / GLA_TAU)


def _head_out(o, r, g):
    o = o * lax.rsqrt(jnp.mean(o * o, axis=-1, keepdims=True) + EPS)
    return o * g * (r * _sigmoid(r))


def _gla_prompt_kernel(q_ref, k_ref, v_ref, r_ref, xg_ref, kt_ref, vt_ref, xgt_ref,
                       w2_ref, b2_ref, g_ref, o_ref, s_ref, st_scr, *, seq, ns, nm):
    dk = q_ref.shape[1]
    scale = float(dk) ** -0.5
    w2 = w2_ref[...]
    b2 = b2_ref[...]
    g = g_ref[...]

    lgm = _gate(xgt_ref[ns:ns + nm, :], w2, b2)
    bm = _cumsum_rows(lgm)
    kdm = kt_ref[ns:ns + nm, :] * jnp.exp(bm[nm - 1:nm, :] - bm)
    st_scr[...] = lax.dot_general(vt_ref[ns:ns + nm, :].astype(BF16), kdm.astype(BF16),
                                  (((0,), (0,)), ((), ())), preferred_element_type=F32)

    c = GLA_CHUNK
    nsub = c // GLA_SUB

    def chunk(ci, carry):
        r0 = pl.multiple_of(ci * c, c)
        q = q_ref[pl.ds(r0, c), :] * scale
        k = k_ref[pl.ds(r0, c), :]
        v = v_ref[pl.ds(r0, c), :]
        vb = v.astype(BF16)
        lg = _gate(xg_ref[pl.ds(r0, c), :], w2, b2)
        b = _cumsum_rows(lg)
        bl = b[c - 1:c, :]
        st = st_scr[...]
        o_inter = lax.dot_general((q * jnp.exp(b)).astype(BF16), st.astype(BF16),
                                  (((1,), (1,)), ((), ())), preferred_element_type=F32)
        outs = []
        for sb in range(nsub):
            lo = sb * GLA_SUB
            hi = lo + GLA_SUB
            beta = b[lo - 1:lo, :] if sb > 0 else jnp.zeros((1, dk), F32)
            qs = q[lo:hi, :] * jnp.exp(b[lo:hi, :] - beta)
            ks = k[0:hi, :] * jnp.exp(beta - b[0:hi, :])
            sc = lax.dot_general(qs.astype(BF16), ks.astype(BF16),
                                 (((1,), (1,)), ((), ())), preferred_element_type=F32)
            rowi = lax.broadcasted_iota(jnp.int32, (GLA_SUB, hi), 0)
            coli = lax.broadcasted_iota(jnp.int32, (GLA_SUB, hi), 1)
            sc = jnp.where(coli <= rowi + lo, sc, 0.0)
            outs.append(jnp.dot(sc.astype(BF16), vb[0:hi, :], preferred_element_type=F32))
        o = o_inter + jnp.concatenate(outs, axis=0)
        o_ref[pl.ds(r0, c), :] = _head_out(o, r_ref[pl.ds(r0, c), :], g).astype(o_ref.dtype)
        kd = k * jnp.exp(bl - b)
        st_scr[...] = jnp.exp(bl) * st + lax.dot_general(
            vb, kd.astype(BF16), (((0,), (0,)), ((), ())), preferred_element_type=F32)
        return carry

    lax.fori_loop(0, seq // c, chunk, 0, unroll=min(8, seq // c))
    s_ref[...] = st_scr[...].T


def _gla_prompt(proj, xg, w2, b2, g, nb, seq, ns, nm, nh, dk, dv, q0, k0, v0, r0):
    tp = nb * seq
    tail_blk = tp // TAIL_ROWS
    rb = seq
    kern = functools.partial(_gla_prompt_kernel, seq=seq, ns=ns, nm=nm)
    return pl.pallas_call(
        kern,
        grid=(nb, nh),
        in_specs=[
            pl.BlockSpec((rb, dk), lambda b, h: (b, q0 // dk + h)),
            pl.BlockSpec((rb, dk), lambda b, h: (b, k0 // dk + h)),
            pl.BlockSpec((rb, dv), lambda b, h: (b, v0 // dv + h)),
            pl.BlockSpec((rb, dv), lambda b, h: (b, r0 // dv + h)),
            pl.BlockSpec((rb, LANES), lambda b, h: (b, 0)),
            pl.BlockSpec((TAIL_ROWS, dk), lambda b, h: (tail_blk, k0 // dk + h)),
            pl.BlockSpec((TAIL_ROWS, dv), lambda b, h: (tail_blk, v0 // dv + h)),
            pl.BlockSpec((TAIL_ROWS, LANES), lambda b, h: (tail_blk, 0)),
            pl.BlockSpec((LANES, dk), lambda b, h: (0, h)),
            pl.BlockSpec((1, dk), lambda b, h: (0, h)),
            pl.BlockSpec((1, dv), lambda b, h: (0, h)),
        ],
        out_specs=[
            pl.BlockSpec((rb, dv), lambda b, h: (b, h)),
            pl.BlockSpec((None, None, dk, dv), lambda b, h: (b, h, 0, 0)),
        ],
        out_shape=[
            jax.ShapeDtypeStruct((tp, nh * dv), BF16),
            jax.ShapeDtypeStruct((nb, nh, dk, dv), F32),
        ],
        scratch_shapes=[pltpu.VMEM((dv, dk), F32)],
        compiler_params=_cparams("parallel", "parallel"),
        name="gla_prompt",
    )(proj, proj, proj, proj, xg, proj, proj, xg, w2, b2, g)


def _gla_sample_kernel(q_ref, k_ref, v_ref, r_ref, xg_ref, w2_ref, b2_ref, g_ref, s0_ref,
                       o_ref, s_ref, *, nh, dk, dv):
    n = pl.program_id(0)
    j = n % SUBLANES
    scale = float(dk) ** -0.5
    lg_all = _gate(xg_ref[pl.ds(j, 1), :], w2_ref[...], b2_ref[...])
    q_all = q_ref[pl.ds(j, 1), :] * scale
    k_all = k_ref[pl.ds(j, 1), :]
    v_all = v_ref[pl.ds(j, 1), :]
    r_all = r_ref[pl.ds(j, 1), :]
    g_all = g_ref[...]
    rows = 2 * SUBLANES
    rk = lax.broadcasted_iota(jnp.int32, (rows, dk), 0)
    rv = lax.broadcasted_iota(jnp.int32, (rows, dv), 0)
    outs = []
    for h in range(nh):
        lg = lg_all[:, h * dk:(h + 1) * dk]
        q = q_all[:, h * dk:(h + 1) * dk]
        k = k_all[:, h * dk:(h + 1) * dk]
        v = v_all[:, h * dv:(h + 1) * dv]
        e = jnp.exp(lg)
        s0 = s0_ref[h]
        e_hi = e.astype(BF16)
        e_mid = (e - e_hi.astype(F32)).astype(BF16)
        e_lo = (e - e_hi.astype(F32) - e_mid.astype(F32)).astype(BF16)
        def rows_of(x, w):
            return jnp.broadcast_to(x.astype(F32), (rows, w))
        lhs = jnp.where(rk == 0, rows_of(e_hi, dk),
              jnp.where(rk == 1, rows_of(e_mid, dk),
              jnp.where(rk == 2, rows_of(e_lo, dk),
              jnp.where(rk == 3, rows_of(k, dk), 0.0)))).astype(BF16)
        ones_part = jnp.where(rv < 3, 1.0, 0.0)
        v_part = jnp.where(rv == 3, rows_of(v, dv), 0.0)
        rhs = jnp.concatenate([ones_part, v_part], axis=1).astype(BF16)
        both = lax.dot_general(lhs, rhs, (((0,), (0,)), ((), ())), preferred_element_type=F32)
        s_ref[h] = both[:, :dv] * s0 + both[:, dv:]
        qe = jnp.broadcast_to((q * e).astype(BF16), (rows, dk))
        o = jnp.dot(qe, s0.astype(BF16), preferred_element_type=F32)[0:1, :]
        o = o + jnp.sum(q * k, axis=-1, keepdims=True) * v
        outs.append(_head_out(o, r_all[:, h * dv:(h + 1) * dv], g_all[:, h * dv:(h + 1) * dv]))
    o_ref[pl.ds(j, 1), :] = jnp.concatenate(outs, axis=1)


def _gla_sample(proj, xg, w2, b2, g, s0, tp, ns, nh, dk, dv, q0, k0, v0, r0):
    rb = SUBLANES
    base = tp // rb
    qk_w = nh * dk
    v_w = nh * dv
    kern = functools.partial(_gla_sample_kernel, nh=nh, dk=dk, dv=dv)
    return pl.pallas_call(
        kern,
        grid=(ns,),
        in_specs=[
            pl.BlockSpec((rb, qk_w), lambda n: (base + n // rb, q0 // qk_w)),
            pl.BlockSpec((rb, qk_w), lambda n: (base + n // rb, k0 // qk_w)),
            pl.BlockSpec((rb, v_w), lambda n: (base + n // rb, v0 // v_w)),
            pl.BlockSpec((rb, v_w), lambda n: (base + n // rb, r0 // v_w)),
            pl.BlockSpec((rb, LANES), lambda n: (base + n // rb, 0)),
            pl.BlockSpec((LANES, qk_w), lambda n: (0, 0)),
            pl.BlockSpec((1, qk_w), lambda n: (0, 0)),
            pl.BlockSpec((1, v_w), lambda n: (0, 0)),
            pl.BlockSpec((None, nh, dk, dv), lambda n: (n, 0, 0, 0)),
        ],
        out_specs=[
            pl.BlockSpec((rb, v_w), lambda n: (n // rb, 0)),
            pl.BlockSpec((None, nh, dk, dv), lambda n: (n, 0, 0, 0)),
        ],
        out_shape=[
            jax.ShapeDtypeStruct((ns, v_w), F32),
            jax.ShapeDtypeStruct((ns, nh, dk, dv), F32),
        ],
        compiler_params=_cparams("arbitrary"),
        name="gla_sample",
    )(proj, proj, proj, proj, xg, w2, b2, g, s0)


def _out_proj_kernel(ys_ref, ogp_ref, ogt_ref, w1_ref, w2_ref, xp_ref, xt_ref, h_ref,
                     *, nfull, split):
    i = pl.program_id(0)
    w2 = w2_ref[...]
    acc = jnp.dot(ys_ref[...], w1_ref[...], preferred_element_type=F32)

    @pl.when(i < nfull)
    def _():
        h_ref[...] = xp_ref[...] + acc + jnp.dot(ogp_ref[...], w2, preferred_element_type=F32)

    @pl.when(i >= nfull)
    def _():
        if split:
            h_ref[0:split, :] = xp_ref[0:split, :] + acc[0:split] + jnp.dot(
                ogp_ref[0:split, :], w2, preferred_element_type=F32)
        h_ref[split:, :] = xt_ref[...] + acc[split:] + jnp.dot(
            ogt_ref[...], w2, preferred_element_type=F32)


def _out_proj(ys, og_p, og_t, w, x_p, x_t, tm, tn):
    t, dh = ys.shape
    d = w.shape[1]
    tp, tr = x_p.shape[0], x_t.shape[0]
    nfull = tp // tm
    split = tp - nfull * tm
    assert t == tp + tr and t == (nfull + 1) * tm and tm - split == tr
    last = pl.cdiv(tp, tm) - 1
    return pl.pallas_call(
        functools.partial(_out_proj_kernel, nfull=nfull, split=split),
        grid=(t // tm, d // tn),
        in_specs=[
            pl.BlockSpec((tm, dh), lambda i, j: (i, 0)),
            pl.BlockSpec((tm, dh), lambda i, j: (jnp.minimum(i, last), 0)),
            pl.BlockSpec((tr, dh), lambda i, j: (0, 0)),
            pl.BlockSpec((dh, tn), lambda i, j: (0, j)),
            pl.BlockSpec((dh, tn), lambda i, j: (1, j)),
            pl.BlockSpec((tm, tn), lambda i, j: (jnp.minimum(i, last), j)),
            pl.BlockSpec((tr, tn), lambda i, j: (0, j)),
        ],
        out_specs=pl.BlockSpec((tm, tn), lambda i, j: (i, j)),
        out_shape=jax.ShapeDtypeStruct((t, d), F32),
        compiler_params=_cparams("parallel", "parallel"),
        name="out_proj",
    )(ys, og_p, og_t, w, w, x_p, x_t)


def _norm_t_kernel(h_ref, g_ref, o_ref):
    o_ref[...] = _rms(h_ref[...], g_ref[...]).T.astype(BF16)


def _norm_t(h, g, tm):
    t, d = h.shape
    return pl.pallas_call(
        _norm_t_kernel,
        grid=(t // tm,),
        in_specs=[pl.BlockSpec((tm, d), lambda i: (i, 0)), pl.BlockSpec((1, d), lambda i: (0, 0))],
        out_specs=pl.BlockSpec((d, tm), lambda i: (0, i)),
        out_shape=jax.ShapeDtypeStruct((d, t), BF16),
        compiler_params=_cparams("parallel"),
        name="ffn_norm_t",
    )(h, g)


def _topk_rows(x, k):
    outs = []
    for _ in range(k):
        m = jnp.max(x, axis=0, keepdims=True)
        outs.append(m)
        x = jnp.where(x == m, -jnp.inf, x)
    return outs


def _route_kernel(wq_ref, hn_ref, k1_ref, k2_ref, s1_ref, c1_ref, s2_ref, e2_ref, tau_ref):
    half = k1_ref.shape[1]
    tb = hn_ref.shape[1]
    qt = jnp.dot(wq_ref[...], hn_ref[...], preferred_element_type=F32)
    s1 = jnp.dot(k1_ref[...].astype(BF16), qt[:half, :].astype(BF16), preferred_element_type=F32)
    s2 = jnp.dot(k2_ref[...].astype(BF16), qt[half:, :].astype(BF16), preferred_element_type=F32)
    s1_ref[...] = s1
    s2_ref[...] = s2
    for t in range(tb // LANES):
        sl = slice(t * LANES, (t + 1) * LANES)
        a = s1[:, sl]
        b = s2[:, sl]
        kk = PEER_TOPK
        v1 = _topk_rows(a, kk)
        v2l = _topk_rows(b, kk)
        v2 = jnp.concatenate(v2l, axis=0)
        cand = jnp.concatenate(
            [v1[0] + v2]
            + [v1[i] + v2[:kk // 2] for i in range(1, kk // 2)]
            + [jnp.concatenate(v1[kk // 2:], axis=0) + v2l[0]], axis=0)
        top = _topk_rows(cand, kk)
        z = jnp.zeros_like(top[0])
        for c in top:
            z = z + jnp.exp(c - top[0])
        tau_ref[:, sl] = top[PEER_TOPK - 1]
        c1_ref[:, sl] = jnp.exp(a - v1[0]) / z
        e2_ref[:, sl] = jnp.exp(b - v2[0:1, :])


def _route(wq_t, hn_t, k1, k2, tb):
    d, t = hn_t.shape
    nh, nk, half = k1.shape
    qd = 2 * half
    tab = jax.ShapeDtypeStruct((nh, nk, t), F32)
    tab_spec = pl.BlockSpec((None, nk, tb), lambda i, h: (h, 0, i))
    return pl.pallas_call(
        _route_kernel,
        grid=(t // tb, nh),
        in_specs=[
            pl.BlockSpec((qd, d), lambda i, h: (h, 0)),
            pl.BlockSpec((d, tb), lambda i, h: (0, i)),
            pl.BlockSpec((None, nk, half), lambda i, h: (h, 0, 0)),
            pl.BlockSpec((None, nk, half), lambda i, h: (h, 0, 0)),
        ],
        out_specs=[tab_spec, tab_spec, tab_spec, tab_spec,
                   pl.BlockSpec((None, 1, tb), lambda i, h: (h, 0, i))],
        out_shape=[tab, tab, tab, tab, jax.ShapeDtypeStruct((nh, 1, t), F32)],
        compiler_params=_cparams("parallel", "parallel"),
        name="peer_route",
    )(wq_t, hn_t, k1, k2)


GATE_ROWS = 32
OUT_ROWS = 512


K_CHUNK = 512
EXPERT_BLOCK = 512


def _peer_kernel(u_ref, vt_ref, hn_ref, s1_ref, c1_ref, s2_ref, e2_ref, tau_ref, o_ref,
                 w_scr, act_scr, a_scr):
    eb, d = u_ref.shape
    tb = hn_ref.shape[1]
    nh, nk, _ = s2_ref.shape
    nr = eb // nk

    @pl.when(pl.program_id(1) == 0)
    def _():
        o_ref[...] = jnp.zeros_like(o_ref)

    def gate_group(t, q):
        sl = slice(t * LANES, (t + 1) * LANES)
        rows = slice(q * GATE_ROWS, (q + 1) * GATE_ROWS)
        accs = [None] * nr
        for h in range(nh):
            s2t = s2_ref[h, rows, sl]
            e2t = e2_ref[h, rows, sl]
            tau = tau_ref[h, :, sl]
            for r in range(nr):
                ssum = s1_ref[r, h:h + 1, sl] + s2t
                term = jnp.where(ssum >= tau, c1_ref[r, h:h + 1, sl] * e2t, 0.0)
                accs[r] = term if accs[r] is None else accs[r] + term
        for r in range(nr):
            lo = r * nk + q * GATE_ROWS
            w_scr[lo:lo + GATE_ROWS, sl] = accs[r]

    groups = [(t, q) for t in range(tb // LANES) for q in range(nk // GATE_ROWS)]
    nkc = d // K_CHUNK
    per = -(-len(groups) // nkc)
    for kc in range(nkc):
        @pl.when(pl.program_id(1) >= 0)
        def _(kc=kc):
            ks = slice(kc * K_CHUNK, (kc + 1) * K_CHUNK)
            part = jnp.dot(u_ref[:, ks], hn_ref[ks, :], preferred_element_type=F32)
            if kc == 0:
                act_scr[...] = part
            else:
                act_scr[...] += part
            for t, q in groups[kc * per:(kc + 1) * per]:
                gate_group(t, q)
    a_scr[...] = (w_scr[...] * _gelu(act_scr[...])).astype(BF16)
    for dc in range(d // OUT_ROWS):
        dr = slice(dc * OUT_ROWS, (dc + 1) * OUT_ROWS)
        o_ref[dr, :] += jnp.dot(vt_ref[dr, :], a_scr[...], preferred_element_type=F32)


def _peer(u_bf, v_t, hn_t, s1, c1, s2, e2, tau, tb, eb):
    ne, d = u_bf.shape
    t = hn_t.shape[1]
    nh, nk, _ = s2.shape
    once = pl.Buffered(1)
    tab_spec = pl.BlockSpec((nh, nk, tb), lambda i, e: (0, 0, i), pipeline_mode=once)
    row_spec = pl.BlockSpec((eb // nk, nh, tb), lambda i, e: (e, 0, i))
    return pl.pallas_call(
        _peer_kernel,
        grid=(t // tb, ne // eb),
        in_specs=[
            pl.BlockSpec((eb, d), lambda i, e: (e, 0)),
            pl.BlockSpec((d, eb), lambda i, e: (0, e)),
            pl.BlockSpec((d, tb), lambda i, e: (0, i), pipeline_mode=once),
            row_spec, row_spec, tab_spec, tab_spec,
            pl.BlockSpec((nh, 1, tb), lambda i, e: (0, 0, i)),
        ],
        out_specs=pl.BlockSpec((d, tb), lambda i, e: (0, i)),
        out_shape=jax.ShapeDtypeStruct((d, t), F32),
        scratch_shapes=[pltpu.VMEM((eb, tb), F32), pltpu.VMEM((eb, tb), F32),
                        pltpu.VMEM((eb, tb), BF16)],
        compiler_params=pltpu.CompilerParams(
            dimension_semantics=("parallel", "arbitrary"), vmem_limit_bytes=PEER_VMEM_LIMIT),
        name="peer_experts",
    )(u_bf, v_t, hn_t, s1, c1, s2, e2, tau)


def _final_kernel(pt_ref, h_ref, g_ref, yp_ref, yt_ref, *, nprompt):
    i = pl.program_id(0)
    y = _rms(h_ref[...] + pt_ref[...].T, g_ref[...])

    @pl.when(i < nprompt)
    def _():
        yp_ref[...] = y

    @pl.when(i >= nprompt)
    def _():
        yt_ref[...] = y


def _final(peer_t, h, g, tp, tr):
    d, t = peer_t.shape
    nprompt = tp // tr
    return pl.pallas_call(
        functools.partial(_final_kernel, nprompt=nprompt),
        grid=(nprompt + 1,),
        in_specs=[
            pl.BlockSpec((d, tr), lambda i: (0, i)),
            pl.BlockSpec((tr, d), lambda i: (i, 0)),
            pl.BlockSpec((1, d), lambda i: (0, 0)),
        ],
        out_specs=[
            pl.BlockSpec((tr, d), lambda i: (jnp.minimum(i, nprompt - 1), 0)),
            pl.BlockSpec((tr, d), lambda i: (0, 0)),
        ],
        out_shape=[jax.ShapeDtypeStruct((tp, d), F32), jax.ShapeDtypeStruct((tr, d), F32)],
        compiler_params=_cparams("arbitrary"),
        name="final_norm",
    )(peer_t, h, g)


def _block_diag(w, ngrp):
    nslab, _, a, b = w.shape
    eye = jnp.eye(ngrp, dtype=w.dtype)
    full = w[:, :, :, None, :] * eye[None, :, None, :, None]
    return full.reshape(nslab, ngrp * a, ngrp * b)


def kernel(x_prompt, x_sample, state_s5_re, state_s5_im, state_gla, meta_tokens, norm_mix_g, w_in, s5_lam_re, s5_lam_im, s5_log_dt, s5_b_re, s5_b_im, s5_c_re, s5_c_im, s5_d, s5_w_glu, s5_b_glu, s5_norm_g, gla_w_gate2, gla_b_gate2, gla_norm_g, w_out, norm_ffn_g, peer_w_q, peer_keys, peer_u, peer_v, norm_final_g):
    nb, seq, d = x_prompt.shape
    ns = x_sample.shape[0]
    nm = meta_tokens.shape[0]
    depth = w_in.shape[0]
    assert depth == 1 and x_sample.shape[1] == 1
    tp = nb * seq
    assert tp % TAIL_ROWS == 0 and ns + nm <= TAIL_ROWS and ns % SUBLANES == 0
    assert seq % GLA_CHUNK == 0 and nm % SUBLANES == 0
    t_all = tp + TAIL_ROWS

    ngrp, nstate, gch = s5_b_re.shape[1:]
    d_ssm = ngrp * gch
    gps = LANES // gch
    nslab = ngrp // gps
    nh, dk, dv = state_gla.shape[2:]
    d_gla = nh * dv
    rank = gla_w_gate2.shape[1]
    q0 = d_ssm
    k0 = q0 + nh * dk
    v0 = k0 + nh * dk
    r0 = v0 + d_gla
    g0 = r0 + d_gla
    assert g0 + rank == w_in.shape[2] and d_ssm + d_gla == w_out.shape[1]

    x_p = x_prompt.reshape(tp, d)
    x_t = jnp.concatenate([x_sample.reshape(ns, d), meta_tokens,
                           jnp.zeros((TAIL_ROWS - ns - nm, d), F32)], axis=0)
    w_in_bf = w_in[0].astype(BF16)
    w_g1 = jnp.pad(w_in[0, :, g0:], ((0, 0), (0, LANES - rank))).astype(BF16)
    w_g2 = jnp.pad(gla_w_gate2[0], ((0, LANES - rank), (0, 0))).astype(BF16)
    lam_re = s5_lam_re[0].reshape(1, ngrp * nstate)
    lam_im = s5_lam_im[0].reshape(1, ngrp * nstate)
    logdt = jnp.repeat(s5_log_dt[0], nstate).reshape(1, ngrp * nstate)
    b_re4 = jnp.transpose(s5_b_re[0].reshape(nslab, gps, nstate, gch), (0, 1, 3, 2))
    b_im4 = jnp.transpose(s5_b_im[0].reshape(nslab, gps, nstate, gch), (0, 1, 3, 2))
    c_re4 = jnp.transpose(s5_c_re[0].reshape(nslab, gps, gch, nstate), (0, 1, 3, 2))
    c_im4 = jnp.transpose(s5_c_im[0].reshape(nslab, gps, gch, nstate), (0, 1, 3, 2))
    bre_bd = _block_diag(b_re4, gps)
    bim_bd = _block_diag(b_im4, gps)
    cre_bd = _block_diag(c_re4, gps)
    cim_bd = _block_diag(c_im4, gps)
    h0r = state_s5_re[0].reshape(ns, ngrp * nstate)
    h0i = state_s5_im[0].reshape(ns, ngrp * nstate)

    tm = _row_tile(t_all, (768, 512, 256))

    xn = _prenorm(x_p, x_t, norm_mix_g)
    proj, xg = _in_proj(xn, w_in_bf, w_g1, tm, PROJ_COLS, g0)

    y_raw, pr, pi_, sr, si = _s5(proj, h0r, h0i, lam_re, lam_im, logdt, bre_bd, bim_bd,
                                 cre_bd, cim_bd, s5_d, nb, seq, ns, nm)
    y_ssm = _glu_norm(y_raw, s5_w_glu[0].astype(BF16), s5_b_glu, s5_norm_g, TAIL_ROWS)

    og_p, gla_p = _gla_prompt(proj, xg, w_g2, gla_b_gate2, gla_norm_g, nb, seq, ns, nm,
                              nh, dk, dv, q0, k0, v0, r0)
    og_s, gla_s = _gla_sample(proj, xg, w_g2, gla_b_gate2, gla_norm_g, state_gla[0],
                              tp, ns, nh, dk, dv, q0, k0, v0, r0)
    og_t = jnp.concatenate([og_s.astype(BF16), jnp.zeros((TAIL_ROWS - ns, d_gla), BF16)], axis=0)

    w_o = w_out[0].astype(BF16)
    assert d_ssm == d_gla
    h = _out_proj(y_ssm, og_p, og_t, w_o, x_p, x_t, tm, PROJ_COLS)

    hn_t = _norm_t(h, norm_ffn_g, TAIL_ROWS)
    wq_t = peer_w_q[0].T.astype(BF16)
    s1, c1, s2, e2, tau = _route(wq_t, hn_t, peer_keys[0, :, 0], peer_keys[0, :, 1], tm)
    u_bf = peer_u[0].astype(BF16)
    v_t = peer_v[0].astype(BF16).T
    peer_t = _peer(u_bf, v_t, hn_t, jnp.transpose(s1, (1, 0, 2)), jnp.transpose(c1, (1, 0, 2)),
                   s2, e2, tau, tm, EXPERT_BLOCK)
    y_p, y_t = _final(peer_t, h, norm_final_g.reshape(1, d), tp, TAIL_ROWS)

    y_prompt = y_p.reshape(nb, seq, d)
    y_sample = y_t[:ns].reshape(ns, 1, d)
    return (y_prompt, y_sample,
            pr.reshape(1, nb, ngrp, nstate), pi_.reshape(1, nb, ngrp, nstate), gla_p[None],
            sr.reshape(1, ns, ngrp, nstate), si.reshape(1, ns, ngrp, nstate), gla_s[None])
```

```python
import functools

import jax
import jax.numpy as jnp
from jax import lax
from jax.experimental import pallas as pl
from jax.experimental.pallas import tpu as pltpu

F32 = jnp.float32
BF16 = jnp.bfloat16

EPS = 1e-6
GLA_TAU = 16.0
GLA_CHUNK = 64
GLA_SUB = 16
PEER_TOPK = 16
LANES = 128
SUBLANES = 8
TAIL_ROWS = 256
PROJ_COLS = 1024
WT_ROWS = 2048
WT_COLS = 256
VMEM_LIMIT = 56 * 1024 * 1024
PEER_VMEM_LIMIT = 60 * 1024 * 1024


def _cparams(*sem):
    return pltpu.CompilerParams(dimension_semantics=sem, vmem_limit_bytes=VMEM_LIMIT)


def _gelu(x):
    return 0.5 * x * (1.0 + jnp.tanh(0.7978845608028654 * (x + 0.044715 * (x * x * x))))


def _sigmoid(x):
    return 1.0 / (1.0 + jnp.exp(-x))


def _log_sigmoid(x):
    return jnp.minimum(x, 0.0) - jnp.log(1.0 + jnp.exp(-jnp.abs(x)))


def _row_tile(n, cands):
    for c in cands:
        if n % c == 0:
            return c
    raise ValueError(f"no row tile for {n}")


def _transpose_cast_kernel(x_ref, o_ref):
    xb = x_ref[...].astype(BF16)
    cb = xb.shape[1]
    eye = jnp.where(lax.broadcasted_iota(jnp.int32, (cb, cb), 0)
                    == lax.broadcasted_iota(jnp.int32, (cb, cb), 1), 1.0, 0.0).astype(BF16)
    o_ref[...] = lax.dot_general(eye, xb, (((1,), (1,)), ((), ())),
                                 preferred_element_type=F32).astype(BF16)


def _transpose_cast(x, rb, cb):
    r, c = x.shape
    return pl.pallas_call(
        _transpose_cast_kernel,
        grid=(r // rb, c // cb),
        in_specs=[pl.BlockSpec((rb, cb), lambda i, j: (i, j))],
        out_specs=pl.BlockSpec((cb, rb), lambda i, j: (j, i)),
        out_shape=jax.ShapeDtypeStruct((c, r), BF16),
        compiler_params=_cparams("parallel", "parallel"),
        name="transpose_cast",
    )(x)


def _rms(x, g):
    return x * lax.rsqrt(jnp.mean(x * x, axis=-1, keepdims=True) + EPS) * g


def _prenorm_kernel(xp_ref, xt_ref, g_ref, o_ref, *, nprompt):
    i = pl.program_id(0)

    @pl.when(i < nprompt)
    def _():
        o_ref[...] = _rms(xp_ref[...], g_ref[...]).astype(BF16)

    @pl.when(i >= nprompt)
    def _():
        o_ref[...] = _rms(xt_ref[...], g_ref[...]).astype(BF16)


def _prenorm(x_p, x_t, g):
    tp, d = x_p.shape
    tr = x_t.shape[0]
    nprompt = tp // tr
    return pl.pallas_call(
        functools.partial(_prenorm_kernel, nprompt=nprompt),
        grid=(nprompt + 1,),
        in_specs=[
            pl.BlockSpec((tr, d), lambda i: (jnp.minimum(i, nprompt - 1), 0)),
            pl.BlockSpec((tr, d), lambda i: (0, 0)),
            pl.BlockSpec((1, d), lambda i: (0, 0)),
        ],
        out_specs=pl.BlockSpec((tr, d), lambda i: (i, 0)),
        out_shape=jax.ShapeDtypeStruct((tp + tr, d), BF16),
        compiler_params=_cparams("arbitrary"),
        name="mix_norm",
    )(x_p, x_t, g)


def _in_proj_kernel(xn_ref, w_ref, wg_ref, o_ref, og_ref):
    @pl.when(pl.program_id(1) == 0)
    def _():
        og_ref[...] = jnp.dot(xn_ref[...], wg_ref[...], preferred_element_type=F32)

    o_ref[...] = jnp.dot(xn_ref[...], w_ref[...], preferred_element_type=F32)


def _in_proj(xn, w, wg, tm, tn, n):
    t, d = xn.shape
    return pl.pallas_call(
        _in_proj_kernel,
        grid=(t // tm, n // tn),
        in_specs=[
            pl.BlockSpec((tm, d), lambda i, j: (i, 0)),
            pl.BlockSpec((d, tn), lambda i, j: (0, j)),
            pl.BlockSpec((d, LANES), lambda i, j: (0, 0)),
        ],
        out_specs=[
            pl.BlockSpec((tm, tn), lambda i, j: (i, j)),
            pl.BlockSpec((tm, LANES), lambda i, j: (i, 0)),
        ],
        out_shape=[jax.ShapeDtypeStruct((t, n), F32), jax.ShapeDtypeStruct((t, LANES), F32)],
        compiler_params=_cparams("parallel", "arbitrary"),
        name="in_proj",
    )(xn, w, wg)


def _s5_kernel(u_ref, h0r_ref, h0i_ref, lr_ref, li_ref, ldt_ref, bre_ref, bim_ref,
               cre_ref, cim_ref, d_ref,
               y_ref, pr_ref, pi_ref, sr_ref, si_ref,
               x_scr, init_scr, up_scr, yp_scr, *, nb, seq, ns, nm):
    sw = lr_ref.shape[1]
    tp = nb * seq
    nseg = SUBLANES
    ls = seq // nseg

    lr = lr_ref[...]
    li = li_ref[...]
    dt = jnp.exp(ldt_ref[...])
    mag = jnp.exp(lr * dt)
    ang = li * dt
    ar = mag * jnp.cos(ang)
    ai = mag * jnp.sin(ang)
    den = lr * lr + li * li
    nr = ar - 1.0
    qr = (nr * lr + ai * li) / den
    qi = (ai * lr - nr * li) / den
    bre = bre_ref[...]
    bim = bim_ref[...]
    bcat = jnp.concatenate([qr * bre - qi * bim, qr * bim + qi * bre], axis=1).astype(BF16)
    ccat = jnp.concatenate([cre_ref[...], -cim_ref[...]], axis=0).astype(BF16)
    dvec = d_ref[...]
    mag_s = jnp.exp(lr * dt * float(ls))
    asr = mag_s * jnp.cos(ang * float(ls))
    asi = mag_s * jnp.sin(ang * float(ls))

    def project_out(h, u):
        return jnp.dot(h.astype(BF16), ccat, preferred_element_type=F32) + dvec * u

    u_s = u_ref[tp:tp + ns, :]
    x_s = jnp.dot(u_s.astype(BF16), bcat, preferred_element_type=F32)
    h0r = h0r_ref[...]
    h0i = h0i_ref[...]
    hr_s = ar * h0r - ai * h0i + x_s[:, :sw]
    hi_s = ar * h0i + ai * h0r + x_s[:, sw:]
    sr_ref[...] = hr_s
    si_ref[...] = hi_s
    y_ref[tp:tp + ns, :] = project_out(jnp.concatenate([hr_s, hi_s], axis=1), u_s)
    y_ref[tp + ns:, :] = jnp.zeros((y_ref.shape[0] - tp - ns, y_ref.shape[1]), F32)

    u_m = u_ref[tp + ns:tp + ns + nm, :]
    x_m = jnp.dot(u_m.astype(BF16), bcat, preferred_element_type=F32)
    mr = jnp.zeros((1, sw), F32)
    mi = jnp.zeros((1, sw), F32)
    for t in range(nm):
        mr, mi = (ar * mr - ai * mi + x_m[t:t + 1, :sw],
                  ar * mi + ai * mr + x_m[t:t + 1, sw:])

    ar8 = jnp.broadcast_to(ar, (nseg, sw))
    ai8 = jnp.broadcast_to(ai, (nseg, sw))
    unroll = 4 if ls % 4 == 0 else 1

    def scan_pass(store):
        def body(i, c):
            hr, hi = c
            r = pl.multiple_of(i * nseg, nseg)
            nhr = ar8 * hr - ai8 * hi + x_scr[pl.ds(r, nseg), 0:sw]
            nhi = ar8 * hi + ai8 * hr + x_scr[pl.ds(r, nseg), sw:2 * sw]
            if store:
                x_scr[pl.ds(r, nseg), 0:sw] = nhr
                x_scr[pl.ds(r, nseg), sw:2 * sw] = nhi
            return nhr, nhi
        return body

    for b in range(nb):
        def regroup_in(i, c, b=b):
            r = pl.multiple_of(i * nseg, nseg)
            up_scr[pl.ds(r, nseg), :] = u_ref[pl.ds(b * seq + i, nseg, stride=ls), :]
            return c
        lax.fori_loop(0, ls, regroup_in, 0, unroll=unroll)
        u_b = up_scr[...]
        x_scr[...] = jnp.dot(u_b.astype(BF16), bcat, preferred_element_type=F32)
        z = jnp.zeros((nseg, sw), F32)
        fr, fi = lax.fori_loop(0, ls, scan_pass(False), (z, z), unroll=unroll)
        cr, ci = mr, mi
        for k in range(nseg):
            init_scr[k:k + 1, 0:sw] = cr
            init_scr[k:k + 1, sw:2 * sw] = ci
            cr, ci = (asr * cr - asi * ci + fr[k:k + 1, :],
                      asr * ci + asi * cr + fi[k:k + 1, :])
        pr_ref[b:b + 1, :] = cr
        pi_ref[b:b + 1, :] = ci
        lax.fori_loop(0, ls, scan_pass(True), (init_scr[:, 0:sw], init_scr[:, sw:2 * sw]),
                      unroll=unroll)
        yp_scr[...] = project_out(x_scr[...], u_b)

        def regroup_out(i, c, b=b):
            r = pl.multiple_of(i * nseg, nseg)
            y_ref[pl.ds(b * seq + i, nseg, stride=ls), :] = yp_scr[pl.ds(r, nseg), :]
            return c
        lax.fori_loop(0, ls, regroup_out, 0, unroll=unroll)


def _s5(proj, h0r, h0i, lam_re, lam_im, logdt, bre_bd, bim_bd, cre_bd, cim_bd, dvec,
        nb, seq, ns, nm):
    t_all = proj.shape[0]
    nslab, cw, sw = bre_bd.shape
    d_ssm = nslab * cw
    kern = functools.partial(_s5_kernel, nb=nb, seq=seq, ns=ns, nm=nm)
    return pl.pallas_call(
        kern,
        grid=(nslab,),
        in_specs=[
            pl.BlockSpec((t_all, cw), lambda s: (0, s)),
            pl.BlockSpec((ns, sw), lambda s: (0, s)),
            pl.BlockSpec((ns, sw), lambda s: (0, s)),
            pl.BlockSpec((1, sw), lambda s: (0, s)),
            pl.BlockSpec((1, sw), lambda s: (0, s)),
            pl.BlockSpec((1, sw), lambda s: (0, s)),
            pl.BlockSpec((None, cw, sw), lambda s: (s, 0, 0)),
            pl.BlockSpec((None, cw, sw), lambda s: (s, 0, 0)),
            pl.BlockSpec((None, sw, cw), lambda s: (s, 0, 0)),
            pl.BlockSpec((None, sw, cw), lambda s: (s, 0, 0)),
            pl.BlockSpec((1, cw), lambda s: (0, s)),
        ],
        out_specs=[
            pl.BlockSpec((t_all, cw), lambda s: (0, s)),
            pl.BlockSpec((nb, sw), lambda s: (0, s)),
            pl.BlockSpec((nb, sw), lambda s: (0, s)),
            pl.BlockSpec((ns, sw), lambda s: (0, s)),
            pl.BlockSpec((ns, sw), lambda s: (0, s)),
        ],
        out_shape=[
            jax.ShapeDtypeStruct((t_all, d_ssm), F32),
            jax.ShapeDtypeStruct((nb, nslab * sw), F32),
            jax.ShapeDtypeStruct((nb, nslab * sw), F32),
            jax.ShapeDtypeStruct((ns, nslab * sw), F32),
            jax.ShapeDtypeStruct((ns, nslab * sw), F32),
        ],
        scratch_shapes=[pltpu.VMEM((seq, 2 * sw), F32), pltpu.VMEM((SUBLANES, 2 * sw), F32),
                        pltpu.VMEM((seq, cw), F32), pltpu.VMEM((seq, cw), F32)],
        compiler_params=_cparams("parallel"),
        name="s5_scan",
    )(proj, h0r, h0i, lam_re, lam_im, logdt, bre_bd, bim_bd, cre_bd, cim_bd, dvec)


def _glu_norm_kernel(y_ref, w_ref, b_ref, g_ref, o_ref):
    z = _gelu(y_ref[...])
    gate = jnp.dot(z.astype(BF16), w_ref[...], preferred_element_type=F32) + b_ref[...]
    zz = z * _sigmoid(gate)
    s = lax.rsqrt(jnp.mean(zz * zz, axis=-1, keepdims=True) + EPS)
    o_ref[...] = (zz * s * g_ref[...]).astype(o_ref.dtype)


def _glu_norm(y, w, b, g, tm):
    t, d = y.shape
    return pl.pallas_call(
        _glu_norm_kernel,
        grid=(t // tm,),
        in_specs=[
            pl.BlockSpec((tm, d), lambda i: (i, 0)),
            pl.BlockSpec((d, d), lambda i: (0, 0)),
            pl.BlockSpec((1, d), lambda i: (0, 0)),
            pl.BlockSpec((1, d), lambda i: (0, 0)),
        ],
        out_specs=pl.BlockSpec((tm, d), lambda i: (i, 0)),
        out_shape=jax.ShapeDtypeStruct((t, d), BF16),
        compiler_params=_cparams("parallel"),
        name="s5_glu_norm",
    )(y, w, b, g)


def _cumsum_rows(x):
    n = x.shape[0]
    row = lax.broadcasted_iota(jnp.int32, x.shape, 0)
    s = 1
    while s < n:
        x = x + jnp.where(row >= s, pltpu.roll(x, s, axis=0), 0.0)
        s *= 2
    return x


def _gate(xg, w2, b2):
    pre = jnp.dot(xg.astype(BF16), w2, preferred_element_type=F32) + b2
    return _log_sigmoid(pre) * (1.0 / GLA_TAU)


def _head_out(o, r, g):
    o = o * lax.rsqrt(jnp.mean(o * o, axis=-1, keepdims=True) + EPS)
    return o * g * (r * _sigmoid(r))


def _gla_prompt_kernel(q_ref, k_ref, v_ref, r_ref, xg_ref, kt_ref, vt_ref, xgt_ref,
                       w2_ref, b2_ref, g_ref, o_ref, s_ref, st_scr, *, seq, ns, nm):
    dk = q_ref.shape[1]
    scale = float(dk) ** -0.5
    w2 = w2_ref[...]
    b2 = b2_ref[...]
    g = g_ref[...]

    lgm = _gate(xgt_ref[ns:ns + nm, :], w2, b2)
    bm = _cumsum_rows(lgm)
    kdm = kt_ref[ns:ns + nm, :] * jnp.exp(bm[nm - 1:nm, :] - bm)
    st_scr[...] = lax.dot_general(vt_ref[ns:ns + nm, :].astype(BF16), kdm.astype(BF16),
                                  (((0,), (0,)), ((), ())), preferred_element_type=F32)

    c = GLA_CHUNK
    nsub = c // GLA_SUB

    def chunk(ci, carry):
        r0 = pl.multiple_of(ci * c, c)
        q = q_ref[pl.ds(r0, c), :] * scale
        k = k_ref[pl.ds(r0, c), :]
        v = v_ref[pl.ds(r0, c), :]
        vb = v.astype(BF16)
        lg = _gate(xg_ref[pl.ds(r0, c), :], w2, b2)
        b = _cumsum_rows(lg)
        bl = b[c - 1:c, :]
        st = st_scr[...]
        o_inter = lax.dot_general((q * jnp.exp(b)).astype(BF16), st.astype(BF16),
                                  (((1,), (1,)), ((), ())), preferred_element_type=F32)
        outs = []
        for sb in range(nsub):
            lo = sb * GLA_SUB
            hi = lo + GLA_SUB
            beta = b[lo - 1:lo, :] if sb > 0 else jnp.zeros((1, dk), F32)
            qs = q[lo:hi, :] * jnp.exp(b[lo:hi, :] - beta)
            ks = k[0:hi, :] * jnp.exp(beta - b[0:hi, :])
            sc = lax.dot_general(qs.astype(BF16), ks.astype(BF16),
                                 (((1,), (1,)), ((), ())), preferred_element_type=F32)
            rowi = lax.broadcasted_iota(jnp.int32, (GLA_SUB, hi), 0)
            coli = lax.broadcasted_iota(jnp.int32, (GLA_SUB, hi), 1)
            sc = jnp.where(coli <= rowi + lo, sc, 0.0)
            outs.append(jnp.dot(sc.astype(BF16), vb[0:hi, :], preferred_element_type=F32))
        o = o_inter + jnp.concatenate(outs, axis=0)
        o_ref[pl.ds(r0, c), :] = _head_out(o, r_ref[pl.ds(r0, c), :], g).astype(o_ref.dtype)
        kd = k * jnp.exp(bl - b)
        st_scr[...] = jnp.exp(bl) * st + lax.dot_general(
            vb, kd.astype(BF16), (((0,), (0,)), ((), ())), preferred_element_type=F32)
        return carry

    lax.fori_loop(0, seq // c, chunk, 0, unroll=min(8, seq // c))
    s_ref[...] = st_scr[...].T


def _gla_prompt(proj, xg, w2, b2, g, nb, seq, ns, nm, nh, dk, dv, q0, k0, v0, r0):
    tp = nb * seq
    tail_blk = tp // TAIL_ROWS
    rb = seq
    kern = functools.partial(_gla_prompt_kernel, seq=seq, ns=ns, nm=nm)
    return pl.pallas_call(
        kern,
        grid=(nb, nh),
        in_specs=[
            pl.BlockSpec((rb, dk), lambda b, h: (b, q0 // dk + h)),
            pl.BlockSpec((rb, dk), lambda b, h: (b, k0 // dk + h)),
            pl.BlockSpec((rb, dv), lambda b, h: (b, v0 // dv + h)),
            pl.BlockSpec((rb, dv), lambda b, h: (b, r0 // dv + h)),
            pl.BlockSpec((rb, LANES), lambda b, h: (b, 0)),
            pl.BlockSpec((TAIL_ROWS, dk), lambda b, h: (tail_blk, k0 // dk + h)),
            pl.BlockSpec((TAIL_ROWS, dv), lambda b, h: (tail_blk, v0 // dv + h)),
            pl.BlockSpec((TAIL_ROWS, LANES), lambda b, h: (tail_blk, 0)),
            pl.BlockSpec((LANES, dk), lambda b, h: (0, h)),
            pl.BlockSpec((1, dk), lambda b, h: (0, h)),
            pl.BlockSpec((1, dv), lambda b, h: (0, h)),
        ],
        out_specs=[
            pl.BlockSpec((rb, dv), lambda b, h: (b, h)),
            pl.BlockSpec((None, None, dk, dv), lambda b, h: (b, h, 0, 0)),
        ],
        out_shape=[
            jax.ShapeDtypeStruct((tp, nh * dv), BF16),
            jax.ShapeDtypeStruct((nb, nh, dk, dv), F32),
        ],
        scratch_shapes=[pltpu.VMEM((dv, dk), F32)],
        compiler_params=_cparams("parallel", "parallel"),
        name="gla_prompt",
    )(proj, proj, proj, proj, xg, proj, proj, xg, w2, b2, g)


def _gla_sample_kernel(q_ref, k_ref, v_ref, r_ref, xg_ref, w2_ref, b2_ref, g_ref, s0_ref,
                       o_ref, s_ref, *, nh, dk, dv):
    n = pl.program_id(0)
    j = n % SUBLANES
    scale = float(dk) ** -0.5
    lg_all = _gate(xg_ref[pl.ds(j, 1), :], w2_ref[...], b2_ref[...])
    q_all = q_ref[pl.ds(j, 1), :] * scale
    k_all = k_ref[pl.ds(j, 1), :]
    v_all = v_ref[pl.ds(j, 1), :]
    r_all = r_ref[pl.ds(j, 1), :]
    g_all = g_ref[...]
    rows = 2 * SUBLANES
    rk = lax.broadcasted_iota(jnp.int32, (rows, dk), 0)
    rv = lax.broadcasted_iota(jnp.int32, (rows, dv), 0)
    outs = []
    for h in range(nh):
        lg = lg_all[:, h * dk:(h + 1) * dk]
        q = q_all[:, h * dk:(h + 1) * dk]
        k = k_all[:, h * dk:(h + 1) * dk]
        v = v_all[:, h * dv:(h + 1) * dv]
        e = jnp.exp(lg)
        s0 = s0_ref[h]
        e_hi = e.astype(BF16)
        e_mid = (e - e_hi.astype(F32)).astype(BF16)
        e_lo = (e - e_hi.astype(F32) - e_mid.astype(F32)).astype(BF16)
        def rows_of(x, w):
            return jnp.broadcast_to(x.astype(F32), (rows, w))
        lhs = jnp.where(rk == 0, rows_of(e_hi, dk),
              jnp.where(rk == 1, rows_of(e_mid, dk),
              jnp.where(rk == 2, rows_of(e_lo, dk),
              jnp.where(rk == 3, rows_of(k, dk), 0.0)))).astype(BF16)
        ones_part = jnp.where(rv < 3, 1.0, 0.0)
        v_part = jnp.where(rv == 3, rows_of(v, dv), 0.0)
        rhs = jnp.concatenate([ones_part, v_part], axis=1).astype(BF16)
        both = lax.dot_general(lhs, rhs, (((0,), (0,)), ((), ())), preferred_element_type=F32)
        s_ref[h] = both[:, :dv] * s0 + both[:, dv:]
        qe = jnp.broadcast_to((q * e).astype(BF16), (rows, dk))
        o = jnp.dot(qe, s0.astype(BF16), preferred_element_type=F32)[0:1, :]
        o = o + jnp.sum(q * k, axis=-1, keepdims=True) * v
        outs.append(_head_out(o, r_all[:, h * dv:(h + 1) * dv], g_all[:, h * dv:(h + 1) * dv]))
    o_ref[pl.ds(j, 1), :] = jnp.concatenate(outs, axis=1)


def _gla_sample(proj, xg, w2, b2, g, s0, tp, ns, nh, dk, dv, q0, k0, v0, r0):
    rb = SUBLANES
    base = tp // rb
    qk_w = nh * dk
    v_w = nh * dv
    kern = functools.partial(_gla_sample_kernel, nh=nh, dk=dk, dv=dv)
    return pl.pallas_call(
        kern,
        grid=(ns,),
        in_specs=[
            pl.BlockSpec((rb, qk_w), lambda n: (base + n // rb, q0 // qk_w)),
            pl.BlockSpec((rb, qk_w), lambda n: (base + n // rb, k0 // qk_w)),
            pl.BlockSpec((rb, v_w), lambda n: (base + n // rb, v0 // v_w)),
            pl.BlockSpec((rb, v_w), lambda n: (base + n // rb, r0 // v_w)),
            pl.BlockSpec((rb, LANES), lambda n: (base + n // rb, 0)),
            pl.BlockSpec((LANES, qk_w), lambda n: (0, 0)),
            pl.BlockSpec((1, qk_w), lambda n: (0, 0)),
            pl.BlockSpec((1, v_w), lambda n: (0, 0)),
            pl.BlockSpec((None, nh, dk, dv), lambda n: (n, 0, 0, 0)),
        ],
        out_specs=[
            pl.BlockSpec((rb, v_w), lambda n: (n // rb, 0)),
            pl.BlockSpec((None, nh, dk, dv), lambda n: (n, 0, 0, 0)),
        ],
        out_shape=[
            jax.ShapeDtypeStruct((ns, v_w), F32),
            jax.ShapeDtypeStruct((ns, nh, dk, dv), F32),
        ],
        compiler_params=_cparams("arbitrary"),
        name="gla_sample",
    )(proj, proj, proj, proj, xg, w2, b2, g, s0)


def _out_proj_kernel(ys_ref, ogp_ref, ogt_ref, w1_ref, w2_ref, xp_ref, xt_ref, h_ref,
                     *, nfull, split):
    i = pl.program_id(0)
    w2 = w2_ref[...]
    acc = jnp.dot(ys_ref[...], w1_ref[...], preferred_element_type=F32)

    @pl.when(i < nfull)
    def _():
        h_ref[...] = xp_ref[...] + acc + jnp.dot(ogp_ref[...], w2, preferred_element_type=F32)

    @pl.when(i >= nfull)
    def _():
        if split:
            h_ref[0:split, :] = xp_ref[0:split, :] + acc[0:split] + jnp.dot(
                ogp_ref[0:split, :], w2, preferred_element_type=F32)
        h_ref[split:, :] = xt_ref[...] + acc[split:] + jnp.dot(
            ogt_ref[...], w2, preferred_element_type=F32)


def _out_proj(ys, og_p, og_t, w, x_p, x_t, tm, tn):
    t, dh = ys.shape
    d = w.shape[1]
    tp, tr = x_p.shape[0], x_t.shape[0]
    nfull = tp // tm
    split = tp - nfull * tm
    assert t == tp + tr and t == (nfull + 1) * tm and tm - split == tr
    last = pl.cdiv(tp, tm) - 1
    return pl.pallas_call(
        functools.partial(_out_proj_kernel, nfull=nfull, split=split),
        grid=(t // tm, d // tn),
        in_specs=[
            pl.BlockSpec((tm, dh), lambda i, j: (i, 0)),
            pl.BlockSpec((tm, dh), lambda i, j: (jnp.minimum(i, last), 0)),
            pl.BlockSpec((tr, dh), lambda i, j: (0, 0)),
            pl.BlockSpec((dh, tn), lambda i, j: (0, j)),
            pl.BlockSpec((dh, tn), lambda i, j: (1, j)),
            pl.BlockSpec((tm, tn), lambda i, j: (jnp.minimum(i, last), j)),
            pl.BlockSpec((tr, tn), lambda i, j: (0, j)),
        ],
        out_specs=pl.BlockSpec((tm, tn), lambda i, j: (i, j)),
        out_shape=jax.ShapeDtypeStruct((t, d), F32),
        compiler_params=_cparams("parallel", "parallel"),
        name="out_proj",
    )(ys, og_p, og_t, w, w, x_p, x_t)


def _norm_t_kernel(h_ref, g_ref, o_ref):
    o_ref[...] = _rms(h_ref[...], g_ref[...]).T.astype(BF16)


def _norm_t(h, g, tm):
    t, d = h.shape
    return pl.pallas_call(
        _norm_t_kernel,
        grid=(t // tm,),
        in_specs=[pl.BlockSpec((tm, d), lambda i: (i, 0)), pl.BlockSpec((1, d), lambda i: (0, 0))],
        out_specs=pl.BlockSpec((d, tm), lambda i: (0, i)),
        out_shape=jax.ShapeDtypeStruct((d, t), BF16),
        compiler_params=_cparams("parallel"),
        name="ffn_norm_t",
    )(h, g)


def _topk_rows(x, k):
    outs = []
    for _ in range(k):
        m = jnp.max(x, axis=0, keepdims=True)
        outs.append(m)
        x = jnp.where(x == m, -jnp.inf, x)
    return outs


def _route_kernel(wq_ref, hn_ref, k1_ref, k2_ref, s1_ref, c1_ref, s2_ref, e2_ref, tau_ref):
    half = k1_ref.shape[1]
    tb = hn_ref.shape[1]
    qt = jnp.dot(wq_ref[...], hn_ref[...], preferred_element_type=F32)
    s1 = jnp.dot(k1_ref[...].astype(BF16), qt[:half, :].astype(BF16), preferred_element_type=F32)
    s2 = jnp.dot(k2_ref[...].astype(BF16), qt[half:, :].astype(BF16), preferred_element_type=F32)
    s1_ref[...] = s1
    s2_ref[...] = s2
    for t in range(tb // LANES):
        sl = slice(t * LANES, (t + 1) * LANES)
        a = s1[:, sl]
        b = s2[:, sl]
        kk = PEER_TOPK
        v1 = _topk_rows(a, kk)
        v2l = _topk_rows(b, kk)
        v2 = jnp.concatenate(v2l, axis=0)
        cand = jnp.concatenate(
            [v1[0] + v2]
            + [v1[i] + v2[:kk // 2] for i in range(1, kk // 2)]
            + [jnp.concatenate(v1[kk // 2:], axis=0) + v2l[0]], axis=0)
        top = _topk_rows(cand, kk)
        z = jnp.zeros_like(top[0])
        for c in top:
            z = z + jnp.exp(c - top[0])
        tau_ref[:, sl] = top[PEER_TOPK - 1]
        c1_ref[:, sl] = jnp.exp(a - v1[0]) / z
        e2_ref[:, sl] = jnp.exp(b - v2[0:1, :])


def _route(wq_t, hn_t, k1, k2, tb):
    d, t = hn_t.shape
    nh, nk, half = k1.shape
    qd = 2 * half
    tab = jax.ShapeDtypeStruct((nh, nk, t), F32)
    tab_spec = pl.BlockSpec((None, nk, tb), lambda i, h: (h, 0, i))
    return pl.pallas_call(
        _route_kernel,
        grid=(t // tb, nh),
        in_specs=[
            pl.BlockSpec((qd, d), lambda i, h: (h, 0)),
            pl.BlockSpec((d, tb), lambda i, h: (0, i)),
            pl.BlockSpec((None, nk, half), lambda i, h: (h, 0, 0)),
            pl.BlockSpec((None, nk, half), lambda i, h: (h, 0, 0)),
        ],
        out_specs=[tab_spec, tab_spec, tab_spec, tab_spec,
                   pl.BlockSpec((None, 1, tb), lambda i, h: (h, 0, i))],
        out_shape=[tab, tab, tab, tab, jax.ShapeDtypeStruct((nh, 1, t), F32)],
        compiler_params=_cparams("parallel", "parallel"),
        name="peer_route",
    )(wq_t, hn_t, k1, k2)


GATE_ROWS = 32
OUT_ROWS = 512


K_CHUNK = 512
EXPERT_BLOCK = 512


def _peer_kernel(u_ref, vt_ref, hn_ref, s1_ref, c1_ref, s2_ref, e2_ref, tau_ref, o_ref,
                 w_scr, act_scr, a_scr):
    eb, d = u_ref.shape
    tb = hn_ref.shape[1]
    nh, nk, _ = s2_ref.shape
    nr = eb // nk

    @pl.when(pl.program_id(1) == 0)
    def _():
        o_ref[...] = jnp.zeros_like(o_ref)

    def gate_group(t, q):
        sl = slice(t * LANES, (t + 1) * LANES)
        rows = slice(q * GATE_ROWS, (q + 1) * GATE_ROWS)
        accs = [None] * nr
        for h in range(nh):
            s2t = s2_ref[h, rows, sl]
            e2t = e2_ref[h, rows, sl]
            tau = tau_ref[h, :, sl]
            for r in range(nr):
                ssum = s1_ref[r, h:h + 1, sl] + s2t
                term = jnp.where(ssum >= tau, c1_ref[r, h:h + 1, sl] * e2t, 0.0)
                accs[r] = term if accs[r] is None else accs[r] + term
        for r in range(nr):
            lo = r * nk + q * GATE_ROWS
            w_scr[lo:lo + GATE_ROWS, sl] = accs[r]

    groups = [(t, q) for t in range(tb // LANES) for q in range(nk // GATE_ROWS)]
    nkc = d // K_CHUNK
    per = -(-len(groups) // nkc)
    for kc in range(nkc):
        @pl.when(pl.program_id(1) >= 0)
        def _(kc=kc):
            ks = slice(kc * K_CHUNK, (kc + 1) * K_CHUNK)
            part = jnp.dot(u_ref[:, ks], hn_ref[ks, :], preferred_element_type=F32)
            if kc == 0:
                act_scr[...] = part
            else:
                act_scr[...] += part
            for t, q in groups[kc * per:(kc + 1) * per]:
                gate_group(t, q)
    a_scr[...] = (w_scr[...] * _gelu(act_scr[...])).astype(BF16)
    for dc in range(d // OUT_ROWS):
        dr = slice(dc * OUT_ROWS, (dc + 1) * OUT_ROWS)
        o_ref[dr, :] += jnp.dot(vt_ref[dr, :], a_scr[...], preferred_element_type=F32)


def _peer(u_bf, v_t, hn_t, s1, c1, s2, e2, tau, tb, eb):
    ne, d = u_bf.shape
    t = hn_t.shape[1]
    nh, nk, _ = s2.shape
    once = pl.Buffered(1)
    tab_spec = pl.BlockSpec((nh, nk, tb), lambda i, e: (0, 0, i), pipeline_mode=once)
    row_spec = pl.BlockSpec((eb // nk, nh, tb), lambda i, e: (e, 0, i))
    return pl.pallas_call(
        _peer_kernel,
        grid=(t // tb, ne // eb),
        in_specs=[
            pl.BlockSpec((eb, d), lambda i, e: (e, 0)),
            pl.BlockSpec((d, eb), lambda i, e: (0, e)),
            pl.BlockSpec((d, tb), lambda i, e: (0, i), pipeline_mode=once),
            row_spec, row_spec, tab_spec, tab_spec,
            pl.BlockSpec((nh, 1, tb), lambda i, e: (0, 0, i)),
        ],
        out_specs=pl.BlockSpec((d, tb), lambda i, e: (0, i)),
        out_shape=jax.ShapeDtypeStruct((d, t), F32),
        scratch_shapes=[pltpu.VMEM((eb, tb), F32), pltpu.VMEM((eb, tb), F32),
                        pltpu.VMEM((eb, tb), BF16)],
        compiler_params=pltpu.CompilerParams(
            dimension_semantics=("parallel", "arbitrary"), vmem_limit_bytes=PEER_VMEM_LIMIT),
        name="peer_experts",
    )(u_bf, v_t, hn_t, s1, c1, s2, e2, tau)


def _final_kernel(pt_ref, h_ref, g_ref, yp_ref, yt_ref, *, nprompt):
    i = pl.program_id(0)
    y = _rms(h_ref[...] + pt_ref[...].T, g_ref[...])

    @pl.when(i < nprompt)
    def _():
        yp_ref[...] = y

    @pl.when(i >= nprompt)
    def _():
        yt_ref[...] = y


def _final(peer_t, h, g, tp, tr):
    d, t = peer_t.shape
    nprompt = tp // tr
    return pl.pallas_call(
        functools.partial(_final_kernel, nprompt=nprompt),
        grid=(nprompt + 1,),
        in_specs=[
            pl.BlockSpec((d, tr), lambda i: (0, i)),
            pl.BlockSpec((tr, d), lambda i: (i, 0)),
            pl.BlockSpec((1, d), lambda i: (0, 0)),
        ],
        out_specs=[
            pl.BlockSpec((tr, d), lambda i: (jnp.minimum(i, nprompt - 1), 0)),
            pl.BlockSpec((tr, d), lambda i: (0, 0)),
        ],
        out_shape=[jax.ShapeDtypeStruct((tp, d), F32), jax.ShapeDtypeStruct((tr, d), F32)],
        compiler_params=_cparams("arbitrary"),
        name="final_norm",
    )(peer_t, h, g)


def _block_diag(w, ngrp):
    nslab, _, a, b = w.shape
    eye = jnp.eye(ngrp, dtype=w.dtype)
    full = w[:, :, :, None, :] * eye[None, :, None, :, None]
    return full.reshape(nslab, ngrp * a, ngrp * b)


def kernel(x_prompt, x_sample, state_s5_re, state_s5_im, state_gla, meta_tokens, norm_mix_g, w_in, s5_lam_re, s5_lam_im, s5_log_dt, s5_b_re, s5_b_im, s5_c_re, s5_c_im, s5_d, s5_w_glu, s5_b_glu, s5_norm_g, gla_w_gate2, gla_b_gate2, gla_norm_g, w_out, norm_ffn_g, peer_w_q, peer_keys, peer_u, peer_v, norm_final_g):
    nb, seq, d = x_prompt.shape
    ns = x_sample.shape[0]
    nm = meta_tokens.shape[0]
    depth = w_in.shape[0]
    assert depth == 1 and x_sample.shape[1] == 1
    tp = nb * seq
    assert tp % TAIL_ROWS == 0 and ns + nm <= TAIL_ROWS and ns % SUBLANES == 0
    assert seq % GLA_CHUNK == 0 and nm % SUBLANES == 0
    t_all = tp + TAIL_ROWS

    ngrp, nstate, gch = s5_b_re.shape[1:]
    d_ssm = ngrp * gch
    gps = LANES // gch
    nslab = ngrp // gps
    nh, dk, dv = state_gla.shape[2:]
    d_gla = nh * dv
    rank = gla_w_gate2.shape[1]
    q0 = d_ssm
    k0 = q0 + nh * dk
    v0 = k0 + nh * dk
    r0 = v0 + d_gla
    g0 = r0 + d_gla
    assert g0 + rank == w_in.shape[2] and d_ssm + d_gla == w_out.shape[1]

    x_p = x_prompt.reshape(tp, d)
    x_t = jnp.concatenate([x_sample.reshape(ns, d), meta_tokens,
                           jnp.zeros((TAIL_ROWS - ns - nm, d), F32)], axis=0)
    w_in_bf = w_in[0].astype(BF16)
    w_g1 = jnp.pad(w_in[0, :, g0:], ((0, 0), (0, LANES - rank))).astype(BF16)
    w_g2 = jnp.pad(gla_w_gate2[0], ((0, LANES - rank), (0, 0))).astype(BF16)
    lam_re = s5_lam_re[0].reshape(1, ngrp * nstate)
    lam_im = s5_lam_im[0].reshape(1, ngrp * nstate)
    logdt = jnp.repeat(s5_log_dt[0], nstate).reshape(1, ngrp * nstate)
    b_re4 = jnp.transpose(s5_b_re[0].reshape(nslab, gps, nstate, gch), (0, 1, 3, 2))
    b_im4 = jnp.transpose(s5_b_im[0].reshape(nslab, gps, nstate, gch), (0, 1, 3, 2))
    c_re4 = jnp.transpose(s5_c_re[0].reshape(nslab, gps, gch, nstate), (0, 1, 3, 2))
    c_im4 = jnp.transpose(s5_c_im[0].reshape(nslab, gps, gch, nstate), (0, 1, 3, 2))
    bre_bd = _block_diag(b_re4, gps)
    bim_bd = _block_diag(b_im4, gps)
    cre_bd = _block_diag(c_re4, gps)
    cim_bd = _block_diag(c_im4, gps)
    h0r = state_s5_re[0].reshape(ns, ngrp * nstate)
    h0i = state_s5_im[0].reshape(ns, ngrp * nstate)

    tm = _row_tile(t_all, (768, 512, 256))

    xn = _prenorm(x_p, x_t, norm_mix_g)
    proj, xg = _in_proj(xn, w_in_bf, w_g1, tm, PROJ_COLS, g0)

    y_raw, pr, pi_, sr, si = _s5(proj, h0r, h0i, lam_re, lam_im, logdt, bre_bd, bim_bd,
                                 cre_bd, cim_bd, s5_d, nb, seq, ns, nm)
    y_ssm = _glu_norm(y_raw, s5_w_glu[0].astype(BF16), s5_b_glu, s5_norm_g, TAIL_ROWS)

    og_p, gla_p = _gla_prompt(proj, xg, w_g2, gla_b_gate2, gla_norm_g, nb, seq, ns, nm,
                              nh, dk, dv, q0, k0, v0, r0)
    og_s, gla_s = _gla_sample(proj, xg, w_g2, gla_b_gate2, gla_norm_g, state_gla[0],
                              tp, ns, nh, dk, dv, q0, k0, v0, r0)
    og_t = jnp.concatenate([og_s.astype(BF16), jnp.zeros((TAIL_ROWS - ns, d_gla), BF16)], axis=0)

    w_o = w_out[0].astype(BF16)
    assert d_ssm == d_gla
    h = _out_proj(y_ssm, og_p, og_t, w_o, x_p, x_t, tm, PROJ_COLS)

    hn_t = _norm_t(h, norm_ffn_g, TAIL_ROWS)
    wq_t = _transpose_cast(peer_w_q[0], WT_ROWS, WT_COLS)
    s1, c1, s2, e2, tau = _route(wq_t, hn_t, peer_keys[0, :, 0], peer_keys[0, :, 1], tm)
    u_bf = peer_u[0].astype(BF16)
    v_t = _transpose_cast(peer_v[0], WT_ROWS, WT_COLS)
    peer_t = _peer(u_bf, v_t, hn_t, jnp.transpose(s1, (1, 0, 2)), jnp.transpose(c1, (1, 0, 2)),
                   s2, e2, tau, tm, EXPERT_BLOCK)
    y_p, y_t = _final(peer_t, h, norm_final_g.reshape(1, d), tp, TAIL_ROWS)

    y_prompt = y_p.reshape(nb, seq, d)
    y_sample = y_t[:ns].reshape(ns, 1, d)
    return (y_prompt, y_sample,
            pr.reshape(1, nb, ngrp, nstate), pi_.reshape(1, nb, ngrp, nstate), gla_p[None],
            sr.reshape(1, ns, ngrp, nstate), si.reshape(1, ns, ngrp, nstate), gla_s[None])
```

```python
import functools

import jax
import jax.numpy as jnp
from jax import lax
from jax.experimental import pallas as pl
from jax.experimental.pallas import tpu as pltpu

F32 = jnp.float32
BF16 = jnp.bfloat16

EPS = 1e-6
GLA_TAU = 16.0
GLA_CHUNK = 64
GLA_SUB = 16
PEER_TOPK = 16
LANES = 128
SUBLANES = 8
TAIL_ROWS = 256
PROJ_COLS = 1024
MXU_DEPTH = 256
WT_ROWS = 2048
WT_COLS = 1024
VMEM_LIMIT = 56 * 1024 * 1024
PEER_VMEM_LIMIT = 60 * 1024 * 1024


def _cparams(*sem):
    return pltpu.CompilerParams(dimension_semantics=sem, vmem_limit_bytes=VMEM_LIMIT)


def _gelu(x):
    return 0.5 * x * (1.0 + jnp.tanh(0.7978845608028654 * (x + 0.044715 * (x * x * x))))


def _sigmoid(x):
    return 1.0 / (1.0 + jnp.exp(-x))


def _log_sigmoid(x):
    return jnp.minimum(x, 0.0) - jnp.log(1.0 + jnp.exp(-jnp.abs(x)))


def _row_tile(n, cands):
    for c in cands:
        if n % c == 0:
            return c
    raise ValueError(f"no row tile for {n}")


def _transpose_cast_kernel(x_ref, o_ref):
    k = MXU_DEPTH
    eye = jnp.where(lax.broadcasted_iota(jnp.int32, (k, k), 0)
                    == lax.broadcasted_iota(jnp.int32, (k, k), 1), 1.0, 0.0).astype(BF16)
    for c in range(x_ref.shape[1] // k):
        xb = x_ref[:, c * k:(c + 1) * k].astype(BF16)
        o_ref[c * k:(c + 1) * k, :] = lax.dot_general(
            eye, xb, (((1,), (1,)), ((), ())), preferred_element_type=F32).astype(BF16)


def _cast_kernel(x_ref, o_ref):
    o_ref[...] = x_ref[...].astype(BF16)


def _cast_bf16(x, rb):
    r, c = x.shape
    return pl.pallas_call(
        _cast_kernel,
        grid=(r // rb,),
        in_specs=[pl.BlockSpec((rb, c), lambda i: (i, 0))],
        out_specs=pl.BlockSpec((rb, c), lambda i: (i, 0)),
        out_shape=jax.ShapeDtypeStruct((r, c), BF16),
        compiler_params=_cparams("parallel"),
        name="cast_bf16",
    )(x)


def _transpose_cast(x, rb, cb):
    r, c = x.shape
    return pl.pallas_call(
        _transpose_cast_kernel,
        grid=(r // rb, c // cb),
        in_specs=[pl.BlockSpec((rb, cb), lambda i, j: (i, j))],
        out_specs=pl.BlockSpec((cb, rb), lambda i, j: (j, i)),
        out_shape=jax.ShapeDtypeStruct((c, r), BF16),
        compiler_params=_cparams("parallel", "parallel"),
        name="transpose_cast",
    )(x)


def _rms(x, g):
    return x * lax.rsqrt(jnp.mean(x * x, axis=-1, keepdims=True) + EPS) * g


def _prenorm_kernel(xp_ref, xt_ref, g_ref, o_ref, *, nprompt):
    i = pl.program_id(0)

    @pl.when(i < nprompt)
    def _():
        o_ref[...] = _rms(xp_ref[...], g_ref[...]).astype(BF16)

    @pl.when(i >= nprompt)
    def _():
        o_ref[...] = _rms(xt_ref[...], g_ref[...]).astype(BF16)


def _prenorm(x_p, x_t, g):
    tp, d = x_p.shape
    tr = x_t.shape[0]
    nprompt = tp // tr
    return pl.pallas_call(
        functools.partial(_prenorm_kernel, nprompt=nprompt),
        grid=(nprompt + 1,),
        in_specs=[
            pl.BlockSpec((tr, d), lambda i: (jnp.minimum(i, nprompt - 1), 0)),
            pl.BlockSpec((tr, d), lambda i: (0, 0)),
            pl.BlockSpec((1, d), lambda i: (0, 0)),
        ],
        out_specs=pl.BlockSpec((tr, d), lambda i: (i, 0)),
        out_shape=jax.ShapeDtypeStruct((tp + tr, d), BF16),
        compiler_params=_cparams("arbitrary"),
        name="mix_norm",
    )(x_p, x_t, g)


def _in_proj_kernel(xn_ref, w_ref, wg_ref, o_ref, og_ref):
    @pl.when(pl.program_id(1) == 0)
    def _():
        og_ref[...] = jnp.dot(xn_ref[...], wg_ref[...], preferred_element_type=F32)

    o_ref[...] = jnp.dot(xn_ref[...], w_ref[...], preferred_element_type=F32)


def _in_proj(xn, w, wg, tm, tn, n):
    t, d = xn.shape
    return pl.pallas_call(
        _in_proj_kernel,
        grid=(t // tm, n // tn),
        in_specs=[
            pl.BlockSpec((tm, d), lambda i, j: (i, 0)),
            pl.BlockSpec((d, tn), lambda i, j: (0, j)),
            pl.BlockSpec((d, LANES), lambda i, j: (0, 0)),
        ],
        out_specs=[
            pl.BlockSpec((tm, tn), lambda i, j: (i, j)),
            pl.BlockSpec((tm, LANES), lambda i, j: (i, 0)),
        ],
        out_shape=[jax.ShapeDtypeStruct((t, n), F32), jax.ShapeDtypeStruct((t, LANES), F32)],
        compiler_params=_cparams("parallel", "arbitrary"),
        name="in_proj",
    )(xn, w, wg)


def _s5_kernel(u_ref, h0r_ref, h0i_ref, lr_ref, li_ref, ldt_ref, bre_ref, bim_ref,
               cre_ref, cim_ref, d_ref,
               y_ref, pr_ref, pi_ref, sr_ref, si_ref,
               x_scr, init_scr, up_scr, yp_scr, *, nb, seq, ns, nm):
    sw = lr_ref.shape[1]
    tp = nb * seq
    nseg = SUBLANES
    ls = seq // nseg

    lr = lr_ref[...]
    li = li_ref[...]
    dt = jnp.exp(ldt_ref[...])
    mag = jnp.exp(lr * dt)
    ang = li * dt
    ar = mag * jnp.cos(ang)
    ai = mag * jnp.sin(ang)
    den = lr * lr + li * li
    nr = ar - 1.0
    qr = (nr * lr + ai * li) / den
    qi = (ai * lr - nr * li) / den
    bre = bre_ref[...]
    bim = bim_ref[...]
    bcat = jnp.concatenate([qr * bre - qi * bim, qr * bim + qi * bre], axis=1).astype(BF16)
    ccat = jnp.concatenate([cre_ref[...], -cim_ref[...]], axis=0).astype(BF16)
    dvec = d_ref[...]
    mag_s = jnp.exp(lr * dt * float(ls))
    asr = mag_s * jnp.cos(ang * float(ls))
    asi = mag_s * jnp.sin(ang * float(ls))

    def project_out(h, u):
        return jnp.dot(h.astype(BF16), ccat, preferred_element_type=F32) + dvec * u

    u_s = u_ref[tp:tp + ns, :]
    x_s = jnp.dot(u_s.astype(BF16), bcat, preferred_element_type=F32)
    h0r = h0r_ref[...]
    h0i = h0i_ref[...]
    hr_s = ar * h0r - ai * h0i + x_s[:, :sw]
    hi_s = ar * h0i + ai * h0r + x_s[:, sw:]
    sr_ref[...] = hr_s
    si_ref[...] = hi_s
    y_ref[tp:tp + ns, :] = project_out(jnp.concatenate([hr_s, hi_s], axis=1), u_s)
    y_ref[tp + ns:, :] = jnp.zeros((y_ref.shape[0] - tp - ns, y_ref.shape[1]), F32)

    u_m = u_ref[tp + ns:tp + ns + nm, :]
    x_m = jnp.dot(u_m.astype(BF16), bcat, preferred_element_type=F32)
    mr = jnp.zeros((1, sw), F32)
    mi = jnp.zeros((1, sw), F32)
    for t in range(nm):
        mr, mi = (ar * mr - ai * mi + x_m[t:t + 1, :sw],
                  ar * mi + ai * mr + x_m[t:t + 1, sw:])

    ar8 = jnp.broadcast_to(ar, (nseg, sw))
    ai8 = jnp.broadcast_to(ai, (nseg, sw))
    unroll = 4 if ls % 4 == 0 else 1

    def scan_pass(store):
        def body(i, c):
            hr, hi = c
            r = pl.multiple_of(i * nseg, nseg)
            nhr = ar8 * hr - ai8 * hi + x_scr[pl.ds(r, nseg), 0:sw]
            nhi = ar8 * hi + ai8 * hr + x_scr[pl.ds(r, nseg), sw:2 * sw]
            if store:
                x_scr[pl.ds(r, nseg), 0:sw] = nhr
                x_scr[pl.ds(r, nseg), sw:2 * sw] = nhi
            return nhr, nhi
        return body

    for b in range(nb):
        def regroup_in(i, c, b=b):
            r = pl.multiple_of(i * nseg, nseg)
            up_scr[pl.ds(r, nseg), :] = u_ref[pl.ds(b * seq + i, nseg, stride=ls), :]
            return c
        lax.fori_loop(0, ls, regroup_in, 0, unroll=unroll)
        u_b = up_scr[...]
        x_scr[...] = jnp.dot(u_b.astype(BF16), bcat, preferred_element_type=F32)
        z = jnp.zeros((nseg, sw), F32)
        fr, fi = lax.fori_loop(0, ls, scan_pass(False), (z, z), unroll=unroll)
        cr, ci = mr, mi
        for k in range(nseg):
            init_scr[k:k + 1, 0:sw] = cr
            init_scr[k:k + 1, sw:2 * sw] = ci
            cr, ci = (asr * cr - asi * ci + fr[k:k + 1, :],
                      asr * ci + asi * cr + fi[k:k + 1, :])
        pr_ref[b:b + 1, :] = cr
        pi_ref[b:b + 1, :] = ci
        lax.fori_loop(0, ls, scan_pass(True), (init_scr[:, 0:sw], init_scr[:, sw:2 * sw]),
                      unroll=unroll)
        yp_scr[...] = project_out(x_scr[...], u_b)

        def regroup_out(i, c, b=b):
            r = pl.multiple_of(i * nseg, nseg)
            y_ref[pl.ds(b * seq + i, nseg, stride=ls), :] = yp_scr[pl.ds(r, nseg), :]
            return c
        lax.fori_loop(0, ls, regroup_out, 0, unroll=unroll)


def _s5(proj, h0r, h0i, lam_re, lam_im, logdt, bre_bd, bim_bd, cre_bd, cim_bd, dvec,
        nb, seq, ns, nm):
    t_all = proj.shape[0]
    nslab, cw, sw = bre_bd.shape
    d_ssm = nslab * cw
    kern = functools.partial(_s5_kernel, nb=nb, seq=seq, ns=ns, nm=nm)
    return pl.pallas_call(
        kern,
        grid=(nslab,),
        in_specs=[
            pl.BlockSpec((t_all, cw), lambda s: (0, s)),
            pl.BlockSpec((ns, sw), lambda s: (0, s)),
            pl.BlockSpec((ns, sw), lambda s: (0, s)),
            pl.BlockSpec((1, sw), lambda s: (0, s)),
            pl.BlockSpec((1, sw), lambda s: (0, s)),
            pl.BlockSpec((1, sw), lambda s: (0, s)),
            pl.BlockSpec((None, cw, sw), lambda s: (s, 0, 0)),
            pl.BlockSpec((None, cw, sw), lambda s: (s, 0, 0)),
            pl.BlockSpec((None, sw, cw), lambda s: (s, 0, 0)),
            pl.BlockSpec((None, sw, cw), lambda s: (s, 0, 0)),
            pl.BlockSpec((1, cw), lambda s: (0, s)),
        ],
        out_specs=[
            pl.BlockSpec((t_all, cw), lambda s: (0, s)),
            pl.BlockSpec((nb, sw), lambda s: (0, s)),
            pl.BlockSpec((nb, sw), lambda s: (0, s)),
            pl.BlockSpec((ns, sw), lambda s: (0, s)),
            pl.BlockSpec((ns, sw), lambda s: (0, s)),
        ],
        out_shape=[
            jax.ShapeDtypeStruct((t_all, d_ssm), F32),
            jax.ShapeDtypeStruct((nb, nslab * sw), F32),
            jax.ShapeDtypeStruct((nb, nslab * sw), F32),
            jax.ShapeDtypeStruct((ns, nslab * sw), F32),
            jax.ShapeDtypeStruct((ns, nslab * sw), F32),
        ],
        scratch_shapes=[pltpu.VMEM((seq, 2 * sw), F32), pltpu.VMEM((SUBLANES, 2 * sw), F32),
                        pltpu.VMEM((seq, cw), F32), pltpu.VMEM((seq, cw), F32)],
        compiler_params=_cparams("parallel"),
        name="s5_scan",
    )(proj, h0r, h0i, lam_re, lam_im, logdt, bre_bd, bim_bd, cre_bd, cim_bd, dvec)


def _glu_norm_kernel(y_ref, w_ref, b_ref, g_ref, o_ref):
    z = _gelu(y_ref[...])
    gate = jnp.dot(z.astype(BF16), w_ref[...], preferred_element_type=F32) + b_ref[...]
    zz = z * _sigmoid(gate)
    s = lax.rsqrt(jnp.mean(zz * zz, axis=-1, keepdims=True) + EPS)
    o_ref[...] = (zz * s * g_ref[...]).astype(o_ref.dtype)


def _glu_norm(y, w, b, g, tm):
    t, d = y.shape
    return pl.pallas_call(
        _glu_norm_kernel,
        grid=(t // tm,),
        in_specs=[
            pl.BlockSpec((tm, d), lambda i: (i, 0)),
            pl.BlockSpec((d, d), lambda i: (0, 0)),
            pl.BlockSpec((1, d), lambda i: (0, 0)),
            pl.BlockSpec((1, d), lambda i: (0, 0)),
        ],
        out_specs=pl.BlockSpec((tm, d), lambda i: (i, 0)),
        out_shape=jax.ShapeDtypeStruct((t, d), BF16),
        compiler_params=_cparams("parallel"),
        name="s5_glu_norm",
    )(y, w, b, g)


def _cumsum_rows(x):
    n = x.shape[0]
    row = lax.broadcasted_iota(jnp.int32, x.shape, 0)
    s = 1
    while s < n:
        x = x + jnp.where(row >= s, pltpu.roll(x, s, axis=0), 0.0)
        s *= 2
    return x


def _gate(xg, w2, b2):
    pre = jnp.dot(xg.astype(BF16), w2, preferred_element_type=F32) + b2
    return _log_sigmoid(pre) * (1.0 / GLA_TAU)


def _head_out(o, r, g):
    o = o * lax.rsqrt(jnp.mean(o * o, axis=-1, keepdims=True) + EPS)
    return o * g * (r * _sigmoid(r))


def _gla_prompt_kernel(q_ref, k_ref, v_ref, r_ref, xg_ref, kt_ref, vt_ref, xgt_ref,
                       w2_ref, b2_ref, g_ref, o_ref, s_ref, st_scr, *, seq, ns, nm):
    dk = q_ref.shape[1]
    scale = float(dk) ** -0.5
    w2 = w2_ref[...]
    b2 = b2_ref[...]
    g = g_ref[...]

    lgm = _gate(xgt_ref[ns:ns + nm, :], w2, b2)
    bm = _cumsum_rows(lgm)
    kdm = kt_ref[ns:ns + nm, :] * jnp.exp(bm[nm - 1:nm, :] - bm)
    st_scr[...] = lax.dot_general(vt_ref[ns:ns + nm, :].astype(BF16), kdm.astype(BF16),
                                  (((0,), (0,)), ((), ())), preferred_element_type=F32)

    c = GLA_CHUNK
    nsub = c // GLA_SUB

    def chunk(ci, carry):
        r0 = pl.multiple_of(ci * c, c)
        q = q_ref[pl.ds(r0, c), :] * scale
        k = k_ref[pl.ds(r0, c), :]
        v = v_ref[pl.ds(r0, c), :]
        vb = v.astype(BF16)
        lg = _gate(xg_ref[pl.ds(r0, c), :], w2, b2)
        b = _cumsum_rows(lg)
        bl = b[c - 1:c, :]
        st = st_scr[...]
        o_inter = lax.dot_general((q * jnp.exp(b)).astype(BF16), st.astype(BF16),
                                  (((1,), (1,)), ((), ())), preferred_element_type=F32)
        outs = []
        for sb in range(nsub):
            lo = sb * GLA_SUB
            hi = lo + GLA_SUB
            beta = b[lo - 1:lo, :] if sb > 0 else jnp.zeros((1, dk), F32)
            qs = q[lo:hi, :] * jnp.exp(b[lo:hi, :] - beta)
            ks = k[0:hi, :] * jnp.exp(beta - b[0:hi, :])
            sc = lax.dot_general(qs.astype(BF16), ks.astype(BF16),
                                 (((1,), (1,)), ((), ())), preferred_element_type=F32)
            rowi = lax.broadcasted_iota(jnp.int32, (GLA_SUB, hi), 0)
            coli = lax.broadcasted_iota(jnp.int32, (GLA_SUB, hi), 1)
            sc = jnp.where(coli <= rowi + lo, sc, 0.0)
            outs.append(jnp.dot(sc.astype(BF16), vb[0:hi, :], preferred_element_type=F32))
        o = o_inter + jnp.concatenate(outs, axis=0)
        o_ref[pl.ds(r0, c), :] = _head_out(o, r_ref[pl.ds(r0, c), :], g).astype(o_ref.dtype)
        kd = k * jnp.exp(bl - b)
        st_scr[...] = jnp.exp(bl) * st + lax.dot_general(
            vb, kd.astype(BF16), (((0,), (0,)), ((), ())), preferred_element_type=F32)
        return carry

    lax.fori_loop(0, seq // c, chunk, 0, unroll=min(8, seq // c))
    s_ref[...] = st_scr[...].T


def _gla_prompt(proj, xg, w2, b2, g, nb, seq, ns, nm, nh, dk, dv, q0, k0, v0, r0):
    tp = nb * seq
    tail_blk = tp // TAIL_ROWS
    rb = seq
    kern = functools.partial(_gla_prompt_kernel, seq=seq, ns=ns, nm=nm)
    return pl.pallas_call(
        kern,
        grid=(nb, nh),
        in_specs=[
            pl.BlockSpec((rb, dk), lambda b, h: (b, q0 // dk + h)),
            pl.BlockSpec((rb, dk), lambda b, h: (b, k0 // dk + h)),
            pl.BlockSpec((rb, dv), lambda b, h: (b, v0 // dv + h)),
            pl.BlockSpec((rb, dv), lambda b, h: (b, r0 // dv + h)),
            pl.BlockSpec((rb, LANES), lambda b, h: (b, 0)),
            pl.BlockSpec((TAIL_ROWS, dk), lambda b, h: (tail_blk, k0 // dk + h)),
            pl.BlockSpec((TAIL_ROWS, dv), lambda b, h: (tail_blk, v0 // dv + h)),
            pl.BlockSpec((TAIL_ROWS, LANES), lambda b, h: (tail_blk, 0)),
            pl.BlockSpec((LANES, dk), lambda b, h: (0, h)),
            pl.BlockSpec((1, dk), lambda b, h: (0, h)),
            pl.BlockSpec((1, dv), lambda b, h: (0, h)),
        ],
        out_specs=[
            pl.BlockSpec((rb, dv), lambda b, h: (b, h)),
            pl.BlockSpec((None, None, dk, dv), lambda b, h: (b, h, 0, 0)),
        ],
        out_shape=[
            jax.ShapeDtypeStruct((tp, nh * dv), BF16),
            jax.ShapeDtypeStruct((nb, nh, dk, dv), F32),
        ],
        scratch_shapes=[pltpu.VMEM((dv, dk), F32)],
        compiler_params=_cparams("parallel", "parallel"),
        name="gla_prompt",
    )(proj, proj, proj, proj, xg, proj, proj, xg, w2, b2, g)


def _gla_sample_kernel(q_ref, k_ref, v_ref, r_ref, xg_ref, w2_ref, b2_ref, g_ref, s0_ref,
                       o_ref, s_ref, *, nh, dk, dv):
    n = pl.program_id(0)
    j = n % SUBLANES
    scale = float(dk) ** -0.5
    lg_all = _gate(xg_ref[pl.ds(j, 1), :], w2_ref[...], b2_ref[...])
    q_all = q_ref[pl.ds(j, 1), :] * scale
    k_all = k_ref[pl.ds(j, 1), :]
    v_all = v_ref[pl.ds(j, 1), :]
    r_all = r_ref[pl.ds(j, 1), :]
    g_all = g_ref[...]
    rows = 2 * SUBLANES
    rk = lax.broadcasted_iota(jnp.int32, (rows, dk), 0)
    rv = lax.broadcasted_iota(jnp.int32, (rows, dv), 0)
    outs = []
    for h in range(nh):
        lg = lg_all[:, h * dk:(h + 1) * dk]
        q = q_all[:, h * dk:(h + 1) * dk]
        k = k_all[:, h * dk:(h + 1) * dk]
        v = v_all[:, h * dv:(h + 1) * dv]
        e = jnp.exp(lg)
        s0 = s0_ref[h]
        e_hi = e.astype(BF16)
        e_mid = (e - e_hi.astype(F32)).astype(BF16)
        e_lo = (e - e_hi.astype(F32) - e_mid.astype(F32)).astype(BF16)
        def rows_of(x, w):
            return jnp.broadcast_to(x.astype(F32), (rows, w))
        lhs = jnp.where(rk == 0, rows_of(e_hi, dk),
              jnp.where(rk == 1, rows_of(e_mid, dk),
              jnp.where(rk == 2, rows_of(e_lo, dk),
              jnp.where(rk == 3, rows_of(k, dk), 0.0)))).astype(BF16)
        ones_part = jnp.where(rv < 3, 1.0, 0.0)
        v_part = jnp.where(rv == 3, rows_of(v, dv), 0.0)
        rhs = jnp.concatenate([ones_part, v_part], axis=1).astype(BF16)
        both = lax.dot_general(lhs, rhs, (((0,), (0,)), ((), ())), preferred_element_type=F32)
        s_ref[h] = both[:, :dv] * s0 + both[:, dv:]
        qe = jnp.broadcast_to((q * e).astype(BF16), (rows, dk))
        o = jnp.dot(qe, s0.astype(BF16), preferred_element_type=F32)[0:1, :]
        o = o + jnp.sum(q * k, axis=-1, keepdims=True) * v
        outs.append(_head_out(o, r_all[:, h * dv:(h + 1) * dv], g_all[:, h * dv:(h + 1) * dv]))
    o_ref[pl.ds(j, 1), :] = jnp.concatenate(outs, axis=1)


def _gla_sample(proj, xg, w2, b2, g, s0, tp, ns, nh, dk, dv, q0, k0, v0, r0):
    rb = SUBLANES
    base = tp // rb
    qk_w = nh * dk
    v_w = nh * dv
    kern = functools.partial(_gla_sample_kernel, nh=nh, dk=dk, dv=dv)
    return pl.pallas_call(
        kern,
        grid=(ns,),
        in_specs=[
            pl.BlockSpec((rb, qk_w), lambda n: (base + n // rb, q0 // qk_w)),
            pl.BlockSpec((rb, qk_w), lambda n: (base + n // rb, k0 // qk_w)),
            pl.BlockSpec((rb, v_w), lambda n: (base + n // rb, v0 // v_w)),
            pl.BlockSpec((rb, v_w), lambda n: (base + n // rb, r0 // v_w)),
            pl.BlockSpec((rb, LANES), lambda n: (base + n // rb, 0)),
            pl.BlockSpec((LANES, qk_w), lambda n: (0, 0)),
            pl.BlockSpec((1, qk_w), lambda n: (0, 0)),
            pl.BlockSpec((1, v_w), lambda n: (0, 0)),
            pl.BlockSpec((None, nh, dk, dv), lambda n: (n, 0, 0, 0)),
        ],
        out_specs=[
            pl.BlockSpec((rb, v_w), lambda n: (n // rb, 0)),
            pl.BlockSpec((None, nh, dk, dv), lambda n: (n, 0, 0, 0)),
        ],
        out_shape=[
            jax.ShapeDtypeStruct((ns, v_w), F32),
            jax.ShapeDtypeStruct((ns, nh, dk, dv), F32),
        ],
        compiler_params=_cparams("arbitrary"),
        name="gla_sample",
    )(proj, proj, proj, proj, xg, w2, b2, g, s0)


def _out_proj_kernel(ys_ref, ogp_ref, ogt_ref, w1_ref, w2_ref, xp_ref, xt_ref, h_ref,
                     *, nfull, split):
    i = pl.program_id(0)
    w2 = w2_ref[...]
    acc = jnp.dot(ys_ref[...], w1_ref[...], preferred_element_type=F32)

    @pl.when(i < nfull)
    def _():
        h_ref[...] = xp_ref[...] + acc + jnp.dot(ogp_ref[...], w2, preferred_element_type=F32)

    @pl.when(i >= nfull)
    def _():
        if split:
            h_ref[0:split, :] = xp_ref[0:split, :] + acc[0:split] + jnp.dot(
                ogp_ref[0:split, :], w2, preferred_element_type=F32)
        h_ref[split:, :] = xt_ref[...] + acc[split:] + jnp.dot(
            ogt_ref[...], w2, preferred_element_type=F32)


def _out_proj(ys, og_p, og_t, w, x_p, x_t, tm, tn):
    t, dh = ys.shape
    d = w.shape[1]
    tp, tr = x_p.shape[0], x_t.shape[0]
    nfull = tp // tm
    split = tp - nfull * tm
    assert t == tp + tr and t == (nfull + 1) * tm and tm - split == tr
    last = pl.cdiv(tp, tm) - 1
    return pl.pallas_call(
        functools.partial(_out_proj_kernel, nfull=nfull, split=split),
        grid=(t // tm, d // tn),
        in_specs=[
            pl.BlockSpec((tm, dh), lambda i, j: (i, 0)),
            pl.BlockSpec((tm, dh), lambda i, j: (jnp.minimum(i, last), 0)),
            pl.BlockSpec((tr, dh), lambda i, j: (0, 0)),
            pl.BlockSpec((dh, tn), lambda i, j: (0, j)),
            pl.BlockSpec((dh, tn), lambda i, j: (1, j)),
            pl.BlockSpec((tm, tn), lambda i, j: (jnp.minimum(i, last), j)),
            pl.BlockSpec((tr, tn), lambda i, j: (0, j)),
        ],
        out_specs=pl.BlockSpec((tm, tn), lambda i, j: (i, j)),
        out_shape=jax.ShapeDtypeStruct((t, d), F32),
        compiler_params=_cparams("parallel", "parallel"),
        name="out_proj",
    )(ys, og_p, og_t, w, w, x_p, x_t)


def _norm_t_kernel(h_ref, g_ref, o_ref):
    o_ref[...] = _rms(h_ref[...], g_ref[...]).T.astype(BF16)


def _norm_t(h, g, tm):
    t, d = h.shape
    return pl.pallas_call(
        _norm_t_kernel,
        grid=(t // tm,),
        in_specs=[pl.BlockSpec((tm, d), lambda i: (i, 0)), pl.BlockSpec((1, d), lambda i: (0, 0))],
        out_specs=pl.BlockSpec((d, tm), lambda i: (0, i)),
        out_shape=jax.ShapeDtypeStruct((d, t), BF16),
        compiler_params=_cparams("parallel"),
        name="ffn_norm_t",
    )(h, g)


def _topk_rows(x, k):
    outs = []
    for _ in range(k):
        m = jnp.max(x, axis=0, keepdims=True)
        outs.append(m)
        x = jnp.where(x == m, -jnp.inf, x)
    return outs


def _route_kernel(wq_ref, hn_ref, k1_ref, k2_ref, s1_ref, c1_ref, s2_ref, e2_ref, tau_ref):
    half = k1_ref.shape[1]
    tb = hn_ref.shape[1]
    qt = jnp.dot(wq_ref[...], hn_ref[...], preferred_element_type=F32)
    s1 = jnp.dot(k1_ref[...].astype(BF16), qt[:half, :].astype(BF16), preferred_element_type=F32)
    s2 = jnp.dot(k2_ref[...].astype(BF16), qt[half:, :].astype(BF16), preferred_element_type=F32)
    s1_ref[...] = s1
    s2_ref[...] = s2
    for t in range(tb // LANES):
        sl = slice(t * LANES, (t + 1) * LANES)
        a = s1[:, sl]
        b = s2[:, sl]
        kk = PEER_TOPK
        v1 = _topk_rows(a, kk)
        v2l = _topk_rows(b, kk)
        v2 = jnp.concatenate(v2l, axis=0)
        cand = jnp.concatenate(
            [v1[0] + v2]
            + [v1[i] + v2[:kk // 2] for i in range(1, kk // 2)]
            + [jnp.concatenate(v1[kk // 2:], axis=0) + v2l[0]], axis=0)
        top = _topk_rows(cand, kk)
        z = jnp.zeros_like(top[0])
        for c in top:
            z = z + jnp.exp(c - top[0])
        tau_ref[:, sl] = top[PEER_TOPK - 1]
        c1_ref[:, sl] = jnp.exp(a - v1[0]) / z
        e2_ref[:, sl] = jnp.exp(b - v2[0:1, :])


def _route(wq_t, hn_t, k1, k2, tb):
    d, t = hn_t.shape
    nh, nk, half = k1.shape
    qd = 2 * half
    tab = jax.ShapeDtypeStruct((nh, nk, t), F32)
    tab_spec = pl.BlockSpec((None, nk, tb), lambda i, h: (h, 0, i))
    return pl.pallas_call(
        _route_kernel,
        grid=(t // tb, nh),
        in_specs=[
            pl.BlockSpec((qd, d), lambda i, h: (h, 0)),
            pl.BlockSpec((d, tb), lambda i, h: (0, i)),
            pl.BlockSpec((None, nk, half), lambda i, h: (h, 0, 0)),
            pl.BlockSpec((None, nk, half), lambda i, h: (h, 0, 0)),
        ],
        out_specs=[tab_spec, tab_spec, tab_spec, tab_spec,
                   pl.BlockSpec((None, 1, tb), lambda i, h: (h, 0, i))],
        out_shape=[tab, tab, tab, tab, jax.ShapeDtypeStruct((nh, 1, t), F32)],
        compiler_params=_cparams("parallel", "parallel"),
        name="peer_route",
    )(wq_t, hn_t, k1, k2)


GATE_ROWS = 32
OUT_ROWS = 512


K_CHUNK = 512
EXPERT_BLOCK = 512


def _peer_kernel(u_ref, vt_ref, hn_ref, s1_ref, c1_ref, s2_ref, e2_ref, tau_ref, o_ref,
                 w_scr, act_scr, a_scr):
    eb, d = u_ref.shape
    tb = hn_ref.shape[1]
    nh, nk, _ = s2_ref.shape
    nr = eb // nk

    @pl.when(pl.program_id(1) == 0)
    def _():
        o_ref[...] = jnp.zeros_like(o_ref)

    def gate_group(t, q):
        sl = slice(t * LANES, (t + 1) * LANES)
        rows = slice(q * GATE_ROWS, (q + 1) * GATE_ROWS)
        accs = [None] * nr
        for h in range(nh):
            s2t = s2_ref[h, rows, sl]
            e2t = e2_ref[h, rows, sl]
            tau = tau_ref[h, :, sl]
            for r in range(nr):
                ssum = s1_ref[r, h:h + 1, sl] + s2t
                term = jnp.where(ssum >= tau, c1_ref[r, h:h + 1, sl] * e2t, 0.0)
                accs[r] = term if accs[r] is None else accs[r] + term
        for r in range(nr):
            lo = r * nk + q * GATE_ROWS
            w_scr[lo:lo + GATE_ROWS, sl] = accs[r]

    groups = [(t, q) for t in range(tb // LANES) for q in range(nk // GATE_ROWS)]
    nkc = d // K_CHUNK
    per = -(-len(groups) // nkc)
    for kc in range(nkc):
        @pl.when(pl.program_id(1) >= 0)
        def _(kc=kc):
            ks = slice(kc * K_CHUNK, (kc + 1) * K_CHUNK)
            part = jnp.dot(u_ref[:, ks], hn_ref[ks, :], preferred_element_type=F32)
            if kc == 0:
                act_scr[...] = part
            else:
                act_scr[...] += part
            for t, q in groups[kc * per:(kc + 1) * per]:
                gate_group(t, q)
    a_scr[...] = (w_scr[...] * _gelu(act_scr[...])).astype(BF16)
    for dc in range(d // OUT_ROWS):
        dr = slice(dc * OUT_ROWS, (dc + 1) * OUT_ROWS)
        o_ref[dr, :] += jnp.dot(vt_ref[dr, :], a_scr[...], preferred_element_type=F32)


def _peer(u_bf, v_t, hn_t, s1, c1, s2, e2, tau, tb, eb):
    ne, d = u_bf.shape
    t = hn_t.shape[1]
    nh, nk, _ = s2.shape
    once = pl.Buffered(1)
    tab_spec = pl.BlockSpec((nh, nk, tb), lambda i, e: (0, 0, i), pipeline_mode=once)
    row_spec = pl.BlockSpec((eb // nk, nh, tb), lambda i, e: (e, 0, i))
    return pl.pallas_call(
        _peer_kernel,
        grid=(t // tb, ne // eb),
        in_specs=[
            pl.BlockSpec((eb, d), lambda i, e: (e, 0)),
            pl.BlockSpec((d, eb), lambda i, e: (0, e)),
            pl.BlockSpec((d, tb), lambda i, e: (0, i), pipeline_mode=once),
            row_spec, row_spec, tab_spec, tab_spec,
            pl.BlockSpec((nh, 1, tb), lambda i, e: (0, 0, i)),
        ],
        out_specs=pl.BlockSpec((d, tb), lambda i, e: (0, i)),
        out_shape=jax.ShapeDtypeStruct((d, t), F32),
        scratch_shapes=[pltpu.VMEM((eb, tb), F32), pltpu.VMEM((eb, tb), F32),
                        pltpu.VMEM((eb, tb), BF16)],
        compiler_params=pltpu.CompilerParams(
            dimension_semantics=("parallel", "arbitrary"), vmem_limit_bytes=PEER_VMEM_LIMIT),
        name="peer_experts",
    )(u_bf, v_t, hn_t, s1, c1, s2, e2, tau)


def _final_kernel(pt_ref, h_ref, g_ref, yp_ref, yt_ref, *, nprompt):
    i = pl.program_id(0)
    y = _rms(h_ref[...] + pt_ref[...].T, g_ref[...])

    @pl.when(i < nprompt)
    def _():
        yp_ref[...] = y

    @pl.when(i >= nprompt)
    def _():
        yt_ref[...] = y


def _final(peer_t, h, g, tp, tr):
    d, t = peer_t.shape
    nprompt = tp // tr
    return pl.pallas_call(
        functools.partial(_final_kernel, nprompt=nprompt),
        grid=(nprompt + 1,),
        in_specs=[
            pl.BlockSpec((d, tr), lambda i: (0, i)),
            pl.BlockSpec((tr, d), lambda i: (i, 0)),
            pl.BlockSpec((1, d), lambda i: (0, 0)),
        ],
        out_specs=[
            pl.BlockSpec((tr, d), lambda i: (jnp.minimum(i, nprompt - 1), 0)),
            pl.BlockSpec((tr, d), lambda i: (0, 0)),
        ],
        out_shape=[jax.ShapeDtypeStruct((tp, d), F32), jax.ShapeDtypeStruct((tr, d), F32)],
        compiler_params=_cparams("arbitrary"),
        name="final_norm",
    )(peer_t, h, g)


def _block_diag(w, ngrp):
    nslab, _, a, b = w.shape
    eye = jnp.eye(ngrp, dtype=w.dtype)
    full = w[:, :, :, None, :] * eye[None, :, None, :, None]
    return full.reshape(nslab, ngrp * a, ngrp * b)


def kernel(x_prompt, x_sample, state_s5_re, state_s5_im, state_gla, meta_tokens, norm_mix_g, w_in, s5_lam_re, s5_lam_im, s5_log_dt, s5_b_re, s5_b_im, s5_c_re, s5_c_im, s5_d, s5_w_glu, s5_b_glu, s5_norm_g, gla_w_gate2, gla_b_gate2, gla_norm_g, w_out, norm_ffn_g, peer_w_q, peer_keys, peer_u, peer_v, norm_final_g):
    nb, seq, d = x_prompt.shape
    ns = x_sample.shape[0]
    nm = meta_tokens.shape[0]
    depth = w_in.shape[0]
    assert depth == 1 and x_sample.shape[1] == 1
    tp = nb * seq
    assert tp % TAIL_ROWS == 0 and ns + nm <= TAIL_ROWS and ns % SUBLANES == 0
    assert seq % GLA_CHUNK == 0 and nm % SUBLANES == 0
    t_all = tp + TAIL_ROWS

    ngrp, nstate, gch = s5_b_re.shape[1:]
    d_ssm = ngrp * gch
    gps = LANES // gch
    nslab = ngrp // gps
    nh, dk, dv = state_gla.shape[2:]
    d_gla = nh * dv
    rank = gla_w_gate2.shape[1]
    q0 = d_ssm
    k0 = q0 + nh * dk
    v0 = k0 + nh * dk
    r0 = v0 + d_gla
    g0 = r0 + d_gla
    assert g0 + rank == w_in.shape[2] and d_ssm + d_gla == w_out.shape[1]

    x_p = x_prompt.reshape(tp, d)
    x_t = jnp.concatenate([x_sample.reshape(ns, d), meta_tokens,
                           jnp.zeros((TAIL_ROWS - ns - nm, d), F32)], axis=0)
    w_in_bf = _cast_bf16(w_in[0], TAIL_ROWS)
    w_g1 = jnp.pad(w_in[0, :, g0:], ((0, 0), (0, LANES - rank))).astype(BF16)
    w_g2 = jnp.pad(gla_w_gate2[0], ((0, LANES - rank), (0, 0))).astype(BF16)
    lam_re = s5_lam_re[0].reshape(1, ngrp * nstate)
    lam_im = s5_lam_im[0].reshape(1, ngrp * nstate)
    logdt = jnp.repeat(s5_log_dt[0], nstate).reshape(1, ngrp * nstate)
    b_re4 = jnp.transpose(s5_b_re[0].reshape(nslab, gps, nstate, gch), (0, 1, 3, 2))
    b_im4 = jnp.transpose(s5_b_im[0].reshape(nslab, gps, nstate, gch), (0, 1, 3, 2))
    c_re4 = jnp.transpose(s5_c_re[0].reshape(nslab, gps, gch, nstate), (0, 1, 3, 2))
    c_im4 = jnp.transpose(s5_c_im[0].reshape(nslab, gps, gch, nstate), (0, 1, 3, 2))
    bre_bd = _block_diag(b_re4, gps)
    bim_bd = _block_diag(b_im4, gps)
    cre_bd = _block_diag(c_re4, gps)
    cim_bd = _block_diag(c_im4, gps)
    h0r = state_s5_re[0].reshape(ns, ngrp * nstate)
    h0i = state_s5_im[0].reshape(ns, ngrp * nstate)

    tm = _row_tile(t_all, (768, 512, 256))

    xn = _prenorm(x_p, x_t, norm_mix_g)
    proj, xg = _in_proj(xn, w_in_bf, w_g1, tm, PROJ_COLS, g0)

    y_raw, pr, pi_, sr, si = _s5(proj, h0r, h0i, lam_re, lam_im, logdt, bre_bd, bim_bd,
                                 cre_bd, cim_bd, s5_d, nb, seq, ns, nm)
    y_ssm = _glu_norm(y_raw, s5_w_glu[0].astype(BF16), s5_b_glu, s5_norm_g, TAIL_ROWS)

    og_p, gla_p = _gla_prompt(proj, xg, w_g2, gla_b_gate2, gla_norm_g, nb, seq, ns, nm,
                              nh, dk, dv, q0, k0, v0, r0)
    og_s, gla_s = _gla_sample(proj, xg, w_g2, gla_b_gate2, gla_norm_g, state_gla[0],
                              tp, ns, nh, dk, dv, q0, k0, v0, r0)
    og_t = jnp.concatenate([og_s.astype(BF16), jnp.zeros((TAIL_ROWS - ns, d_gla), BF16)], axis=0)

    w_o = w_out[0].astype(BF16)
    assert d_ssm == d_gla
    h = _out_proj(y_ssm, og_p, og_t, w_o, x_p, x_t, tm, PROJ_COLS)

    hn_t = _norm_t(h, norm_ffn_g, TAIL_ROWS)
    wq_t = _transpose_cast(peer_w_q[0], WT_ROWS, WT_COLS)
    s1, c1, s2, e2, tau = _route(wq_t, hn_t, peer_keys[0, :, 0], peer_keys[0, :, 1], tm)
    u_bf = peer_u[0].astype(BF16)
    v_t = _transpose_cast(peer_v[0], WT_ROWS, WT_COLS)
    peer_t = _peer(u_bf, v_t, hn_t, jnp.transpose(s1, (1, 0, 2)), jnp.transpose(c1, (1, 0, 2)),
                   s2, e2, tau, tm, EXPERT_BLOCK)
    y_p, y_t = _final(peer_t, h, norm_final_g.reshape(1, d), tp, TAIL_ROWS)

    y_prompt = y_p.reshape(nb, seq, d)
    y_sample = y_t[:ns].reshape(ns, 1, d)
    return (y_prompt, y_sample,
            pr.reshape(1, nb, ngrp, nstate), pi_.reshape(1, nb, ngrp, nstate), gla_p[None],
            sr.reshape(1, ns, ngrp, nstate), si.reshape(1, ns, ngrp, nstate), gla_s[None])
```

```python
import functools

import jax
import jax.numpy as jnp
from jax import lax
from jax.experimental import pallas as pl
from jax.experimental.pallas import tpu as pltpu

F32 = jnp.float32
BF16 = jnp.bfloat16

EPS = 1e-6
GLA_TAU = 16.0
GLA_CHUNK = 64
GLA_SUB = 16
PEER_TOPK = 16
LANES = 128
SUBLANES = 8
TAIL_ROWS = 256
PROJ_COLS = 1024
MXU_DEPTH = 256
WT_ROWS = 2048
WT_COLS = 1024
VMEM_LIMIT = 56 * 1024 * 1024
PEER_VMEM_LIMIT = 60 * 1024 * 1024


def _cparams(*sem):
    return pltpu.CompilerParams(dimension_semantics=sem, vmem_limit_bytes=VMEM_LIMIT)


def _gelu(x):
    return 0.5 * x * (1.0 + jnp.tanh(0.7978845608028654 * (x + 0.044715 * (x * x * x))))


def _sigmoid(x):
    return 1.0 / (1.0 + jnp.exp(-x))


def _log_sigmoid(x):
    return jnp.minimum(x, 0.0) - jnp.log(1.0 + jnp.exp(-jnp.abs(x)))


def _row_tile(n, cands):
    for c in cands:
        if n % c == 0:
            return c
    raise ValueError(f"no row tile for {n}")


def _transpose_cast_kernel(x_ref, o_ref):
    k = MXU_DEPTH
    eye = jnp.where(lax.broadcasted_iota(jnp.int32, (k, k), 0)
                    == lax.broadcasted_iota(jnp.int32, (k, k), 1), 1.0, 0.0).astype(BF16)
    for c in range(x_ref.shape[1] // k):
        xb = x_ref[:, c * k:(c + 1) * k].astype(BF16)
        o_ref[c * k:(c + 1) * k, :] = lax.dot_general(
            eye, xb, (((1,), (1,)), ((), ())), preferred_element_type=F32).astype(BF16)


def _transpose_cast(x, rb, cb):
    r, c = x.shape
    return pl.pallas_call(
        _transpose_cast_kernel,
        grid=(r // rb, c // cb),
        in_specs=[pl.BlockSpec((rb, cb), lambda i, j: (i, j))],
        out_specs=pl.BlockSpec((cb, rb), lambda i, j: (j, i)),
        out_shape=jax.ShapeDtypeStruct((c, r), BF16),
        compiler_params=_cparams("parallel", "parallel"),
        name="transpose_cast",
    )(x)


def _rms(x, g):
    return x * lax.rsqrt(jnp.mean(x * x, axis=-1, keepdims=True) + EPS) * g


def _prenorm_kernel(xp_ref, xt_ref, g_ref, o_ref, *, nprompt):
    i = pl.program_id(0)

    @pl.when(i < nprompt)
    def _():
        o_ref[...] = _rms(xp_ref[...], g_ref[...]).astype(BF16)

    @pl.when(i >= nprompt)
    def _():
        o_ref[...] = _rms(xt_ref[...], g_ref[...]).astype(BF16)


def _prenorm(x_p, x_t, g):
    tp, d = x_p.shape
    tr = x_t.shape[0]
    nprompt = tp // tr
    return pl.pallas_call(
        functools.partial(_prenorm_kernel, nprompt=nprompt),
        grid=(nprompt + 1,),
        in_specs=[
            pl.BlockSpec((tr, d), lambda i: (jnp.minimum(i, nprompt - 1), 0)),
            pl.BlockSpec((tr, d), lambda i: (0, 0)),
            pl.BlockSpec((1, d), lambda i: (0, 0)),
        ],
        out_specs=pl.BlockSpec((tr, d), lambda i: (i, 0)),
        out_shape=jax.ShapeDtypeStruct((tp + tr, d), BF16),
        compiler_params=_cparams("arbitrary"),
        name="mix_norm",
    )(x_p, x_t, g)


def _in_proj_kernel(xn_ref, w_ref, wg_ref, o_ref, og_ref):
    @pl.when(pl.program_id(1) == 0)
    def _():
        og_ref[...] = jnp.dot(xn_ref[...], wg_ref[...], preferred_element_type=F32)

    o_ref[...] = jnp.dot(xn_ref[...], w_ref[...], preferred_element_type=F32)


def _in_proj(xn, w, wg, tm, tn, n):
    t, d = xn.shape
    return pl.pallas_call(
        _in_proj_kernel,
        grid=(t // tm, n // tn),
        in_specs=[
            pl.BlockSpec((tm, d), lambda i, j: (i, 0)),
            pl.BlockSpec((d, tn), lambda i, j: (0, j)),
            pl.BlockSpec((d, LANES), lambda i, j: (0, 0)),
        ],
        out_specs=[
            pl.BlockSpec((tm, tn), lambda i, j: (i, j)),
            pl.BlockSpec((tm, LANES), lambda i, j: (i, 0)),
        ],
        out_shape=[jax.ShapeDtypeStruct((t, n), F32), jax.ShapeDtypeStruct((t, LANES), F32)],
        compiler_params=_cparams("parallel", "arbitrary"),
        name="in_proj",
    )(xn, w, wg)


def _s5_kernel(u_ref, h0r_ref, h0i_ref, lr_ref, li_ref, ldt_ref, bre_ref, bim_ref,
               cre_ref, cim_ref, d_ref,
               y_ref, pr_ref, pi_ref, sr_ref, si_ref,
               x_scr, init_scr, up_scr, yp_scr, *, nb, seq, ns, nm):
    sw = lr_ref.shape[1]
    tp = nb * seq
    nseg = SUBLANES
    ls = seq // nseg

    lr = lr_ref[...]
    li = li_ref[...]
    dt = jnp.exp(ldt_ref[...])
    mag = jnp.exp(lr * dt)
    ang = li * dt
    ar = mag * jnp.cos(ang)
    ai = mag * jnp.sin(ang)
    den = lr * lr + li * li
    nr = ar - 1.0
    qr = (nr * lr + ai * li) / den
    qi = (ai * lr - nr * li) / den
    bre = bre_ref[...]
    bim = bim_ref[...]
    bcat = jnp.concatenate([qr * bre - qi * bim, qr * bim + qi * bre], axis=1).astype(BF16)
    ccat = jnp.concatenate([cre_ref[...], -cim_ref[...]], axis=0).astype(BF16)
    dvec = d_ref[...]
    mag_s = jnp.exp(lr * dt * float(ls))
    asr = mag_s * jnp.cos(ang * float(ls))
    asi = mag_s * jnp.sin(ang * float(ls))

    def project_out(h, u):
        return jnp.dot(h.astype(BF16), ccat, preferred_element_type=F32) + dvec * u

    u_s = u_ref[tp:tp + ns, :]
    x_s = jnp.dot(u_s.astype(BF16), bcat, preferred_element_type=F32)
    h0r = h0r_ref[...]
    h0i = h0i_ref[...]
    hr_s = ar * h0r - ai * h0i + x_s[:, :sw]
    hi_s = ar * h0i + ai * h0r + x_s[:, sw:]
    sr_ref[...] = hr_s
    si_ref[...] = hi_s
    y_ref[tp:tp + ns, :] = project_out(jnp.concatenate([hr_s, hi_s], axis=1), u_s)
    y_ref[tp + ns:, :] = jnp.zeros((y_ref.shape[0] - tp - ns, y_ref.shape[1]), F32)

    u_m = u_ref[tp + ns:tp + ns + nm, :]
    x_m = jnp.dot(u_m.astype(BF16), bcat, preferred_element_type=F32)
    mr = jnp.zeros((1, sw), F32)
    mi = jnp.zeros((1, sw), F32)
    for t in range(nm):
        mr, mi = (ar * mr - ai * mi + x_m[t:t + 1, :sw],
                  ar * mi + ai * mr + x_m[t:t + 1, sw:])

    ar8 = jnp.broadcast_to(ar, (nseg, sw))
    ai8 = jnp.broadcast_to(ai, (nseg, sw))
    unroll = 4 if ls % 4 == 0 else 1

    def scan_pass(store):
        def body(i, c):
            hr, hi = c
            r = pl.multiple_of(i * nseg, nseg)
            nhr = ar8 * hr - ai8 * hi + x_scr[pl.ds(r, nseg), 0:sw]
            nhi = ar8 * hi + ai8 * hr + x_scr[pl.ds(r, nseg), sw:2 * sw]
            if store:
                x_scr[pl.ds(r, nseg), 0:sw] = nhr
                x_scr[pl.ds(r, nseg), sw:2 * sw] = nhi
            return nhr, nhi
        return body

    for b in range(nb):
        def regroup_in(i, c, b=b):
            r = pl.multiple_of(i * nseg, nseg)
            up_scr[pl.ds(r, nseg), :] = u_ref[pl.ds(b * seq + i, nseg, stride=ls), :]
            return c
        lax.fori_loop(0, ls, regroup_in, 0, unroll=unroll)
        u_b = up_scr[...]
        x_scr[...] = jnp.dot(u_b.astype(BF16), bcat, preferred_element_type=F32)
        z = jnp.zeros((nseg, sw), F32)
        fr, fi = lax.fori_loop(0, ls, scan_pass(False), (z, z), unroll=unroll)
        cr, ci = mr, mi
        for k in range(nseg):
            init_scr[k:k + 1, 0:sw] = cr
            init_scr[k:k + 1, sw:2 * sw] = ci
            cr, ci = (asr * cr - asi * ci + fr[k:k + 1, :],
                      asr * ci + asi * cr + fi[k:k + 1, :])
        pr_ref[b:b + 1, :] = cr
        pi_ref[b:b + 1, :] = ci
        lax.fori_loop(0, ls, scan_pass(True), (init_scr[:, 0:sw], init_scr[:, sw:2 * sw]),
                      unroll=unroll)
        yp_scr[...] = project_out(x_scr[...], u_b)

        def regroup_out(i, c, b=b):
            r = pl.multiple_of(i * nseg, nseg)
            y_ref[pl.ds(b * seq + i, nseg, stride=ls), :] = yp_scr[pl.ds(r, nseg), :]
            return c
        lax.fori_loop(0, ls, regroup_out, 0, unroll=unroll)


def _s5(proj, h0r, h0i, lam_re, lam_im, logdt, bre_bd, bim_bd, cre_bd, cim_bd, dvec,
        nb, seq, ns, nm):
    t_all = proj.shape[0]
    nslab, cw, sw = bre_bd.shape
    d_ssm = nslab * cw
    kern = functools.partial(_s5_kernel, nb=nb, seq=seq, ns=ns, nm=nm)
    return pl.pallas_call(
        kern,
        grid=(nslab,),
        in_specs=[
            pl.BlockSpec((t_all, cw), lambda s: (0, s)),
            pl.BlockSpec((ns, sw), lambda s: (0, s)),
            pl.BlockSpec((ns, sw), lambda s: (0, s)),
            pl.BlockSpec((1, sw), lambda s: (0, s)),
            pl.BlockSpec((1, sw), lambda s: (0, s)),
            pl.BlockSpec((1, sw), lambda s: (0, s)),
            pl.BlockSpec((None, cw, sw), lambda s: (s, 0, 0)),
            pl.BlockSpec((None, cw, sw), lambda s: (s, 0, 0)),
            pl.BlockSpec((None, sw, cw), lambda s: (s, 0, 0)),
            pl.BlockSpec((None, sw, cw), lambda s: (s, 0, 0)),
            pl.BlockSpec((1, cw), lambda s: (0, s)),
        ],
        out_specs=[
            pl.BlockSpec((t_all, cw), lambda s: (0, s)),
            pl.BlockSpec((nb, sw), lambda s: (0, s)),
            pl.BlockSpec((nb, sw), lambda s: (0, s)),
            pl.BlockSpec((ns, sw), lambda s: (0, s)),
            pl.BlockSpec((ns, sw), lambda s: (0, s)),
        ],
        out_shape=[
            jax.ShapeDtypeStruct((t_all, d_ssm), F32),
            jax.ShapeDtypeStruct((nb, nslab * sw), F32),
            jax.ShapeDtypeStruct((nb, nslab * sw), F32),
            jax.ShapeDtypeStruct((ns, nslab * sw), F32),
            jax.ShapeDtypeStruct((ns, nslab * sw), F32),
        ],
        scratch_shapes=[pltpu.VMEM((seq, 2 * sw), F32), pltpu.VMEM((SUBLANES, 2 * sw), F32),
                        pltpu.VMEM((seq, cw), F32), pltpu.VMEM((seq, cw), F32)],
        compiler_params=_cparams("parallel"),
        name="s5_scan",
    )(proj, h0r, h0i, lam_re, lam_im, logdt, bre_bd, bim_bd, cre_bd, cim_bd, dvec)


def _glu_norm_kernel(y_ref, w_ref, b_ref, g_ref, o_ref):
    z = _gelu(y_ref[...])
    gate = jnp.dot(z.astype(BF16), w_ref[...], preferred_element_type=F32) + b_ref[...]
    zz = z * _sigmoid(gate)
    s = lax.rsqrt(jnp.mean(zz * zz, axis=-1, keepdims=True) + EPS)
    o_ref[...] = (zz * s * g_ref[...]).astype(o_ref.dtype)


def _glu_norm(y, w, b, g, tm):
    t, d = y.shape
    return pl.pallas_call(
        _glu_norm_kernel,
        grid=(t // tm,),
        in_specs=[
            pl.BlockSpec((tm, d), lambda i: (i, 0)),
            pl.BlockSpec((d, d), lambda i: (0, 0)),
            pl.BlockSpec((1, d), lambda i: (0, 0)),
            pl.BlockSpec((1, d), lambda i: (0, 0)),
        ],
        out_specs=pl.BlockSpec((tm, d), lambda i: (i, 0)),
        out_shape=jax.ShapeDtypeStruct((t, d), BF16),
        compiler_params=_cparams("parallel"),
        name="s5_glu_norm",
    )(y, w, b, g)


def _cumsum_rows(x):
    n = x.shape[0]
    row = lax.broadcasted_iota(jnp.int32, x.shape, 0)
    s = 1
    while s < n:
        x = x + jnp.where(row >= s, pltpu.roll(x, s, axis=0), 0.0)
        s *= 2
    return x


def _gate(xg, w2, b2):
    pre = jnp.dot(xg.astype(BF16), w2, preferred_element_type=F32) + b2
    return _log_sigmoid(pre) * (1.0 / GLA_TAU)


def _head_out(o, r, g):
    o = o * lax.rsqrt(jnp.mean(o * o, axis=-1, keepdims=True) + EPS)
    return o * g * (r * _sigmoid(r))


def _gla_prompt_kernel(q_ref, k_ref, v_ref, r_ref, xg_ref, kt_ref, vt_ref, xgt_ref,
                       w2_ref, b2_ref, g_ref, o_ref, s_ref, st_scr, *, seq, ns, nm):
    dk = q_ref.shape[1]
    scale = float(dk) ** -0.5
    w2 = w2_ref[...]
    b2 = b2_ref[...]
    g = g_ref[...]

    lgm = _gate(xgt_ref[ns:ns + nm, :], w2, b2)
    bm = _cumsum_rows(lgm)
    kdm = kt_ref[ns:ns + nm, :] * jnp.exp(bm[nm - 1:nm, :] - bm)
    st_scr[...] = lax.dot_general(vt_ref[ns:ns + nm, :].astype(BF16), kdm.astype(BF16),
                                  (((0,), (0,)), ((), ())), preferred_element_type=F32)

    c = GLA_CHUNK
    nsub = c // GLA_SUB

    def chunk(ci, carry):
        r0 = pl.multiple_of(ci * c, c)
        q = q_ref[pl.ds(r0, c), :] * scale
        k = k_ref[pl.ds(r0, c), :]
        v = v_ref[pl.ds(r0, c), :]
        vb = v.astype(BF16)
        lg = _gate(xg_ref[pl.ds(r0, c), :], w2, b2)
        b = _cumsum_rows(lg)
        bl = b[c - 1:c, :]
        st = st_scr[...]
        o_inter = lax.dot_general((q * jnp.exp(b)).astype(BF16), st.astype(BF16),
                                  (((1,), (1,)), ((), ())), preferred_element_type=F32)
        outs = []
        for sb in range(nsub):
            lo = sb * GLA_SUB
            hi = lo + GLA_SUB
            beta = b[lo - 1:lo, :] if sb > 0 else jnp.zeros((1, dk), F32)
            qs = q[lo:hi, :] * jnp.exp(b[lo:hi, :] - beta)
            ks = k[0:hi, :] * jnp.exp(beta - b[0:hi, :])
            sc = lax.dot_general(qs.astype(BF16), ks.astype(BF16),
                                 (((1,), (1,)), ((), ())), preferred_element_type=F32)
            rowi = lax.broadcasted_iota(jnp.int32, (GLA_SUB, hi), 0)
            coli = lax.broadcasted_iota(jnp.int32, (GLA_SUB, hi), 1)
            sc = jnp.where(coli <= rowi + lo, sc, 0.0)
            outs.append(jnp.dot(sc.astype(BF16), vb[0:hi, :], preferred_element_type=F32))
        o = o_inter + jnp.concatenate(outs, axis=0)
        o_ref[pl.ds(r0, c), :] = _head_out(o, r_ref[pl.ds(r0, c), :], g).astype(o_ref.dtype)
        kd = k * jnp.exp(bl - b)
        st_scr[...] = jnp.exp(bl) * st + lax.dot_general(
            vb, kd.astype(BF16), (((0,), (0,)), ((), ())), preferred_element_type=F32)
        return carry

    lax.fori_loop(0, seq // c, chunk, 0, unroll=min(8, seq // c))
    s_ref[...] = st_scr[...].T


def _gla_prompt(proj, xg, w2, b2, g, nb, seq, ns, nm, nh, dk, dv, q0, k0, v0, r0):
    tp = nb * seq
    tail_blk = tp // TAIL_ROWS
    rb = seq
    kern = functools.partial(_gla_prompt_kernel, seq=seq, ns=ns, nm=nm)
    return pl.pallas_call(
        kern,
        grid=(nb, nh),
        in_specs=[
            pl.BlockSpec((rb, dk), lambda b, h: (b, q0 // dk + h)),
            pl.BlockSpec((rb, dk), lambda b, h: (b, k0 // dk + h)),
            pl.BlockSpec((rb, dv), lambda b, h: (b, v0 // dv + h)),
            pl.BlockSpec((rb, dv), lambda b, h: (b, r0 // dv + h)),
            pl.BlockSpec((rb, LANES), lambda b, h: (b, 0)),
            pl.BlockSpec((TAIL_ROWS, dk), lambda b, h: (tail_blk, k0 // dk + h)),
            pl.BlockSpec((TAIL_ROWS, dv), lambda b, h: (tail_blk, v0 // dv + h)),
            pl.BlockSpec((TAIL_ROWS, LANES), lambda b, h: (tail_blk, 0)),
            pl.BlockSpec((LANES, dk), lambda b, h: (0, h)),
            pl.BlockSpec((1, dk), lambda b, h: (0, h)),
            pl.BlockSpec((1, dv), lambda b, h: (0, h)),
        ],
        out_specs=[
            pl.BlockSpec((rb, dv), lambda b, h: (b, h)),
            pl.BlockSpec((None, None, dk, dv), lambda b, h: (b, h, 0, 0)),
        ],
        out_shape=[
            jax.ShapeDtypeStruct((tp, nh * dv), BF16),
            jax.ShapeDtypeStruct((nb, nh, dk, dv), F32),
        ],
        scratch_shapes=[pltpu.VMEM((dv, dk), F32)],
        compiler_params=_cparams("parallel", "parallel"),
        name="gla_prompt",
    )(proj, proj, proj, proj, xg, proj, proj, xg, w2, b2, g)


def _gla_sample_kernel(q_ref, k_ref, v_ref, r_ref, xg_ref, w2_ref, b2_ref, g_ref, s0_ref,
                       o_ref, s_ref, *, nh, dk, dv):
    n = pl.program_id(0)
    j = n % SUBLANES
    scale = float(dk) ** -0.5
    lg_all = _gate(xg_ref[pl.ds(j, 1), :], w2_ref[...], b2_ref[...])
    q_all = q_ref[pl.ds(j, 1), :] * scale
    k_all = k_ref[pl.ds(j, 1), :]
    v_all = v_ref[pl.ds(j, 1), :]
    r_all = r_ref[pl.ds(j, 1), :]
    g_all = g_ref[...]
    rows = 2 * SUBLANES
    rk = lax.broadcasted_iota(jnp.int32, (rows, dk), 0)
    rv = lax.broadcasted_iota(jnp.int32, (rows, dv), 0)
    outs = []
    for h in range(nh):
        lg = lg_all[:, h * dk:(h + 1) * dk]
        q = q_all[:, h * dk:(h + 1) * dk]
        k = k_all[:, h * dk:(h + 1) * dk]
        v = v_all[:, h * dv:(h + 1) * dv]
        e = jnp.exp(lg)
        s0 = s0_ref[h]
        e_hi = e.astype(BF16)
        e_mid = (e - e_hi.astype(F32)).astype(BF16)
        e_lo = (e - e_hi.astype(F32) - e_mid.astype(F32)).astype(BF16)
        def rows_of(x, w):
            return jnp.broadcast_to(x.astype(F32), (rows, w))
        lhs = jnp.where(rk == 0, rows_of(e_hi, dk),
              jnp.where(rk == 1, rows_of(e_mid, dk),
              jnp.where(rk == 2, rows_of(e_lo, dk),
              jnp.where(rk == 3, rows_of(k, dk), 0.0)))).astype(BF16)
        ones_part = jnp.where(rv < 3, 1.0, 0.0)
        v_part = jnp.where(rv == 3, rows_of(v, dv), 0.0)
        rhs = jnp.concatenate([ones_part, v_part], axis=1).astype(BF16)
        both = lax.dot_general(lhs, rhs, (((0,), (0,)), ((), ())), preferred_element_type=F32)
        s_ref[h] = both[:, :dv] * s0 + both[:, dv:]
        qe = jnp.broadcast_to((q * e).astype(BF16), (rows, dk))
        o = jnp.dot(qe, s0.astype(BF16), preferred_element_type=F32)[0:1, :]
        o = o + jnp.sum(q * k, axis=-1, keepdims=True) * v
        outs.append(_head_out(o, r_all[:, h * dv:(h + 1) * dv], g_all[:, h * dv:(h + 1) * dv]))
    o_ref[pl.ds(j, 1), :] = jnp.concatenate(outs, axis=1)


def _gla_sample(proj, xg, w2, b2, g, s0, tp, ns, nh, dk, dv, q0, k0, v0, r0):
    rb = SUBLANES
    base = tp // rb
    qk_w = nh * dk
    v_w = nh * dv
    kern = functools.partial(_gla_sample_kernel, nh=nh, dk=dk, dv=dv)
    return pl.pallas_call(
        kern,
        grid=(ns,),
        in_specs=[
            pl.BlockSpec((rb, qk_w), lambda n: (base + n // rb, q0 // qk_w)),
            pl.BlockSpec((rb, qk_w), lambda n: (base + n // rb, k0 // qk_w)),
            pl.BlockSpec((rb, v_w), lambda n: (base + n // rb, v0 // v_w)),
            pl.BlockSpec((rb, v_w), lambda n: (base + n // rb, r0 // v_w)),
            pl.BlockSpec((rb, LANES), lambda n: (base + n // rb, 0)),
            pl.BlockSpec((LANES, qk_w), lambda n: (0, 0)),
            pl.BlockSpec((1, qk_w), lambda n: (0, 0)),
            pl.BlockSpec((1, v_w), lambda n: (0, 0)),
            pl.BlockSpec((None, nh, dk, dv), lambda n: (n, 0, 0, 0)),
        ],
        out_specs=[
            pl.BlockSpec((rb, v_w), lambda n: (n // rb, 0)),
            pl.BlockSpec((None, nh, dk, dv), lambda n: (n, 0, 0, 0)),
        ],
        out_shape=[
            jax.ShapeDtypeStruct((ns, v_w), F32),
            jax.ShapeDtypeStruct((ns, nh, dk, dv), F32),
        ],
        compiler_params=_cparams("arbitrary"),
        name="gla_sample",
    )(proj, proj, proj, proj, xg, w2, b2, g, s0)


def _out_proj_kernel(ys_ref, ogp_ref, ogt_ref, w1_ref, w2_ref, xp_ref, xt_ref, h_ref,
                     *, nfull, split):
    i = pl.program_id(0)
    w2 = w2_ref[...]
    acc = jnp.dot(ys_ref[...], w1_ref[...], preferred_element_type=F32)

    @pl.when(i < nfull)
    def _():
        h_ref[...] = xp_ref[...] + acc + jnp.dot(ogp_ref[...], w2, preferred_element_type=F32)

    @pl.when(i >= nfull)
    def _():
        if split:
            h_ref[0:split, :] = xp_ref[0:split, :] + acc[0:split] + jnp.dot(
                ogp_ref[0:split, :], w2, preferred_element_type=F32)
        h_ref[split:, :] = xt_ref[...] + acc[split:] + jnp.dot(
            ogt_ref[...], w2, preferred_element_type=F32)


def _out_proj(ys, og_p, og_t, w, x_p, x_t, tm, tn):
    t, dh = ys.shape
    d = w.shape[1]
    tp, tr = x_p.shape[0], x_t.shape[0]
    nfull = tp // tm
    split = tp - nfull * tm
    assert t == tp + tr and t == (nfull + 1) * tm and tm - split == tr
    last = pl.cdiv(tp, tm) - 1
    return pl.pallas_call(
        functools.partial(_out_proj_kernel, nfull=nfull, split=split),
        grid=(t // tm, d // tn),
        in_specs=[
            pl.BlockSpec((tm, dh), lambda i, j: (i, 0)),
            pl.BlockSpec((tm, dh), lambda i, j: (jnp.minimum(i, last), 0)),
            pl.BlockSpec((tr, dh), lambda i, j: (0, 0)),
            pl.BlockSpec((dh, tn), lambda i, j: (0, j)),
            pl.BlockSpec((dh, tn), lambda i, j: (1, j)),
            pl.BlockSpec((tm, tn), lambda i, j: (jnp.minimum(i, last), j)),
            pl.BlockSpec((tr, tn), lambda i, j: (0, j)),
        ],
        out_specs=pl.BlockSpec((tm, tn), lambda i, j: (i, j)),
        out_shape=jax.ShapeDtypeStruct((t, d), F32),
        compiler_params=_cparams("parallel", "parallel"),
        name="out_proj",
    )(ys, og_p, og_t, w, w, x_p, x_t)


def _norm_t_kernel(h_ref, g_ref, o_ref):
    o_ref[...] = _rms(h_ref[...], g_ref[...]).T.astype(BF16)


def _norm_t(h, g, tm):
    t, d = h.shape
    return pl.pallas_call(
        _norm_t_kernel,
        grid=(t // tm,),
        in_specs=[pl.BlockSpec((tm, d), lambda i: (i, 0)), pl.BlockSpec((1, d), lambda i: (0, 0))],
        out_specs=pl.BlockSpec((d, tm), lambda i: (0, i)),
        out_shape=jax.ShapeDtypeStruct((d, t), BF16),
        compiler_params=_cparams("parallel"),
        name="ffn_norm_t",
    )(h, g)


def _topk_rows(x, k):
    outs = []
    for _ in range(k):
        m = jnp.max(x, axis=0, keepdims=True)
        outs.append(m)
        x = jnp.where(x == m, -jnp.inf, x)
    return outs


def _route_kernel(wq_ref, hn_ref, k1_ref, k2_ref, s1_ref, c1_ref, s2_ref, e2_ref, tau_ref):
    half = k1_ref.shape[1]
    tb = hn_ref.shape[1]
    qt = jnp.dot(wq_ref[...], hn_ref[...], preferred_element_type=F32)
    s1 = jnp.dot(k1_ref[...].astype(BF16), qt[:half, :].astype(BF16), preferred_element_type=F32)
    s2 = jnp.dot(k2_ref[...].astype(BF16), qt[half:, :].astype(BF16), preferred_element_type=F32)
    s1_ref[...] = s1
    s2_ref[...] = s2
    for t in range(tb // LANES):
        sl = slice(t * LANES, (t + 1) * LANES)
        a = s1[:, sl]
        b = s2[:, sl]
        kk = PEER_TOPK
        v1 = _topk_rows(a, kk)
        v2l = _topk_rows(b, kk)
        v2 = jnp.concatenate(v2l, axis=0)
        cand = jnp.concatenate(
            [v1[0] + v2]
            + [v1[i] + v2[:kk // 2] for i in range(1, kk // 2)]
            + [jnp.concatenate(v1[kk // 2:], axis=0) + v2l[0]], axis=0)
        top = _topk_rows(cand, kk)
        z = jnp.zeros_like(top[0])
        for c in top:
            z = z + jnp.exp(c - top[0])
        tau_ref[:, sl] = top[PEER_TOPK - 1]
        c1_ref[:, sl] = jnp.exp(a - v1[0]) / z
        e2_ref[:, sl] = jnp.exp(b - v2[0:1, :])


def _route(wq_t, hn_t, k1, k2, tb):
    d, t = hn_t.shape
    nh, nk, half = k1.shape
    qd = 2 * half
    tab = jax.ShapeDtypeStruct((nh, nk, t), F32)
    tab_spec = pl.BlockSpec((None, nk, tb), lambda i, h: (h, 0, i))
    return pl.pallas_call(
        _route_kernel,
        grid=(t // tb, nh),
        in_specs=[
            pl.BlockSpec((qd, d), lambda i, h: (h, 0)),
            pl.BlockSpec((d, tb), lambda i, h: (0, i)),
            pl.BlockSpec((None, nk, half), lambda i, h: (h, 0, 0)),
            pl.BlockSpec((None, nk, half), lambda i, h: (h, 0, 0)),
        ],
        out_specs=[tab_spec, tab_spec, tab_spec, tab_spec,
                   pl.BlockSpec((None, 1, tb), lambda i, h: (h, 0, i))],
        out_shape=[tab, tab, tab, tab, jax.ShapeDtypeStruct((nh, 1, t), F32)],
        compiler_params=_cparams("parallel", "parallel"),
        name="peer_route",
    )(wq_t, hn_t, k1, k2)


GATE_ROWS = 32
OUT_ROWS = 512


K_CHUNK = 512
EXPERT_BLOCK = 512


def _peer_kernel(u_ref, vt_ref, hn_ref, s1_ref, c1_ref, s2_ref, e2_ref, tau_ref, o_ref,
                 w_scr, act_scr, a_scr):
    eb, d = u_ref.shape
    tb = hn_ref.shape[1]
    nh, nk, _ = s2_ref.shape
    nr = eb // nk

    @pl.when(pl.program_id(1) == 0)
    def _():
        o_ref[...] = jnp.zeros_like(o_ref)

    def gate_group(t, q):
        sl = slice(t * LANES, (t + 1) * LANES)
        rows = slice(q * GATE_ROWS, (q + 1) * GATE_ROWS)
        accs = [None] * nr
        for h in range(nh):
            s2t = s2_ref[h, rows, sl]
            e2t = e2_ref[h, rows, sl]
            tau = tau_ref[h, :, sl]
            for r in range(nr):
                ssum = s1_ref[r, h:h + 1, sl] + s2t
                term = jnp.where(ssum >= tau, c1_ref[r, h:h + 1, sl] * e2t, 0.0)
                accs[r] = term if accs[r] is None else accs[r] + term
        for r in range(nr):
            lo = r * nk + q * GATE_ROWS
            w_scr[lo:lo + GATE_ROWS, sl] = accs[r]

    groups = [(t, q) for t in range(tb // LANES) for q in range(nk // GATE_ROWS)]
    nkc = d // K_CHUNK
    per = -(-len(groups) // nkc)
    for kc in range(nkc):
        @pl.when(pl.program_id(1) >= 0)
        def _(kc=kc):
            ks = slice(kc * K_CHUNK, (kc + 1) * K_CHUNK)
            part = jnp.dot(u_ref[:, ks], hn_ref[ks, :], preferred_element_type=F32)
            if kc == 0:
                act_scr[...] = part
            else:
                act_scr[...] += part
            for t, q in groups[kc * per:(kc + 1) * per]:
                gate_group(t, q)
    a_scr[...] = (w_scr[...] * _gelu(act_scr[...])).astype(BF16)
    for dc in range(d // OUT_ROWS):
        dr = slice(dc * OUT_ROWS, (dc + 1) * OUT_ROWS)
        o_ref[dr, :] += jnp.dot(vt_ref[dr, :], a_scr[...], preferred_element_type=F32)


def _peer(u_bf, v_t, hn_t, s1, c1, s2, e2, tau, tb, eb):
    ne, d = u_bf.shape
    t = hn_t.shape[1]
    nh, nk, _ = s2.shape
    once = pl.Buffered(1)
    tab_spec = pl.BlockSpec((nh, nk, tb), lambda i, e: (0, 0, i), pipeline_mode=once)
    row_spec = pl.BlockSpec((eb // nk, nh, tb), lambda i, e: (e, 0, i))
    return pl.pallas_call(
        _peer_kernel,
        grid=(t // tb, ne // eb),
        in_specs=[
            pl.BlockSpec((eb, d), lambda i, e: (e, 0)),
            pl.BlockSpec((d, eb), lambda i, e: (0, e)),
            pl.BlockSpec((d, tb), lambda i, e: (0, i), pipeline_mode=once),
            row_spec, row_spec, tab_spec, tab_spec,
            pl.BlockSpec((nh, 1, tb), lambda i, e: (0, 0, i)),
        ],
        out_specs=pl.BlockSpec((d, tb), lambda i, e: (0, i)),
        out_shape=jax.ShapeDtypeStruct((d, t), F32),
        scratch_shapes=[pltpu.VMEM((eb, tb), F32), pltpu.VMEM((eb, tb), F32),
                        pltpu.VMEM((eb, tb), BF16)],
        compiler_params=pltpu.CompilerParams(
            dimension_semantics=("parallel", "arbitrary"), vmem_limit_bytes=PEER_VMEM_LIMIT),
        name="peer_experts",
    )(u_bf, v_t, hn_t, s1, c1, s2, e2, tau)


def _final_kernel(pt_ref, h_ref, g_ref, yp_ref, yt_ref, *, nprompt):
    i = pl.program_id(0)
    y = _rms(h_ref[...] + pt_ref[...].T, g_ref[...])

    @pl.when(i < nprompt)
    def _():
        yp_ref[...] = y

    @pl.when(i >= nprompt)
    def _():
        yt_ref[...] = y


def _final(peer_t, h, g, tp, tr):
    d, t = peer_t.shape
    nprompt = tp // tr
    return pl.pallas_call(
        functools.partial(_final_kernel, nprompt=nprompt),
        grid=(nprompt + 1,),
        in_specs=[
            pl.BlockSpec((d, tr), lambda i: (0, i)),
            pl.BlockSpec((tr, d), lambda i: (i, 0)),
            pl.BlockSpec((1, d), lambda i: (0, 0)),
        ],
        out_specs=[
            pl.BlockSpec((tr, d), lambda i: (jnp.minimum(i, nprompt - 1), 0)),
            pl.BlockSpec((tr, d), lambda i: (0, 0)),
        ],
        out_shape=[jax.ShapeDtypeStruct((tp, d), F32), jax.ShapeDtypeStruct((tr, d), F32)],
        compiler_params=_cparams("arbitrary"),
        name="final_norm",
    )(peer_t, h, g)


def _block_diag(w, ngrp):
    nslab, _, a, b = w.shape
    eye = jnp.eye(ngrp, dtype=w.dtype)
    full = w[:, :, :, None, :] * eye[None, :, None, :, None]
    return full.reshape(nslab, ngrp * a, ngrp * b)


def kernel(x_prompt, x_sample, state_s5_re, state_s5_im, state_gla, meta_tokens, norm_mix_g, w_in, s5_lam_re, s5_lam_im, s5_log_dt, s5_b_re, s5_b_im, s5_c_re, s5_c_im, s5_d, s5_w_glu, s5_b_glu, s5_norm_g, gla_w_gate2, gla_b_gate2, gla_norm_g, w_out, norm_ffn_g, peer_w_q, peer_keys, peer_u, peer_v, norm_final_g):
    nb, seq, d = x_prompt.shape
    ns = x_sample.shape[0]
    nm = meta_tokens.shape[0]
    depth = w_in.shape[0]
    assert depth == 1 and x_sample.shape[1] == 1
    tp = nb * seq
    assert tp % TAIL_ROWS == 0 and ns + nm <= TAIL_ROWS and ns % SUBLANES == 0
    assert seq % GLA_CHUNK == 0 and nm % SUBLANES == 0
    t_all = tp + TAIL_ROWS

    ngrp, nstate, gch = s5_b_re.shape[1:]
    d_ssm = ngrp * gch
    gps = LANES // gch
    nslab = ngrp // gps
    nh, dk, dv = state_gla.shape[2:]
    d_gla = nh * dv
    rank = gla_w_gate2.shape[1]
    q0 = d_ssm
    k0 = q0 + nh * dk
    v0 = k0 + nh * dk
    r0 = v0 + d_gla
    g0 = r0 + d_gla
    assert g0 + rank == w_in.shape[2] and d_ssm + d_gla == w_out.shape[1]

    x_p = x_prompt.reshape(tp, d)
    x_t = jnp.concatenate([x_sample.reshape(ns, d), meta_tokens,
                           jnp.zeros((TAIL_ROWS - ns - nm, d), F32)], axis=0)
    w_in_bf = w_in[0].astype(BF16)
    w_g1 = jnp.pad(w_in[0, :, g0:], ((0, 0), (0, LANES - rank))).astype(BF16)
    w_g2 = jnp.pad(gla_w_gate2[0], ((0, LANES - rank), (0, 0))).astype(BF16)
    lam_re = s5_lam_re[0].reshape(1, ngrp * nstate)
    lam_im = s5_lam_im[0].reshape(1, ngrp * nstate)
    logdt = jnp.repeat(s5_log_dt[0], nstate).reshape(1, ngrp * nstate)
    b_re4 = jnp.transpose(s5_b_re[0].reshape(nslab, gps, nstate, gch), (0, 1, 3, 2))
    b_im4 = jnp.transpose(s5_b_im[0].reshape(nslab, gps, nstate, gch), (0, 1, 3, 2))
    c_re4 = jnp.transpose(s5_c_re[0].reshape(nslab, gps, gch, nstate), (0, 1, 3, 2))
    c_im4 = jnp.transpose(s5_c_im[0].reshape(nslab, gps, gch, nstate), (0, 1, 3, 2))
    bre_bd = _block_diag(b_re4, gps)
    bim_bd = _block_diag(b_im4, gps)
    cre_bd = _block_diag(c_re4, gps)
    cim_bd = _block_diag(c_im4, gps)
    h0r = state_s5_re[0].reshape(ns, ngrp * nstate)
    h0i = state_s5_im[0].reshape(ns, ngrp * nstate)

    tm = _row_tile(t_all, (768, 512, 256))

    xn = _prenorm(x_p, x_t, norm_mix_g)
    proj, xg = _in_proj(xn, w_in_bf, w_g1, tm, PROJ_COLS, g0)

    y_raw, pr, pi_, sr, si = _s5(proj, h0r, h0i, lam_re, lam_im, logdt, bre_bd, bim_bd,
                                 cre_bd, cim_bd, s5_d, nb, seq, ns, nm)
    y_ssm = _glu_norm(y_raw, s5_w_glu[0].astype(BF16), s5_b_glu, s5_norm_g, TAIL_ROWS)

    og_p, gla_p = _gla_prompt(proj, xg, w_g2, gla_b_gate2, gla_norm_g, nb, seq, ns, nm,
                              nh, dk, dv, q0, k0, v0, r0)
    og_s, gla_s = _gla_sample(proj, xg, w_g2, gla_b_gate2, gla_norm_g, state_gla[0],
                              tp, ns, nh, dk, dv, q0, k0, v0, r0)
    og_t = jnp.concatenate([og_s.astype(BF16), jnp.zeros((TAIL_ROWS - ns, d_gla), BF16)], axis=0)

    w_o = w_out[0].astype(BF16)
    assert d_ssm == d_gla
    h = _out_proj(y_ssm, og_p, og_t, w_o, x_p, x_t, tm, PROJ_COLS)

    hn_t = _norm_t(h, norm_ffn_g, TAIL_ROWS)
    wq_t = _transpose_cast(peer_w_q[0], WT_ROWS, WT_COLS)
    s1, c1, s2, e2, tau = _route(wq_t, hn_t, peer_keys[0, :, 0], peer_keys[0, :, 1], tm)
    u_bf = peer_u[0].astype(BF16)
    v_t = _transpose_cast(peer_v[0], WT_ROWS, WT_COLS)
    peer_t = _peer(u_bf, v_t, hn_t, jnp.transpose(s1, (1, 0, 2)), jnp.transpose(c1, (1, 0, 2)),
                   s2, e2, tau, tm, EXPERT_BLOCK)
    y_p, y_t = _final(peer_t, h, norm_final_g.reshape(1, d), tp, TAIL_ROWS)

    y_prompt = y_p.reshape(nb, seq, d)
    y_sample = y_t[:ns].reshape(ns, 1, d)
    return (y_prompt, y_sample,
            pr.reshape(1, nb, ngrp, nstate), pi_.reshape(1, nb, ngrp, nstate), gla_p[None],
            sr.reshape(1, ns, ngrp, nstate), si.reshape(1, ns, ngrp, nstate), gla_s[None])
```

```python
import functools

import jax
import jax.numpy as jnp
from jax import lax
from jax.experimental import pallas as pl
from jax.experimental.pallas import tpu as pltpu

F32 = jnp.float32
BF16 = jnp.bfloat16

EPS = 1e-6
GLA_TAU = 16.0
GLA_CHUNK = 64
GLA_SUB = 16
PEER_TOPK = 16
LANES = 128
SUBLANES = 8
TAIL_ROWS = 256
PROJ_COLS = 1024
MXU_DEPTH = 256
WT_ROWS = 2048
WT_COLS = 1024
VMEM_LIMIT = 56 * 1024 * 1024
PEER_VMEM_LIMIT = 60 * 1024 * 1024


def _cparams(*sem):
    return pltpu.CompilerParams(dimension_semantics=sem, vmem_limit_bytes=VMEM_LIMIT)


def _gelu(x):
    return 0.5 * x * (1.0 + jnp.tanh(0.7978845608028654 * (x + 0.044715 * (x * x * x))))


def _sigmoid(x):
    return 1.0 / (1.0 + jnp.exp(-x))


def _log_sigmoid(x):
    return jnp.minimum(x, 0.0) - jnp.log(1.0 + jnp.exp(-jnp.abs(x)))


def _row_tile(n, cands):
    for c in cands:
        if n % c == 0:
            return c
    raise ValueError(f"no row tile for {n}")


def _transpose_block(x_ref, o_ref):
    k = MXU_DEPTH
    eye = jnp.where(lax.broadcasted_iota(jnp.int32, (k, k), 0)
                    == lax.broadcasted_iota(jnp.int32, (k, k), 1), 1.0, 0.0).astype(BF16)
    for c in range(x_ref.shape[1] // k):
        xb = x_ref[:, c * k:(c + 1) * k].astype(BF16)
        o_ref[c * k:(c + 1) * k, :] = lax.dot_general(
            eye, xb, (((1,), (1,)), ((), ())), preferred_element_type=F32).astype(BF16)


def _transpose_cast_kernel(x_ref, o_ref):
    _transpose_block(x_ref, o_ref)


def _transpose_cast(x, rb, cb):
    r, c = x.shape
    return pl.pallas_call(
        _transpose_cast_kernel,
        grid=(r // rb, c // cb),
        in_specs=[pl.BlockSpec((rb, cb), lambda i, j: (i, j))],
        out_specs=pl.BlockSpec((cb, rb), lambda i, j: (j, i)),
        out_shape=jax.ShapeDtypeStruct((c, r), BF16),
        compiler_params=_cparams("parallel", "parallel"),
        name="transpose_cast",
    )(x)


def _rms(x, g):
    return x * lax.rsqrt(jnp.mean(x * x, axis=-1, keepdims=True) + EPS) * g


def _prenorm_kernel(xp_ref, xt_ref, g_ref, o_ref, *, nprompt):
    i = pl.program_id(0)

    @pl.when(i < nprompt)
    def _():
        o_ref[...] = _rms(xp_ref[...], g_ref[...]).astype(BF16)

    @pl.when(i >= nprompt)
    def _():
        o_ref[...] = _rms(xt_ref[...], g_ref[...]).astype(BF16)


def _prenorm(x_p, x_t, g):
    tp, d = x_p.shape
    tr = x_t.shape[0]
    nprompt = tp // tr
    return pl.pallas_call(
        functools.partial(_prenorm_kernel, nprompt=nprompt),
        grid=(nprompt + 1,),
        in_specs=[
            pl.BlockSpec((tr, d), lambda i: (jnp.minimum(i, nprompt - 1), 0)),
            pl.BlockSpec((tr, d), lambda i: (0, 0)),
            pl.BlockSpec((1, d), lambda i: (0, 0)),
        ],
        out_specs=pl.BlockSpec((tr, d), lambda i: (i, 0)),
        out_shape=jax.ShapeDtypeStruct((tp + tr, d), BF16),
        compiler_params=_cparams("arbitrary"),
        name="mix_norm",
    )(x_p, x_t, g)


def _in_proj_kernel(xn_ref, w_ref, wg_ref, side_ref, o_ref, og_ref, side_o_ref, *, nside):
    nj = pl.num_programs(1)

    @pl.when(pl.program_id(1) == 0)
    def _():
        og_ref[...] = jnp.dot(xn_ref[...], wg_ref[...], preferred_element_type=F32)

    o_ref[...] = jnp.dot(xn_ref[...], w_ref[...], preferred_element_type=F32)

    @pl.when(pl.program_id(0) * nj + pl.program_id(1) < nside)
    def _():
        side_o_ref[...] = side_ref[...].astype(BF16)


def _in_proj(xn, w, wg, side, tm, tn, n):
    t, d = xn.shape
    ni, nj = t // tm, n // tn
    rs, cs = side.shape
    nside = max(k for k in range(1, ni * nj + 1) if rs % k == 0 and (rs // k) % SUBLANES == 0)
    rb = rs // nside
    side_idx = lambda i, j: (jnp.minimum(i * nj + j, nside - 1), 0)
    return pl.pallas_call(
        functools.partial(_in_proj_kernel, nside=nside),
        grid=(ni, nj),
        in_specs=[
            pl.BlockSpec((tm, d), lambda i, j: (i, 0)),
            pl.BlockSpec((d, tn), lambda i, j: (0, j)),
            pl.BlockSpec((d, LANES), lambda i, j: (0, 0)),
            pl.BlockSpec((rb, cs), side_idx),
        ],
        out_specs=[
            pl.BlockSpec((tm, tn), lambda i, j: (i, j)),
            pl.BlockSpec((tm, LANES), lambda i, j: (i, 0)),
            pl.BlockSpec((rb, cs), side_idx),
        ],
        out_shape=[jax.ShapeDtypeStruct((t, n), F32), jax.ShapeDtypeStruct((t, LANES), F32),
                   jax.ShapeDtypeStruct((rs, cs), BF16)],
        compiler_params=_cparams("arbitrary", "arbitrary"),
        name="in_proj",
    )(xn, w, wg, side)


def _s5_kernel(u_ref, h0r_ref, h0i_ref, lr_ref, li_ref, ldt_ref, bre_ref, bim_ref,
               cre_ref, cim_ref, d_ref,
               y_ref, pr_ref, pi_ref, sr_ref, si_ref,
               x_scr, init_scr, up_scr, yp_scr, *, nb, seq, ns, nm):
    sw = lr_ref.shape[1]
    tp = nb * seq
    nseg = SUBLANES
    ls = seq // nseg

    lr = lr_ref[...]
    li = li_ref[...]
    dt = jnp.exp(ldt_ref[...])
    mag = jnp.exp(lr * dt)
    ang = li * dt
    ar = mag * jnp.cos(ang)
    ai = mag * jnp.sin(ang)
    den = lr * lr + li * li
    nr = ar - 1.0
    qr = (nr * lr + ai * li) / den
    qi = (ai * lr - nr * li) / den
    bre = bre_ref[...]
    bim = bim_ref[...]
    bcat = jnp.concatenate([qr * bre - qi * bim, qr * bim + qi * bre], axis=1).astype(BF16)
    ccat = jnp.concatenate([cre_ref[...], -cim_ref[...]], axis=0).astype(BF16)
    dvec = d_ref[...]
    mag_s = jnp.exp(lr * dt * float(ls))
    asr = mag_s * jnp.cos(ang * float(ls))
    asi = mag_s * jnp.sin(ang * float(ls))

    def project_out(h, u):
        return jnp.dot(h.astype(BF16), ccat, preferred_element_type=F32) + dvec * u

    u_s = u_ref[tp:tp + ns, :]
    x_s = jnp.dot(u_s.astype(BF16), bcat, preferred_element_type=F32)
    h0r = h0r_ref[...]
    h0i = h0i_ref[...]
    hr_s = ar * h0r - ai * h0i + x_s[:, :sw]
    hi_s = ar * h0i + ai * h0r + x_s[:, sw:]
    sr_ref[...] = hr_s
    si_ref[...] = hi_s
    y_ref[tp:tp + ns, :] = project_out(jnp.concatenate([hr_s, hi_s], axis=1), u_s)
    y_ref[tp + ns:, :] = jnp.zeros((y_ref.shape[0] - tp - ns, y_ref.shape[1]), F32)

    u_m = u_ref[tp + ns:tp + ns + nm, :]
    x_m = jnp.dot(u_m.astype(BF16), bcat, preferred_element_type=F32)
    mr = jnp.zeros((1, sw), F32)
    mi = jnp.zeros((1, sw), F32)
    for t in range(nm):
        mr, mi = (ar * mr - ai * mi + x_m[t:t + 1, :sw],
                  ar * mi + ai * mr + x_m[t:t + 1, sw:])

    ar8 = jnp.broadcast_to(ar, (nseg, sw))
    ai8 = jnp.broadcast_to(ai, (nseg, sw))
    unroll = 4 if ls % 4 == 0 else 1

    def scan_pass(store):
        def body(i, c):
            hr, hi = c
            r = pl.multiple_of(i * nseg, nseg)
            nhr = ar8 * hr - ai8 * hi + x_scr[pl.ds(r, nseg), 0:sw]
            nhi = ar8 * hi + ai8 * hr + x_scr[pl.ds(r, nseg), sw:2 * sw]
            if store:
                x_scr[pl.ds(r, nseg), 0:sw] = nhr
                x_scr[pl.ds(r, nseg), sw:2 * sw] = nhi
            return nhr, nhi
        return body

    for b in range(nb):
        def regroup_in(i, c, b=b):
            r = pl.multiple_of(i * nseg, nseg)
            up_scr[pl.ds(r, nseg), :] = u_ref[pl.ds(b * seq + i, nseg, stride=ls), :]
            return c
        lax.fori_loop(0, ls, regroup_in, 0, unroll=unroll)
        u_b = up_scr[...]
        x_scr[...] = jnp.dot(u_b.astype(BF16), bcat, preferred_element_type=F32)
        z = jnp.zeros((nseg, sw), F32)
        fr, fi = lax.fori_loop(0, ls, scan_pass(False), (z, z), unroll=unroll)
        cr, ci = mr, mi
        for k in range(nseg):
            init_scr[k:k + 1, 0:sw] = cr
            init_scr[k:k + 1, sw:2 * sw] = ci
            cr, ci = (asr * cr - asi * ci + fr[k:k + 1, :],
                      asr * ci + asi * cr + fi[k:k + 1, :])
        pr_ref[b:b + 1, :] = cr
        pi_ref[b:b + 1, :] = ci
        lax.fori_loop(0, ls, scan_pass(True), (init_scr[:, 0:sw], init_scr[:, sw:2 * sw]),
                      unroll=unroll)
        yp_scr[...] = project_out(x_scr[...], u_b)

        def regroup_out(i, c, b=b):
            r = pl.multiple_of(i * nseg, nseg)
            y_ref[pl.ds(b * seq + i, nseg, stride=ls), :] = yp_scr[pl.ds(r, nseg), :]
            return c
        lax.fori_loop(0, ls, regroup_out, 0, unroll=unroll)


def _s5(proj, h0r, h0i, lam_re, lam_im, logdt, bre_bd, bim_bd, cre_bd, cim_bd, dvec,
        nb, seq, ns, nm):
    t_all = proj.shape[0]
    nslab, cw, sw = bre_bd.shape
    d_ssm = nslab * cw
    kern = functools.partial(_s5_kernel, nb=nb, seq=seq, ns=ns, nm=nm)
    return pl.pallas_call(
        kern,
        grid=(nslab,),
        in_specs=[
            pl.BlockSpec((t_all, cw), lambda s: (0, s)),
            pl.BlockSpec((ns, sw), lambda s: (0, s)),
            pl.BlockSpec((ns, sw), lambda s: (0, s)),
            pl.BlockSpec((1, sw), lambda s: (0, s)),
            pl.BlockSpec((1, sw), lambda s: (0, s)),
            pl.BlockSpec((1, sw), lambda s: (0, s)),
            pl.BlockSpec((None, cw, sw), lambda s: (s, 0, 0)),
            pl.BlockSpec((None, cw, sw), lambda s: (s, 0, 0)),
            pl.BlockSpec((None, sw, cw), lambda s: (s, 0, 0)),
            pl.BlockSpec((None, sw, cw), lambda s: (s, 0, 0)),
            pl.BlockSpec((1, cw), lambda s: (0, s)),
        ],
        out_specs=[
            pl.BlockSpec((t_all, cw), lambda s: (0, s)),
            pl.BlockSpec((nb, sw), lambda s: (0, s)),
            pl.BlockSpec((nb, sw), lambda s: (0, s)),
            pl.BlockSpec((ns, sw), lambda s: (0, s)),
            pl.BlockSpec((ns, sw), lambda s: (0, s)),
        ],
        out_shape=[
            jax.ShapeDtypeStruct((t_all, d_ssm), F32),
            jax.ShapeDtypeStruct((nb, nslab * sw), F32),
            jax.ShapeDtypeStruct((nb, nslab * sw), F32),
            jax.ShapeDtypeStruct((ns, nslab * sw), F32),
            jax.ShapeDtypeStruct((ns, nslab * sw), F32),
        ],
        scratch_shapes=[pltpu.VMEM((seq, 2 * sw), F32), pltpu.VMEM((SUBLANES, 2 * sw), F32),
                        pltpu.VMEM((seq, cw), F32), pltpu.VMEM((seq, cw), F32)],
        compiler_params=_cparams("parallel"),
        name="s5_scan",
    )(proj, h0r, h0i, lam_re, lam_im, logdt, bre_bd, bim_bd, cre_bd, cim_bd, dvec)


def _glu_norm_kernel(y_ref, w_ref, b_ref, g_ref, o_ref):
    z = _gelu(y_ref[...])
    gate = jnp.dot(z.astype(BF16), w_ref[...], preferred_element_type=F32) + b_ref[...]
    zz = z * _sigmoid(gate)
    s = lax.rsqrt(jnp.mean(zz * zz, axis=-1, keepdims=True) + EPS)
    o_ref[...] = (zz * s * g_ref[...]).astype(o_ref.dtype)


def _glu_norm(y, w, b, g, tm):
    t, d = y.shape
    return pl.pallas_call(
        _glu_norm_kernel,
        grid=(t // tm,),
        in_specs=[
            pl.BlockSpec((tm, d), lambda i: (i, 0)),
            pl.BlockSpec((d, d), lambda i: (0, 0)),
            pl.BlockSpec((1, d), lambda i: (0, 0)),
            pl.BlockSpec((1, d), lambda i: (0, 0)),
        ],
        out_specs=pl.BlockSpec((tm, d), lambda i: (i, 0)),
        out_shape=jax.ShapeDtypeStruct((t, d), BF16),
        compiler_params=_cparams("parallel"),
        name="s5_glu_norm",
    )(y, w, b, g)


def _cumsum_rows(x):
    n = x.shape[0]
    row = lax.broadcasted_iota(jnp.int32, x.shape, 0)
    s = 1
    while s < n:
        x = x + jnp.where(row >= s, pltpu.roll(x, s, axis=0), 0.0)
        s *= 2
    return x


def _gate(xg, w2, b2):
    pre = jnp.dot(xg.astype(BF16), w2, preferred_element_type=F32) + b2
    return _log_sigmoid(pre) * (1.0 / GLA_TAU)


def _head_out(o, r, g):
    o = o * lax.rsqrt(jnp.mean(o * o, axis=-1, keepdims=True) + EPS)
    return o * g * (r * _sigmoid(r))


def _gla_prompt_kernel(q_ref, k_ref, v_ref, r_ref, xg_ref, kt_ref, vt_ref, xgt_ref,
                       w2_ref, b2_ref, g_ref, o_ref, s_ref, st_scr, *, seq, ns, nm):
    dk = q_ref.shape[1]
    scale = float(dk) ** -0.5
    w2 = w2_ref[...]
    b2 = b2_ref[...]
    g = g_ref[...]

    lgm = _gate(xgt_ref[ns:ns + nm, :], w2, b2)
    bm = _cumsum_rows(lgm)
    kdm = kt_ref[ns:ns + nm, :] * jnp.exp(bm[nm - 1:nm, :] - bm)
    st_scr[...] = lax.dot_general(vt_ref[ns:ns + nm, :].astype(BF16), kdm.astype(BF16),
                                  (((0,), (0,)), ((), ())), preferred_element_type=F32)

    c = GLA_CHUNK
    nsub = c // GLA_SUB

    def chunk(ci, carry):
        r0 = pl.multiple_of(ci * c, c)
        q = q_ref[pl.ds(r0, c), :] * scale
        k = k_ref[pl.ds(r0, c), :]
        v = v_ref[pl.ds(r0, c), :]
        vb = v.astype(BF16)
        lg = _gate(xg_ref[pl.ds(r0, c), :], w2, b2)
        b = _cumsum_rows(lg)
        bl = b[c - 1:c, :]
        st = st_scr[...]
        o_inter = lax.dot_general((q * jnp.exp(b)).astype(BF16), st.astype(BF16),
                                  (((1,), (1,)), ((), ())), preferred_element_type=F32)
        outs = []
        for sb in range(nsub):
            lo = sb * GLA_SUB
            hi = lo + GLA_SUB
            beta = b[lo - 1:lo, :] if sb > 0 else jnp.zeros((1, dk), F32)
            qs = q[lo:hi, :] * jnp.exp(b[lo:hi, :] - beta)
            ks = k[0:hi, :] * jnp.exp(beta - b[0:hi, :])
            sc = lax.dot_general(qs.astype(BF16), ks.astype(BF16),
                                 (((1,), (1,)), ((), ())), preferred_element_type=F32)
            rowi = lax.broadcasted_iota(jnp.int32, (GLA_SUB, hi), 0)
            coli = lax.broadcasted_iota(jnp.int32, (GLA_SUB, hi), 1)
            sc = jnp.where(coli <= rowi + lo, sc, 0.0)
            outs.append(jnp.dot(sc.astype(BF16), vb[0:hi, :], preferred_element_type=F32))
        o = o_inter + jnp.concatenate(outs, axis=0)
        o_ref[pl.ds(r0, c), :] = _head_out(o, r_ref[pl.ds(r0, c), :], g).astype(o_ref.dtype)
        kd = k * jnp.exp(bl - b)
        st_scr[...] = jnp.exp(bl) * st + lax.dot_general(
            vb, kd.astype(BF16), (((0,), (0,)), ((), ())), preferred_element_type=F32)
        return carry

    lax.fori_loop(0, seq // c, chunk, 0, unroll=min(8, seq // c))
    s_ref[...] = st_scr[...].T


def _gla_prompt(proj, xg, w2, b2, g, nb, seq, ns, nm, nh, dk, dv, q0, k0, v0, r0):
    tp = nb * seq
    tail_blk = tp // TAIL_ROWS
    rb = seq
    kern = functools.partial(_gla_prompt_kernel, seq=seq, ns=ns, nm=nm)
    return pl.pallas_call(
        kern,
        grid=(nb, nh),
        in_specs=[
            pl.BlockSpec((rb, dk), lambda b, h: (b, q0 // dk + h)),
            pl.BlockSpec((rb, dk), lambda b, h: (b, k0 // dk + h)),
            pl.BlockSpec((rb, dv), lambda b, h: (b, v0 // dv + h)),
            pl.BlockSpec((rb, dv), lambda b, h: (b, r0 // dv + h)),
            pl.BlockSpec((rb, LANES), lambda b, h: (b, 0)),
            pl.BlockSpec((TAIL_ROWS, dk), lambda b, h: (tail_blk, k0 // dk + h)),
            pl.BlockSpec((TAIL_ROWS, dv), lambda b, h: (tail_blk, v0 // dv + h)),
            pl.BlockSpec((TAIL_ROWS, LANES), lambda b, h: (tail_blk, 0)),
            pl.BlockSpec((LANES, dk), lambda b, h: (0, h)),
            pl.BlockSpec((1, dk), lambda b, h: (0, h)),
            pl.BlockSpec((1, dv), lambda b, h: (0, h)),
        ],
        out_specs=[
            pl.BlockSpec((rb, dv), lambda b, h: (b, h)),
            pl.BlockSpec((None, None, dk, dv), lambda b, h: (b, h, 0, 0)),
        ],
        out_shape=[
            jax.ShapeDtypeStruct((tp, nh * dv), BF16),
            jax.ShapeDtypeStruct((nb, nh, dk, dv), F32),
        ],
        scratch_shapes=[pltpu.VMEM((dv, dk), F32)],
        compiler_params=_cparams("parallel", "parallel"),
        name="gla_prompt",
    )(proj, proj, proj, proj, xg, proj, proj, xg, w2, b2, g)


def _gla_sample_kernel(q_ref, k_ref, v_ref, r_ref, xg_ref, w2_ref, b2_ref, g_ref, s0_ref,
                       o_ref, s_ref, *, nh, dk, dv):
    n = pl.program_id(0)
    j = n % SUBLANES
    scale = float(dk) ** -0.5
    lg_all = _gate(xg_ref[pl.ds(j, 1), :], w2_ref[...], b2_ref[...])
    q_all = q_ref[pl.ds(j, 1), :] * scale
    k_all = k_ref[pl.ds(j, 1), :]
    v_all = v_ref[pl.ds(j, 1), :]
    r_all = r_ref[pl.ds(j, 1), :]
    g_all = g_ref[...]
    rows = 2 * SUBLANES
    rk = lax.broadcasted_iota(jnp.int32, (rows, dk), 0)
    rv = lax.broadcasted_iota(jnp.int32, (rows, dv), 0)
    outs = []
    for h in range(nh):
        lg = lg_all[:, h * dk:(h + 1) * dk]
        q = q_all[:, h * dk:(h + 1) * dk]
        k = k_all[:, h * dk:(h + 1) * dk]
        v = v_all[:, h * dv:(h + 1) * dv]
        e = jnp.exp(lg)
        s0 = s0_ref[h]
        e_hi = e.astype(BF16)
        e_mid = (e - e_hi.astype(F32)).astype(BF16)
        e_lo = (e - e_hi.astype(F32) - e_mid.astype(F32)).astype(BF16)
        def rows_of(x, w):
            return jnp.broadcast_to(x.astype(F32), (rows, w))
        lhs = jnp.where(rk == 0, rows_of(e_hi, dk),
              jnp.where(rk == 1, rows_of(e_mid, dk),
              jnp.where(rk == 2, rows_of(e_lo, dk),
              jnp.where(rk == 3, rows_of(k, dk), 0.0)))).astype(BF16)
        ones_part = jnp.where(rv < 3, 1.0, 0.0)
        v_part = jnp.where(rv == 3, rows_of(v, dv), 0.0)
        rhs = jnp.concatenate([ones_part, v_part], axis=1).astype(BF16)
        both = lax.dot_general(lhs, rhs, (((0,), (0,)), ((), ())), preferred_element_type=F32)
        s_ref[h] = both[:, :dv] * s0 + both[:, dv:]
        qe = jnp.broadcast_to((q * e).astype(BF16), (rows, dk))
        o = jnp.dot(qe, s0.astype(BF16), preferred_element_type=F32)[0:1, :]
        o = o + jnp.sum(q * k, axis=-1, keepdims=True) * v
        outs.append(_head_out(o, r_all[:, h * dv:(h + 1) * dv], g_all[:, h * dv:(h + 1) * dv]))
    o_ref[pl.ds(j, 1), :] = jnp.concatenate(outs, axis=1)


def _gla_sample(proj, xg, w2, b2, g, s0, tp, ns, nh, dk, dv, q0, k0, v0, r0):
    rb = SUBLANES
    base = tp // rb
    qk_w = nh * dk
    v_w = nh * dv
    kern = functools.partial(_gla_sample_kernel, nh=nh, dk=dk, dv=dv)
    return pl.pallas_call(
        kern,
        grid=(ns,),
        in_specs=[
            pl.BlockSpec((rb, qk_w), lambda n: (base + n // rb, q0 // qk_w)),
            pl.BlockSpec((rb, qk_w), lambda n: (base + n // rb, k0 // qk_w)),
            pl.BlockSpec((rb, v_w), lambda n: (base + n // rb, v0 // v_w)),
            pl.BlockSpec((rb, v_w), lambda n: (base + n // rb, r0 // v_w)),
            pl.BlockSpec((rb, LANES), lambda n: (base + n // rb, 0)),
            pl.BlockSpec((LANES, qk_w), lambda n: (0, 0)),
            pl.BlockSpec((1, qk_w), lambda n: (0, 0)),
            pl.BlockSpec((1, v_w), lambda n: (0, 0)),
            pl.BlockSpec((None, nh, dk, dv), lambda n: (n, 0, 0, 0)),
        ],
        out_specs=[
            pl.BlockSpec((rb, v_w), lambda n: (n // rb, 0)),
            pl.BlockSpec((None, nh, dk, dv), lambda n: (n, 0, 0, 0)),
        ],
        out_shape=[
            jax.ShapeDtypeStruct((ns, v_w), F32),
            jax.ShapeDtypeStruct((ns, nh, dk, dv), F32),
        ],
        compiler_params=_cparams("arbitrary"),
        name="gla_sample",
    )(proj, proj, proj, proj, xg, w2, b2, g, s0)


def _out_proj_kernel(ys_ref, ogp_ref, ogt_ref, w1_ref, w2_ref, xp_ref, xt_ref, h_ref,
                     *, nfull, split):
    i = pl.program_id(0)
    w2 = w2_ref[...]
    acc = jnp.dot(ys_ref[...], w1_ref[...], preferred_element_type=F32)

    @pl.when(i < nfull)
    def _():
        h_ref[...] = xp_ref[...] + acc + jnp.dot(ogp_ref[...], w2, preferred_element_type=F32)

    @pl.when(i >= nfull)
    def _():
        if split:
            h_ref[0:split, :] = xp_ref[0:split, :] + acc[0:split] + jnp.dot(
                ogp_ref[0:split, :], w2, preferred_element_type=F32)
        h_ref[split:, :] = xt_ref[...] + acc[split:] + jnp.dot(
            ogt_ref[...], w2, preferred_element_type=F32)


def _out_proj(ys, og_p, og_t, w, x_p, x_t, tm, tn):
    t, dh = ys.shape
    d = w.shape[1]
    tp, tr = x_p.shape[0], x_t.shape[0]
    nfull = tp // tm
    split = tp - nfull * tm
    assert t == tp + tr and t == (nfull + 1) * tm and tm - split == tr
    last = pl.cdiv(tp, tm) - 1
    return pl.pallas_call(
        functools.partial(_out_proj_kernel, nfull=nfull, split=split),
        grid=(t // tm, d // tn),
        in_specs=[
            pl.BlockSpec((tm, dh), lambda i, j: (i, 0)),
            pl.BlockSpec((tm, dh), lambda i, j: (jnp.minimum(i, last), 0)),
            pl.BlockSpec((tr, dh), lambda i, j: (0, 0)),
            pl.BlockSpec((dh, tn), lambda i, j: (0, j)),
            pl.BlockSpec((dh, tn), lambda i, j: (1, j)),
            pl.BlockSpec((tm, tn), lambda i, j: (jnp.minimum(i, last), j)),
            pl.BlockSpec((tr, tn), lambda i, j: (0, j)),
        ],
        out_specs=pl.BlockSpec((tm, tn), lambda i, j: (i, j)),
        out_shape=jax.ShapeDtypeStruct((t, d), F32),
        compiler_params=_cparams("parallel", "parallel"),
        name="out_proj",
    )(ys, og_p, og_t, w, w, x_p, x_t)


def _norm_t_kernel(h_ref, g_ref, o_ref):
    o_ref[...] = _rms(h_ref[...], g_ref[...]).T.astype(BF16)


def _norm_t(h, g, tm):
    t, d = h.shape
    return pl.pallas_call(
        _norm_t_kernel,
        grid=(t // tm,),
        in_specs=[pl.BlockSpec((tm, d), lambda i: (i, 0)), pl.BlockSpec((1, d), lambda i: (0, 0))],
        out_specs=pl.BlockSpec((d, tm), lambda i: (0, i)),
        out_shape=jax.ShapeDtypeStruct((d, t), BF16),
        compiler_params=_cparams("parallel"),
        name="ffn_norm_t",
    )(h, g)


def _topk_rows(x, k):
    outs = []
    for _ in range(k):
        m = jnp.max(x, axis=0, keepdims=True)
        outs.append(m)
        x = jnp.where(x == m, -jnp.inf, x)
    return outs


def _route_kernel(wq_ref, hn_ref, k1_ref, k2_ref, side_ref,
                  s1_ref, c1_ref, s2_ref, e2_ref, tau_ref, side_o_ref, *, nside):
    @pl.when(pl.program_id(0) * pl.num_programs(1) + pl.program_id(1) < nside)
    def _():
        _transpose_block(side_ref, side_o_ref)

    half = k1_ref.shape[1]
    tb = hn_ref.shape[1]
    qt = jnp.dot(wq_ref[...], hn_ref[...], preferred_element_type=F32)
    s1 = jnp.dot(k1_ref[...].astype(BF16), qt[:half, :].astype(BF16), preferred_element_type=F32)
    s2 = jnp.dot(k2_ref[...].astype(BF16), qt[half:, :].astype(BF16), preferred_element_type=F32)
    s1_ref[...] = s1
    s2_ref[...] = s2
    for t in range(tb // LANES):
        sl = slice(t * LANES, (t + 1) * LANES)
        a = s1[:, sl]
        b = s2[:, sl]
        kk = PEER_TOPK
        v1 = _topk_rows(a, kk)
        v2l = _topk_rows(b, kk)
        v2 = jnp.concatenate(v2l, axis=0)
        cand = jnp.concatenate(
            [v1[0] + v2]
            + [v1[i] + v2[:kk // 2] for i in range(1, kk // 2)]
            + [jnp.concatenate(v1[kk // 2:], axis=0) + v2l[0]], axis=0)
        top = _topk_rows(cand, kk)
        z = jnp.zeros_like(top[0])
        for c in top:
            z = z + jnp.exp(c - top[0])
        tau_ref[:, sl] = top[PEER_TOPK - 1]
        c1_ref[:, sl] = jnp.exp(a - v1[0]) / z
        e2_ref[:, sl] = jnp.exp(b - v2[0:1, :])


def _route(wq_t, hn_t, k1, k2, side, tb):
    d, t = hn_t.shape
    nh, nk, half = k1.shape
    qd = 2 * half
    ni = t // tb
    rs, cs = side.shape
    nside = max(k for k in range(1, ni * nh + 1) if rs % k == 0 and (rs // k) % MXU_DEPTH == 0)
    rb = rs // nside
    tab = jax.ShapeDtypeStruct((nh, nk, t), F32)
    tab_spec = pl.BlockSpec((None, nk, tb), lambda i, h: (h, 0, i))
    return pl.pallas_call(
        functools.partial(_route_kernel, nside=nside),
        grid=(ni, nh),
        in_specs=[
            pl.BlockSpec((qd, d), lambda i, h: (h, 0)),
            pl.BlockSpec((d, tb), lambda i, h: (0, i)),
            pl.BlockSpec((None, nk, half), lambda i, h: (h, 0, 0)),
            pl.BlockSpec((None, nk, half), lambda i, h: (h, 0, 0)),
            pl.BlockSpec((rb, cs), lambda i, h: (jnp.minimum(i * nh + h, nside - 1), 0)),
        ],
        out_specs=[tab_spec, tab_spec, tab_spec, tab_spec,
                   pl.BlockSpec((None, 1, tb), lambda i, h: (h, 0, i)),
                   pl.BlockSpec((cs, rb), lambda i, h: (0, jnp.minimum(i * nh + h, nside - 1)))],
        out_shape=[tab, tab, tab, tab, jax.ShapeDtypeStruct((nh, 1, t), F32),
                   jax.ShapeDtypeStruct((cs, rs), BF16)],
        compiler_params=_cparams("arbitrary", "arbitrary"),
        name="peer_route",
    )(wq_t, hn_t, k1, k2, side)


GATE_ROWS = 32
OUT_ROWS = 512


K_CHUNK = 512
EXPERT_BLOCK = 512


def _peer_kernel(u_ref, vt_ref, hn_ref, s1_ref, c1_ref, s2_ref, e2_ref, tau_ref, o_ref,
                 w_scr, act_scr, a_scr):
    eb, d = u_ref.shape
    tb = hn_ref.shape[1]
    nh, nk, _ = s2_ref.shape
    nr = eb // nk

    @pl.when(pl.program_id(1) == 0)
    def _():
        o_ref[...] = jnp.zeros_like(o_ref)

    def gate_group(t, q):
        sl = slice(t * LANES, (t + 1) * LANES)
        rows = slice(q * GATE_ROWS, (q + 1) * GATE_ROWS)
        accs = [None] * nr
        for h in range(nh):
            s2t = s2_ref[h, rows, sl]
            e2t = e2_ref[h, rows, sl]
            tau = tau_ref[h, :, sl]
            for r in range(nr):
                ssum = s1_ref[r, h:h + 1, sl] + s2t
                term = jnp.where(ssum >= tau, c1_ref[r, h:h + 1, sl] * e2t, 0.0)
                accs[r] = term if accs[r] is None else accs[r] + term
        for r in range(nr):
            lo = r * nk + q * GATE_ROWS
            w_scr[lo:lo + GATE_ROWS, sl] = accs[r]

    groups = [(t, q) for t in range(tb // LANES) for q in range(nk // GATE_ROWS)]
    nkc = d // K_CHUNK
    per = -(-len(groups) // nkc)
    for kc in range(nkc):
        @pl.when(pl.program_id(1) >= 0)
        def _(kc=kc):
            ks = slice(kc * K_CHUNK, (kc + 1) * K_CHUNK)
            part = jnp.dot(u_ref[:, ks], hn_ref[ks, :], preferred_element_type=F32)
            if kc == 0:
                act_scr[...] = part
            else:
                act_scr[...] += part
            for t, q in groups[kc * per:(kc + 1) * per]:
                gate_group(t, q)
    a_scr[...] = (w_scr[...] * _gelu(act_scr[...])).astype(BF16)
    for dc in range(d // OUT_ROWS):
        dr = slice(dc * OUT_ROWS, (dc + 1) * OUT_ROWS)
        o_ref[dr, :] += jnp.dot(vt_ref[dr, :], a_scr[...], preferred_element_type=F32)


def _peer(u_bf, v_t, hn_t, s1, c1, s2, e2, tau, tb, eb):
    ne, d = u_bf.shape
    t = hn_t.shape[1]
    nh, nk, _ = s2.shape
    once = pl.Buffered(1)
    tab_spec = pl.BlockSpec((nh, nk, tb), lambda i, e: (0, 0, i), pipeline_mode=once)
    row_spec = pl.BlockSpec((eb // nk, nh, tb), lambda i, e: (e, 0, i))
    return pl.pallas_call(
        _peer_kernel,
        grid=(t // tb, ne // eb),
        in_specs=[
            pl.BlockSpec((eb, d), lambda i, e: (e, 0)),
            pl.BlockSpec((d, eb), lambda i, e: (0, e)),
            pl.BlockSpec((d, tb), lambda i, e: (0, i), pipeline_mode=once),
            row_spec, row_spec, tab_spec, tab_spec,
            pl.BlockSpec((nh, 1, tb), lambda i, e: (0, 0, i)),
        ],
        out_specs=pl.BlockSpec((d, tb), lambda i, e: (0, i)),
        out_shape=jax.ShapeDtypeStruct((d, t), F32),
        scratch_shapes=[pltpu.VMEM((eb, tb), F32), pltpu.VMEM((eb, tb), F32),
                        pltpu.VMEM((eb, tb), BF16)],
        compiler_params=pltpu.CompilerParams(
            dimension_semantics=("parallel", "arbitrary"), vmem_limit_bytes=PEER_VMEM_LIMIT),
        name="peer_experts",
    )(u_bf, v_t, hn_t, s1, c1, s2, e2, tau)


def _final_kernel(pt_ref, h_ref, g_ref, yp_ref, yt_ref, *, nprompt):
    i = pl.program_id(0)
    y = _rms(h_ref[...] + pt_ref[...].T, g_ref[...])

    @pl.when(i < nprompt)
    def _():
        yp_ref[...] = y

    @pl.when(i >= nprompt)
    def _():
        yt_ref[...] = y


def _final(peer_t, h, g, tp, tr):
    d, t = peer_t.shape
    nprompt = tp // tr
    return pl.pallas_call(
        functools.partial(_final_kernel, nprompt=nprompt),
        grid=(nprompt + 1,),
        in_specs=[
            pl.BlockSpec((d, tr), lambda i: (0, i)),
            pl.BlockSpec((tr, d), lambda i: (i, 0)),
            pl.BlockSpec((1, d), lambda i: (0, 0)),
        ],
        out_specs=[
            pl.BlockSpec((tr, d), lambda i: (jnp.minimum(i, nprompt - 1), 0)),
            pl.BlockSpec((tr, d), lambda i: (0, 0)),
        ],
        out_shape=[jax.ShapeDtypeStruct((tp, d), F32), jax.ShapeDtypeStruct((tr, d), F32)],
        compiler_params=_cparams("arbitrary"),
        name="final_norm",
    )(peer_t, h, g)


def _block_diag(w, ngrp):
    nslab, _, a, b = w.shape
    eye = jnp.eye(ngrp, dtype=w.dtype)
    full = w[:, :, :, None, :] * eye[None, :, None, :, None]
    return full.reshape(nslab, ngrp * a, ngrp * b)


def kernel(x_prompt, x_sample, state_s5_re, state_s5_im, state_gla, meta_tokens, norm_mix_g, w_in, s5_lam_re, s5_lam_im, s5_log_dt, s5_b_re, s5_b_im, s5_c_re, s5_c_im, s5_d, s5_w_glu, s5_b_glu, s5_norm_g, gla_w_gate2, gla_b_gate2, gla_norm_g, w_out, norm_ffn_g, peer_w_q, peer_keys, peer_u, peer_v, norm_final_g):
    nb, seq, d = x_prompt.shape
    ns = x_sample.shape[0]
    nm = meta_tokens.shape[0]
    depth = w_in.shape[0]
    assert depth == 1 and x_sample.shape[1] == 1
    tp = nb * seq
    assert tp % TAIL_ROWS == 0 and ns + nm <= TAIL_ROWS and ns % SUBLANES == 0
    assert seq % GLA_CHUNK == 0 and nm % SUBLANES == 0
    t_all = tp + TAIL_ROWS

    ngrp, nstate, gch = s5_b_re.shape[1:]
    d_ssm = ngrp * gch
    gps = LANES // gch
    nslab = ngrp // gps
    nh, dk, dv = state_gla.shape[2:]
    d_gla = nh * dv
    rank = gla_w_gate2.shape[1]
    q0 = d_ssm
    k0 = q0 + nh * dk
    v0 = k0 + nh * dk
    r0 = v0 + d_gla
    g0 = r0 + d_gla
    assert g0 + rank == w_in.shape[2] and d_ssm + d_gla == w_out.shape[1]

    x_p = x_prompt.reshape(tp, d)
    x_t = jnp.concatenate([x_sample.reshape(ns, d), meta_tokens,
                           jnp.zeros((TAIL_ROWS - ns - nm, d), F32)], axis=0)
    w_in_bf = w_in[0].astype(BF16)
    w_g1 = jnp.pad(w_in[0, :, g0:], ((0, 0), (0, LANES - rank))).astype(BF16)
    w_g2 = jnp.pad(gla_w_gate2[0], ((0, LANES - rank), (0, 0))).astype(BF16)
    lam_re = s5_lam_re[0].reshape(1, ngrp * nstate)
    lam_im = s5_lam_im[0].reshape(1, ngrp * nstate)
    logdt = jnp.repeat(s5_log_dt[0], nstate).reshape(1, ngrp * nstate)
    b_re4 = jnp.transpose(s5_b_re[0].reshape(nslab, gps, nstate, gch), (0, 1, 3, 2))
    b_im4 = jnp.transpose(s5_b_im[0].reshape(nslab, gps, nstate, gch), (0, 1, 3, 2))
    c_re4 = jnp.transpose(s5_c_re[0].reshape(nslab, gps, gch, nstate), (0, 1, 3, 2))
    c_im4 = jnp.transpose(s5_c_im[0].reshape(nslab, gps, gch, nstate), (0, 1, 3, 2))
    bre_bd = _block_diag(b_re4, gps)
    bim_bd = _block_diag(b_im4, gps)
    cre_bd = _block_diag(c_re4, gps)
    cim_bd = _block_diag(c_im4, gps)
    h0r = state_s5_re[0].reshape(ns, ngrp * nstate)
    h0i = state_s5_im[0].reshape(ns, ngrp * nstate)

    tm = _row_tile(t_all, (768, 512, 256))

    xn = _prenorm(x_p, x_t, norm_mix_g)
    proj, xg, u_bf = _in_proj(xn, w_in_bf, w_g1, peer_u[0], tm, PROJ_COLS, g0)

    y_raw, pr, pi_, sr, si = _s5(proj, h0r, h0i, lam_re, lam_im, logdt, bre_bd, bim_bd,
                                 cre_bd, cim_bd, s5_d, nb, seq, ns, nm)
    y_ssm = _glu_norm(y_raw, s5_w_glu[0].astype(BF16), s5_b_glu, s5_norm_g, TAIL_ROWS)

    og_p, gla_p = _gla_prompt(proj, xg, w_g2, gla_b_gate2, gla_norm_g, nb, seq, ns, nm,
                              nh, dk, dv, q0, k0, v0, r0)
    og_s, gla_s = _gla_sample(proj, xg, w_g2, gla_b_gate2, gla_norm_g, state_gla[0],
                              tp, ns, nh, dk, dv, q0, k0, v0, r0)
    og_t = jnp.concatenate([og_s.astype(BF16), jnp.zeros((TAIL_ROWS - ns, d_gla), BF16)], axis=0)

    w_o = w_out[0].astype(BF16)
    assert d_ssm == d_gla
    h = _out_proj(y_ssm, og_p, og_t, w_o, x_p, x_t, tm, PROJ_COLS)

    hn_t = _norm_t(h, norm_ffn_g, TAIL_ROWS)
    wq_t = _transpose_cast(peer_w_q[0], WT_ROWS, WT_COLS)
    s1, c1, s2, e2, tau, v_t = _route(wq_t, hn_t, peer_keys[0, :, 0], peer_keys[0, :, 1],
                                      peer_v[0], tm)
    peer_t = _peer(u_bf, v_t, hn_t, jnp.transpose(s1, (1, 0, 2)), jnp.transpose(c1, (1, 0, 2)),
                   s2, e2, tau, tm, EXPERT_BLOCK)
    y_p, y_t = _final(peer_t, h, norm_final_g.reshape(1, d), tp, TAIL_ROWS)

    y_prompt = y_p.reshape(nb, seq, d)
    y_sample = y_t[:ns].reshape(ns, 1, d)
    return (y_prompt, y_sample,
            pr.reshape(1, nb, ngrp, nstate), pi_.reshape(1, nb, ngrp, nstate), gla_p[None],
            sr.reshape(1, ns, ngrp, nstate), si.reshape(1, ns, ngrp, nstate), gla_s[None])
```

```python
import functools

import jax
import jax.numpy as jnp
from jax import lax
from jax.experimental import pallas as pl
from jax.experimental.pallas import tpu as pltpu

F32 = jnp.float32
BF16 = jnp.bfloat16

EPS = 1e-6
GLA_TAU = 16.0
GLA_CHUNK = 64
GLA_SUB = 16
GLA_SAMPLES_PER_STEP = 2
PEER_TOPK = 16
LANES = 128
SUBLANES = 8
TAIL_ROWS = 256
PROJ_COLS = 1024
MXU_DEPTH = 256
WT_ROWS = 2048
WT_COLS = 1024
VMEM_LIMIT = 56 * 1024 * 1024
PEER_VMEM_LIMIT = 60 * 1024 * 1024


def _cparams(*sem):
    return pltpu.CompilerParams(dimension_semantics=sem, vmem_limit_bytes=VMEM_LIMIT)


def _gelu(x):
    return 0.5 * x * (1.0 + jnp.tanh(0.7978845608028654 * (x + 0.044715 * (x * x * x))))


def _sigmoid(x):
    return 1.0 / (1.0 + jnp.exp(-x))


def _log_sigmoid(x):
    return jnp.minimum(x, 0.0) - jnp.log(1.0 + jnp.exp(-jnp.abs(x)))


def _row_tile(n, cands):
    for c in cands:
        if n % c == 0:
            return c
    raise ValueError(f"no row tile for {n}")


def _transpose_block(x_ref, o_ref):
    k = MXU_DEPTH
    eye = jnp.where(lax.broadcasted_iota(jnp.int32, (k, k), 0)
                    == lax.broadcasted_iota(jnp.int32, (k, k), 1), 1.0, 0.0).astype(BF16)
    for c in range(x_ref.shape[1] // k):
        xb = x_ref[:, c * k:(c + 1) * k].astype(BF16)
        o_ref[c * k:(c + 1) * k, :] = lax.dot_general(
            eye, xb, (((1,), (1,)), ((), ())), preferred_element_type=F32).astype(BF16)


def _transpose_cast_kernel(x_ref, o_ref):
    _transpose_block(x_ref, o_ref)


def _transpose_cast(x, rb, cb):
    r, c = x.shape
    return pl.pallas_call(
        _transpose_cast_kernel,
        grid=(r // rb, c // cb),
        in_specs=[pl.BlockSpec((rb, cb), lambda i, j: (i, j))],
        out_specs=pl.BlockSpec((cb, rb), lambda i, j: (j, i)),
        out_shape=jax.ShapeDtypeStruct((c, r), BF16),
        compiler_params=_cparams("parallel", "parallel"),
        name="transpose_cast",
    )(x)


def _rms(x, g):
    return x * lax.rsqrt(jnp.mean(x * x, axis=-1, keepdims=True) + EPS) * g


def _prenorm_kernel(xp_ref, xt_ref, g_ref, o_ref, *, nprompt):
    i = pl.program_id(0)

    @pl.when(i < nprompt)
    def _():
        o_ref[...] = _rms(xp_ref[...], g_ref[...]).astype(BF16)

    @pl.when(i >= nprompt)
    def _():
        o_ref[...] = _rms(xt_ref[...], g_ref[...]).astype(BF16)


def _prenorm(x_p, x_t, g):
    tp, d = x_p.shape
    tr = x_t.shape[0]
    nprompt = tp // tr
    return pl.pallas_call(
        functools.partial(_prenorm_kernel, nprompt=nprompt),
        grid=(nprompt + 1,),
        in_specs=[
            pl.BlockSpec((tr, d), lambda i: (jnp.minimum(i, nprompt - 1), 0)),
            pl.BlockSpec((tr, d), lambda i: (0, 0)),
            pl.BlockSpec((1, d), lambda i: (0, 0)),
        ],
        out_specs=pl.BlockSpec((tr, d), lambda i: (i, 0)),
        out_shape=jax.ShapeDtypeStruct((tp + tr, d), BF16),
        compiler_params=_cparams("arbitrary"),
        name="mix_norm",
    )(x_p, x_t, g)


def _in_proj_kernel(xn_ref, w_ref, wg_ref, side_ref, o_ref, og_ref, side_o_ref, *, nside):
    nj = pl.num_programs(1)

    @pl.when(pl.program_id(1) == 0)
    def _():
        og_ref[...] = jnp.dot(xn_ref[...], wg_ref[...], preferred_element_type=F32)

    o_ref[...] = jnp.dot(xn_ref[...], w_ref[...], preferred_element_type=F32)

    @pl.when(pl.program_id(0) * nj + pl.program_id(1) < nside)
    def _():
        side_o_ref[...] = side_ref[...].astype(BF16)


def _in_proj(xn, w, wg, side, tm, tn, n):
    t, d = xn.shape
    ni, nj = t // tm, n // tn
    rs, cs = side.shape
    nside = max(k for k in range(1, ni * nj + 1) if rs % k == 0 and (rs // k) % SUBLANES == 0)
    rb = rs // nside
    side_idx = lambda i, j: (jnp.minimum(i * nj + j, nside - 1), 0)
    return pl.pallas_call(
        functools.partial(_in_proj_kernel, nside=nside),
        grid=(ni, nj),
        in_specs=[
            pl.BlockSpec((tm, d), lambda i, j: (i, 0)),
            pl.BlockSpec((d, tn), lambda i, j: (0, j)),
            pl.BlockSpec((d, LANES), lambda i, j: (0, 0)),
            pl.BlockSpec((rb, cs), side_idx),
        ],
        out_specs=[
            pl.BlockSpec((tm, tn), lambda i, j: (i, j)),
            pl.BlockSpec((tm, LANES), lambda i, j: (i, 0)),
            pl.BlockSpec((rb, cs), side_idx),
        ],
        out_shape=[jax.ShapeDtypeStruct((t, n), F32), jax.ShapeDtypeStruct((t, LANES), F32),
                   jax.ShapeDtypeStruct((rs, cs), BF16)],
        compiler_params=_cparams("arbitrary", "arbitrary"),
        name="in_proj",
    )(xn, w, wg, side)


def _s5_kernel(u_ref, h0r_ref, h0i_ref, lr_ref, li_ref, ldt_ref, bre_ref, bim_ref,
               cre_ref, cim_ref, d_ref,
               y_ref, pr_ref, pi_ref, sr_ref, si_ref,
               x_scr, init_scr, up_scr, yp_scr, *, nb, seq, ns, nm):
    sw = lr_ref.shape[1]
    tp = nb * seq
    nseg = SUBLANES
    ls = seq // nseg

    lr = lr_ref[...]
    li = li_ref[...]
    dt = jnp.exp(ldt_ref[...])
    mag = jnp.exp(lr * dt)
    ang = li * dt
    ar = mag * jnp.cos(ang)
    ai = mag * jnp.sin(ang)
    den = lr * lr + li * li
    nr = ar - 1.0
    qr = (nr * lr + ai * li) / den
    qi = (ai * lr - nr * li) / den
    bre = bre_ref[...]
    bim = bim_ref[...]
    bcat = jnp.concatenate([qr * bre - qi * bim, qr * bim + qi * bre], axis=1).astype(BF16)
    ccat = jnp.concatenate([cre_ref[...], -cim_ref[...]], axis=0).astype(BF16)
    dvec = d_ref[...]
    mag_s = jnp.exp(lr * dt * float(ls))
    asr = mag_s * jnp.cos(ang * float(ls))
    asi = mag_s * jnp.sin(ang * float(ls))

    def project_out(h, u):
        return jnp.dot(h.astype(BF16), ccat, preferred_element_type=F32) + dvec * u

    u_s = u_ref[tp:tp + ns, :]
    x_s = jnp.dot(u_s.astype(BF16), bcat, preferred_element_type=F32)
    h0r = h0r_ref[...]
    h0i = h0i_ref[...]
    hr_s = ar * h0r - ai * h0i + x_s[:, :sw]
    hi_s = ar * h0i + ai * h0r + x_s[:, sw:]
    sr_ref[...] = hr_s
    si_ref[...] = hi_s
    y_ref[tp:tp + ns, :] = project_out(jnp.concatenate([hr_s, hi_s], axis=1), u_s)
    y_ref[tp + ns:, :] = jnp.zeros((y_ref.shape[0] - tp - ns, y_ref.shape[1]), F32)

    u_m = u_ref[tp + ns:tp + ns + nm, :]
    x_m = jnp.dot(u_m.astype(BF16), bcat, preferred_element_type=F32)
    mr = jnp.zeros((1, sw), F32)
    mi = jnp.zeros((1, sw), F32)
    for t in range(nm):
        mr, mi = (ar * mr - ai * mi + x_m[t:t + 1, :sw],
                  ar * mi + ai * mr + x_m[t:t + 1, sw:])

    ar8 = jnp.broadcast_to(ar, (nseg, sw))
    ai8 = jnp.broadcast_to(ai, (nseg, sw))
    unroll = 4 if ls % 4 == 0 else 1

    def scan_pass(store):
        def body(i, c):
            hr, hi = c
            r = pl.multiple_of(i * nseg, nseg)
            nhr = ar8 * hr - ai8 * hi + x_scr[pl.ds(r, nseg), 0:sw]
            nhi = ar8 * hi + ai8 * hr + x_scr[pl.ds(r, nseg), sw:2 * sw]
            if store:
                x_scr[pl.ds(r, nseg), 0:sw] = nhr
                x_scr[pl.ds(r, nseg), sw:2 * sw] = nhi
            return nhr, nhi
        return body

    for b in range(nb):
        def regroup_in(i, c, b=b):
            r = pl.multiple_of(i * nseg, nseg)
            up_scr[pl.ds(r, nseg), :] = u_ref[pl.ds(b * seq + i, nseg, stride=ls), :]
            return c
        lax.fori_loop(0, ls, regroup_in, 0, unroll=unroll)
        u_b = up_scr[...]
        x_scr[...] = jnp.dot(u_b.astype(BF16), bcat, preferred_element_type=F32)
        z = jnp.zeros((nseg, sw), F32)
        fr, fi = lax.fori_loop(0, ls, scan_pass(False), (z, z), unroll=unroll)
        cr, ci = mr, mi
        for k in range(nseg):
            init_scr[k:k + 1, 0:sw] = cr
            init_scr[k:k + 1, sw:2 * sw] = ci
            cr, ci = (asr * cr - asi * ci + fr[k:k + 1, :],
                      asr * ci + asi * cr + fi[k:k + 1, :])
        pr_ref[b:b + 1, :] = cr
        pi_ref[b:b + 1, :] = ci
        lax.fori_loop(0, ls, scan_pass(True), (init_scr[:, 0:sw], init_scr[:, sw:2 * sw]),
                      unroll=unroll)
        yp_scr[...] = project_out(x_scr[...], u_b)

        def regroup_out(i, c, b=b):
            r = pl.multiple_of(i * nseg, nseg)
            y_ref[pl.ds(b * seq + i, nseg, stride=ls), :] = yp_scr[pl.ds(r, nseg), :]
            return c
        lax.fori_loop(0, ls, regroup_out, 0, unroll=unroll)


def _s5(proj, h0r, h0i, lam_re, lam_im, logdt, bre_bd, bim_bd, cre_bd, cim_bd, dvec,
        nb, seq, ns, nm):
    t_all = proj.shape[0]
    nslab, cw, sw = bre_bd.shape
    d_ssm = nslab * cw
    kern = functools.partial(_s5_kernel, nb=nb, seq=seq, ns=ns, nm=nm)
    return pl.pallas_call(
        kern,
        grid=(nslab,),
        in_specs=[
            pl.BlockSpec((t_all, cw), lambda s: (0, s)),
            pl.BlockSpec((ns, sw), lambda s: (0, s)),
            pl.BlockSpec((ns, sw), lambda s: (0, s)),
            pl.BlockSpec((1, sw), lambda s: (0, s)),
            pl.BlockSpec((1, sw), lambda s: (0, s)),
            pl.BlockSpec((1, sw), lambda s: (0, s)),
            pl.BlockSpec((None, cw, sw), lambda s: (s, 0, 0)),
            pl.BlockSpec((None, cw, sw), lambda s: (s, 0, 0)),
            pl.BlockSpec((None, sw, cw), lambda s: (s, 0, 0)),
            pl.BlockSpec((None, sw, cw), lambda s: (s, 0, 0)),
            pl.BlockSpec((1, cw), lambda s: (0, s)),
        ],
        out_specs=[
            pl.BlockSpec((t_all, cw), lambda s: (0, s)),
            pl.BlockSpec((nb, sw), lambda s: (0, s)),
            pl.BlockSpec((nb, sw), lambda s: (0, s)),
            pl.BlockSpec((ns, sw), lambda s: (0, s)),
            pl.BlockSpec((ns, sw), lambda s: (0, s)),
        ],
        out_shape=[
            jax.ShapeDtypeStruct((t_all, d_ssm), F32),
            jax.ShapeDtypeStruct((nb, nslab * sw), F32),
            jax.ShapeDtypeStruct((nb, nslab * sw), F32),
            jax.ShapeDtypeStruct((ns, nslab * sw), F32),
            jax.ShapeDtypeStruct((ns, nslab * sw), F32),
        ],
        scratch_shapes=[pltpu.VMEM((seq, 2 * sw), F32), pltpu.VMEM((SUBLANES, 2 * sw), F32),
                        pltpu.VMEM((seq, cw), F32), pltpu.VMEM((seq, cw), F32)],
        compiler_params=_cparams("parallel"),
        name="s5_scan",
    )(proj, h0r, h0i, lam_re, lam_im, logdt, bre_bd, bim_bd, cre_bd, cim_bd, dvec)


def _glu_norm_kernel(y_ref, w_ref, b_ref, g_ref, o_ref):
    z = _gelu(y_ref[...])
    gate = jnp.dot(z.astype(BF16), w_ref[...], preferred_element_type=F32) + b_ref[...]
    zz = z * _sigmoid(gate)
    s = lax.rsqrt(jnp.mean(zz * zz, axis=-1, keepdims=True) + EPS)
    o_ref[...] = (zz * s * g_ref[...]).astype(o_ref.dtype)


def _glu_norm(y, w, b, g, tm):
    t, d = y.shape
    return pl.pallas_call(
        _glu_norm_kernel,
        grid=(t // tm,),
        in_specs=[
            pl.BlockSpec((tm, d), lambda i: (i, 0)),
            pl.BlockSpec((d, d), lambda i: (0, 0)),
            pl.BlockSpec((1, d), lambda i: (0, 0)),
            pl.BlockSpec((1, d), lambda i: (0, 0)),
        ],
        out_specs=pl.BlockSpec((tm, d), lambda i: (i, 0)),
        out_shape=jax.ShapeDtypeStruct((t, d), BF16),
        compiler_params=_cparams("parallel"),
        name="s5_glu_norm",
    )(y, w, b, g)


def _cumsum_rows(x):
    n = x.shape[0]
    row = lax.broadcasted_iota(jnp.int32, x.shape, 0)
    s = 1
    while s < n:
        x = x + jnp.where(row >= s, pltpu.roll(x, s, axis=0), 0.0)
        s *= 2
    return x


def _gate(xg, w2, b2):
    pre = jnp.dot(xg.astype(BF16), w2, preferred_element_type=F32) + b2
    return _log_sigmoid(pre) * (1.0 / GLA_TAU)


def _head_out(o, r, g):
    o = o * lax.rsqrt(jnp.mean(o * o, axis=-1, keepdims=True) + EPS)
    return o * g * (r * _sigmoid(r))


def _gla_prompt_kernel(q_ref, k_ref, v_ref, r_ref, xg_ref, kt_ref, vt_ref, xgt_ref,
                       w2_ref, b2_ref, g_ref, o_ref, s_ref, st_scr, *, seq, ns, nm):
    dk = q_ref.shape[1]
    scale = float(dk) ** -0.5
    w2 = w2_ref[...]
    b2 = b2_ref[...]
    g = g_ref[...]

    lgm = _gate(xgt_ref[ns:ns + nm, :], w2, b2)
    bm = _cumsum_rows(lgm)
    kdm = kt_ref[ns:ns + nm, :] * jnp.exp(bm[nm - 1:nm, :] - bm)
    st_scr[...] = lax.dot_general(vt_ref[ns:ns + nm, :].astype(BF16), kdm.astype(BF16),
                                  (((0,), (0,)), ((), ())), preferred_element_type=F32)

    c = GLA_CHUNK
    nsub = c // GLA_SUB

    def chunk(ci, carry):
        r0 = pl.multiple_of(ci * c, c)
        q = q_ref[pl.ds(r0, c), :] * scale
        k = k_ref[pl.ds(r0, c), :]
        v = v_ref[pl.ds(r0, c), :]
        vb = v.astype(BF16)
        lg = _gate(xg_ref[pl.ds(r0, c), :], w2, b2)
        b = _cumsum_rows(lg)
        bl = b[c - 1:c, :]
        st = st_scr[...]
        o_inter = lax.dot_general((q * jnp.exp(b)).astype(BF16), st.astype(BF16),
                                  (((1,), (1,)), ((), ())), preferred_element_type=F32)
        outs = []
        for sb in range(nsub):
            lo = sb * GLA_SUB
            hi = lo + GLA_SUB
            beta = b[lo - 1:lo, :] if sb > 0 else jnp.zeros((1, dk), F32)
            qs = q[lo:hi, :] * jnp.exp(b[lo:hi, :] - beta)
            ks = k[0:hi, :] * jnp.exp(beta - b[0:hi, :])
            sc = lax.dot_general(qs.astype(BF16), ks.astype(BF16),
                                 (((1,), (1,)), ((), ())), preferred_element_type=F32)
            rowi = lax.broadcasted_iota(jnp.int32, (GLA_SUB, hi), 0)
            coli = lax.broadcasted_iota(jnp.int32, (GLA_SUB, hi), 1)
            sc = jnp.where(coli <= rowi + lo, sc, 0.0)
            outs.append(jnp.dot(sc.astype(BF16), vb[0:hi, :], preferred_element_type=F32))
        o = o_inter + jnp.concatenate(outs, axis=0)
        o_ref[pl.ds(r0, c), :] = _head_out(o, r_ref[pl.ds(r0, c), :], g).astype(o_ref.dtype)
        kd = k * jnp.exp(bl - b)
        st_scr[...] = jnp.exp(bl) * st + lax.dot_general(
            vb, kd.astype(BF16), (((0,), (0,)), ((), ())), preferred_element_type=F32)
        return carry

    lax.fori_loop(0, seq // c, chunk, 0, unroll=min(8, seq // c))
    s_ref[...] = st_scr[...].T


def _gla_prompt(proj, xg, w2, b2, g, nb, seq, ns, nm, nh, dk, dv, q0, k0, v0, r0):
    tp = nb * seq
    tail_blk = tp // TAIL_ROWS
    rb = seq
    kern = functools.partial(_gla_prompt_kernel, seq=seq, ns=ns, nm=nm)
    return pl.pallas_call(
        kern,
        grid=(nb, nh),
        in_specs=[
            pl.BlockSpec((rb, dk), lambda b, h: (b, q0 // dk + h)),
            pl.BlockSpec((rb, dk), lambda b, h: (b, k0 // dk + h)),
            pl.BlockSpec((rb, dv), lambda b, h: (b, v0 // dv + h)),
            pl.BlockSpec((rb, dv), lambda b, h: (b, r0 // dv + h)),
            pl.BlockSpec((rb, LANES), lambda b, h: (b, 0)),
            pl.BlockSpec((TAIL_ROWS, dk), lambda b, h: (tail_blk, k0 // dk + h)),
            pl.BlockSpec((TAIL_ROWS, dv), lambda b, h: (tail_blk, v0 // dv + h)),
            pl.BlockSpec((TAIL_ROWS, LANES), lambda b, h: (tail_blk, 0)),
            pl.BlockSpec((LANES, dk), lambda b, h: (0, h)),
            pl.BlockSpec((1, dk), lambda b, h: (0, h)),
            pl.BlockSpec((1, dv), lambda b, h: (0, h)),
        ],
        out_specs=[
            pl.BlockSpec((rb, dv), lambda b, h: (b, h)),
            pl.BlockSpec((None, None, dk, dv), lambda b, h: (b, h, 0, 0)),
        ],
        out_shape=[
            jax.ShapeDtypeStruct((tp, nh * dv), BF16),
            jax.ShapeDtypeStruct((nb, nh, dk, dv), F32),
        ],
        scratch_shapes=[pltpu.VMEM((dv, dk), F32)],
        compiler_params=_cparams("parallel", "parallel"),
        name="gla_prompt",
    )(proj, proj, proj, proj, xg, proj, proj, xg, w2, b2, g)


def _gla_sample_kernel(q_ref, k_ref, v_ref, r_ref, xg_ref, w2_ref, b2_ref, g_ref, s0_ref,
                       o_ref, s_ref, *, nh, dk, dv, sps):
    for si in range(sps):
        _gla_sample_one(q_ref, k_ref, v_ref, r_ref, xg_ref, w2_ref, b2_ref, g_ref, s0_ref,
                        o_ref, s_ref, si, nh=nh, dk=dk, dv=dv, sps=sps)


def _gla_sample_one(q_ref, k_ref, v_ref, r_ref, xg_ref, w2_ref, b2_ref, g_ref, s0_ref,
                    o_ref, s_ref, si, *, nh, dk, dv, sps):
    n = pl.program_id(0) * sps + si
    j = n % SUBLANES
    scale = float(dk) ** -0.5
    lg_all = _gate(xg_ref[pl.ds(j, 1), :], w2_ref[...], b2_ref[...])
    q_all = q_ref[pl.ds(j, 1), :] * scale
    k_all = k_ref[pl.ds(j, 1), :]
    v_all = v_ref[pl.ds(j, 1), :]
    r_all = r_ref[pl.ds(j, 1), :]
    g_all = g_ref[...]
    rows = 2 * SUBLANES
    rk = lax.broadcasted_iota(jnp.int32, (rows, dk), 0)
    rv = lax.broadcasted_iota(jnp.int32, (rows, dv), 0)
    outs = []
    for h in range(nh):
        lg = lg_all[:, h * dk:(h + 1) * dk]
        q = q_all[:, h * dk:(h + 1) * dk]
        k = k_all[:, h * dk:(h + 1) * dk]
        v = v_all[:, h * dv:(h + 1) * dv]
        e = jnp.exp(lg)
        s0 = s0_ref[si, h]
        e_hi = e.astype(BF16)
        e_mid = (e - e_hi.astype(F32)).astype(BF16)
        e_lo = (e - e_hi.astype(F32) - e_mid.astype(F32)).astype(BF16)
        def rows_of(x, w):
            return jnp.broadcast_to(x.astype(F32), (rows, w))
        lhs = jnp.where(rk == 0, rows_of(e_hi, dk),
              jnp.where(rk == 1, rows_of(e_mid, dk),
              jnp.where(rk == 2, rows_of(e_lo, dk),
              jnp.where(rk == 3, rows_of(k, dk), 0.0)))).astype(BF16)
        ones_part = jnp.where(rv < 3, 1.0, 0.0)
        v_part = jnp.where(rv == 3, rows_of(v, dv), 0.0)
        rhs = jnp.concatenate([ones_part, v_part], axis=1).astype(BF16)
        both = lax.dot_general(lhs, rhs, (((0,), (0,)), ((), ())), preferred_element_type=F32)
        s_ref[si, h] = both[:, :dv] * s0 + both[:, dv:]
        qe = jnp.broadcast_to((q * e).astype(BF16), (rows, dk))
        o = jnp.dot(qe, s0.astype(BF16), preferred_element_type=F32)[0:1, :]
        o = o + jnp.sum(q * k, axis=-1, keepdims=True) * v
        outs.append(_head_out(o, r_all[:, h * dv:(h + 1) * dv], g_all[:, h * dv:(h + 1) * dv]))
    o_ref[pl.ds(j, 1), :] = jnp.concatenate(outs, axis=1)


def _gla_sample(proj, xg, w2, b2, g, s0, tp, ns, nh, dk, dv, q0, k0, v0, r0):
    rb = SUBLANES
    base = tp // rb
    qk_w = nh * dk
    v_w = nh * dv
    sps = GLA_SAMPLES_PER_STEP
    assert ns % sps == 0 and rb % sps == 0
    kern = functools.partial(_gla_sample_kernel, nh=nh, dk=dk, dv=dv, sps=sps)
    row = lambda n: base + (n * sps) // rb
    return pl.pallas_call(
        kern,
        grid=(ns // sps,),
        in_specs=[
            pl.BlockSpec((rb, qk_w), lambda n: (row(n), q0 // qk_w)),
            pl.BlockSpec((rb, qk_w), lambda n: (row(n), k0 // qk_w)),
            pl.BlockSpec((rb, v_w), lambda n: (row(n), v0 // v_w)),
            pl.BlockSpec((rb, v_w), lambda n: (row(n), r0 // v_w)),
            pl.BlockSpec((rb, LANES), lambda n: (row(n), 0)),
            pl.BlockSpec((LANES, qk_w), lambda n: (0, 0)),
            pl.BlockSpec((1, qk_w), lambda n: (0, 0)),
            pl.BlockSpec((1, v_w), lambda n: (0, 0)),
            pl.BlockSpec((sps, nh, dk, dv), lambda n: (n, 0, 0, 0)),
        ],
        out_specs=[
            pl.BlockSpec((rb, v_w), lambda n: ((n * sps) // rb, 0)),
            pl.BlockSpec((sps, nh, dk, dv), lambda n: (n, 0, 0, 0)),
        ],
        out_shape=[
            jax.ShapeDtypeStruct((ns, v_w), F32),
            jax.ShapeDtypeStruct((ns, nh, dk, dv), F32),
        ],
        compiler_params=_cparams("arbitrary"),
        name="gla_sample",
    )(proj, proj, proj, proj, xg, w2, b2, g, s0)


def _out_proj_kernel(ys_ref, ogp_ref, ogt_ref, w1_ref, w2_ref, xp_ref, xt_ref, h_ref,
                     *, nfull, split):
    i = pl.program_id(0)
    w2 = w2_ref[...]
    acc = jnp.dot(ys_ref[...], w1_ref[...], preferred_element_type=F32)

    @pl.when(i < nfull)
    def _():
        h_ref[...] = xp_ref[...] + acc + jnp.dot(ogp_ref[...], w2, preferred_element_type=F32)

    @pl.when(i >= nfull)
    def _():
        if split:
            h_ref[0:split, :] = xp_ref[0:split, :] + acc[0:split] + jnp.dot(
                ogp_ref[0:split, :], w2, preferred_element_type=F32)
        h_ref[split:, :] = xt_ref[...] + acc[split:] + jnp.dot(
            ogt_ref[...], w2, preferred_element_type=F32)


def _out_proj(ys, og_p, og_t, w, x_p, x_t, tm, tn):
    t, dh = ys.shape
    d = w.shape[1]
    tp, tr = x_p.shape[0], x_t.shape[0]
    nfull = tp // tm
    split = tp - nfull * tm
    assert t == tp + tr and t == (nfull + 1) * tm and tm - split == tr
    last = pl.cdiv(tp, tm) - 1
    return pl.pallas_call(
        functools.partial(_out_proj_kernel, nfull=nfull, split=split),
        grid=(t // tm, d // tn),
        in_specs=[
            pl.BlockSpec((tm, dh), lambda i, j: (i, 0)),
            pl.BlockSpec((tm, dh), lambda i, j: (jnp.minimum(i, last), 0)),
            pl.BlockSpec((tr, dh), lambda i, j: (0, 0)),
            pl.BlockSpec((dh, tn), lambda i, j: (0, j)),
            pl.BlockSpec((dh, tn), lambda i, j: (1, j)),
            pl.BlockSpec((tm, tn), lambda i, j: (jnp.minimum(i, last), j)),
            pl.BlockSpec((tr, tn), lambda i, j: (0, j)),
        ],
        out_specs=pl.BlockSpec((tm, tn), lambda i, j: (i, j)),
        out_shape=jax.ShapeDtypeStruct((t, d), F32),
        compiler_params=_cparams("parallel", "parallel"),
        name="out_proj",
    )(ys, og_p, og_t, w, w, x_p, x_t)


def _norm_t_kernel(h_ref, g_ref, o_ref):
    o_ref[...] = _rms(h_ref[...], g_ref[...]).T.astype(BF16)


def _norm_t(h, g, tm):
    t, d = h.shape
    return pl.pallas_call(
        _norm_t_kernel,
        grid=(t // tm,),
        in_specs=[pl.BlockSpec((tm, d), lambda i: (i, 0)), pl.BlockSpec((1, d), lambda i: (0, 0))],
        out_specs=pl.BlockSpec((d, tm), lambda i: (0, i)),
        out_shape=jax.ShapeDtypeStruct((d, t), BF16),
        compiler_params=_cparams("parallel"),
        name="ffn_norm_t",
    )(h, g)


def _topk_rows(x, k):
    outs = []
    for _ in range(k):
        m = jnp.max(x, axis=0, keepdims=True)
        outs.append(m)
        x = jnp.where(x == m, -jnp.inf, x)
    return outs


def _route_kernel(wq_ref, hn_ref, k1_ref, k2_ref, side_ref,
                  s1_ref, c1_ref, s2_ref, e2_ref, tau_ref, side_o_ref, *, nside):
    @pl.when(pl.program_id(0) * pl.num_programs(1) + pl.program_id(1) < nside)
    def _():
        _transpose_block(side_ref, side_o_ref)

    half = k1_ref.shape[1]
    tb = hn_ref.shape[1]
    qt = jnp.dot(wq_ref[...], hn_ref[...], preferred_element_type=F32)
    s1 = jnp.dot(k1_ref[...].astype(BF16), qt[:half, :].astype(BF16), preferred_element_type=F32)
    s2 = jnp.dot(k2_ref[...].astype(BF16), qt[half:, :].astype(BF16), preferred_element_type=F32)
    s1_ref[...] = s1
    s2_ref[...] = s2
    for t in range(tb // LANES):
        sl = slice(t * LANES, (t + 1) * LANES)
        a = s1[:, sl]
        b = s2[:, sl]
        kk = PEER_TOPK
        v1 = _topk_rows(a, kk)
        v2l = _topk_rows(b, kk)
        v2 = jnp.concatenate(v2l, axis=0)
        cand = jnp.concatenate(
            [v1[0] + v2]
            + [v1[i] + v2[:kk // 2] for i in range(1, kk // 2)]
            + [jnp.concatenate(v1[kk // 2:], axis=0) + v2l[0]], axis=0)
        top = _topk_rows(cand, kk)
        z = jnp.zeros_like(top[0])
        for c in top:
            z = z + jnp.exp(c - top[0])
        tau_ref[:, sl] = top[PEER_TOPK - 1]
        c1_ref[:, sl] = jnp.exp(a - v1[0]) / z
        e2_ref[:, sl] = jnp.exp(b - v2[0:1, :])


def _route(wq_t, hn_t, k1, k2, side, tb):
    d, t = hn_t.shape
    nh, nk, half = k1.shape
    qd = 2 * half
    ni = t // tb
    rs, cs = side.shape
    nside = max(k for k in range(1, ni * nh + 1) if rs % k == 0 and (rs // k) % MXU_DEPTH == 0)
    rb = rs // nside
    tab = jax.ShapeDtypeStruct((nh, nk, t), F32)
    tab_spec = pl.BlockSpec((None, nk, tb), lambda i, h: (h, 0, i))
    return pl.pallas_call(
        functools.partial(_route_kernel, nside=nside),
        grid=(ni, nh),
        in_specs=[
            pl.BlockSpec((qd, d), lambda i, h: (h, 0)),
            pl.BlockSpec((d, tb), lambda i, h: (0, i)),
            pl.BlockSpec((None, nk, half), lambda i, h: (h, 0, 0)),
            pl.BlockSpec((None, nk, half), lambda i, h: (h, 0, 0)),
            pl.BlockSpec((rb, cs), lambda i, h: (jnp.minimum(i * nh + h, nside - 1), 0)),
        ],
        out_specs=[tab_spec, tab_spec, tab_spec, tab_spec,
                   pl.BlockSpec((None, 1, tb), lambda i, h: (h, 0, i)),
                   pl.BlockSpec((cs, rb), lambda i, h: (0, jnp.minimum(i * nh + h, nside - 1)))],
        out_shape=[tab, tab, tab, tab, jax.ShapeDtypeStruct((nh, 1, t), F32),
                   jax.ShapeDtypeStruct((cs, rs), BF16)],
        compiler_params=_cparams("arbitrary", "arbitrary"),
        name="peer_route",
    )(wq_t, hn_t, k1, k2, side)


GATE_ROWS = 32
OUT_ROWS = 512


K_CHUNK = 512
EXPERT_BLOCK = 512


def _peer_kernel(u_ref, vt_ref, hn_ref, s1_ref, c1_ref, s2_ref, e2_ref, tau_ref, o_ref,
                 w_scr, act_scr, a_scr):
    eb, d = u_ref.shape
    tb = hn_ref.shape[1]
    nh, nk, _ = s2_ref.shape
    nr = eb // nk

    @pl.when(pl.program_id(1) == 0)
    def _():
        o_ref[...] = jnp.zeros_like(o_ref)

    def gate_group(t, q):
        sl = slice(t * LANES, (t + 1) * LANES)
        rows = slice(q * GATE_ROWS, (q + 1) * GATE_ROWS)
        accs = [None] * nr
        for h in range(nh):
            s2t = s2_ref[h, rows, sl]
            e2t = e2_ref[h, rows, sl]
            tau = tau_ref[h, :, sl]
            for r in range(nr):
                ssum = s1_ref[r, h:h + 1, sl] + s2t
                term = jnp.where(ssum >= tau, c1_ref[r, h:h + 1, sl] * e2t, 0.0)
                accs[r] = term if accs[r] is None else accs[r] + term
        for r in range(nr):
            lo = r * nk + q * GATE_ROWS
            w_scr[lo:lo + GATE_ROWS, sl] = accs[r]

    groups = [(t, q) for t in range(tb // LANES) for q in range(nk // GATE_ROWS)]
    nkc = d // K_CHUNK
    per = -(-len(groups) // nkc)
    for kc in range(nkc):
        @pl.when(pl.program_id(1) >= 0)
        def _(kc=kc):
            ks = slice(kc * K_CHUNK, (kc + 1) * K_CHUNK)
            part = jnp.dot(u_ref[:, ks], hn_ref[ks, :], preferred_element_type=F32)
            if kc == 0:
                act_scr[...] = part
            else:
                act_scr[...] += part
            for t, q in groups[kc * per:(kc + 1) * per]:
                gate_group(t, q)
    a_scr[...] = (w_scr[...] * _gelu(act_scr[...])).astype(BF16)
    for dc in range(d // OUT_ROWS):
        dr = slice(dc * OUT_ROWS, (dc + 1) * OUT_ROWS)
        o_ref[dr, :] += jnp.dot(vt_ref[dr, :], a_scr[...], preferred_element_type=F32)


def _peer(u_bf, v_t, hn_t, s1, c1, s2, e2, tau, tb, eb):
    ne, d = u_bf.shape
    t = hn_t.shape[1]
    nh, nk, _ = s2.shape
    once = pl.Buffered(1)
    tab_spec = pl.BlockSpec((nh, nk, tb), lambda i, e: (0, 0, i), pipeline_mode=once)
    row_spec = pl.BlockSpec((eb // nk, nh, tb), lambda i, e: (e, 0, i))
    return pl.pallas_call(
        _peer_kernel,
        grid=(t // tb, ne // eb),
        in_specs=[
            pl.BlockSpec((eb, d), lambda i, e: (e, 0)),
            pl.BlockSpec((d, eb), lambda i, e: (0, e)),
            pl.BlockSpec((d, tb), lambda i, e: (0, i), pipeline_mode=once),
            row_spec, row_spec, tab_spec, tab_spec,
            pl.BlockSpec((nh, 1, tb), lambda i, e: (0, 0, i)),
        ],
        out_specs=pl.BlockSpec((d, tb), lambda i, e: (0, i)),
        out_shape=jax.ShapeDtypeStruct((d, t), F32),
        scratch_shapes=[pltpu.VMEM((eb, tb), F32), pltpu.VMEM((eb, tb), F32),
                        pltpu.VMEM((eb, tb), BF16)],
        compiler_params=pltpu.CompilerParams(
            dimension_semantics=("parallel", "arbitrary"), vmem_limit_bytes=PEER_VMEM_LIMIT),
        name="peer_experts",
    )(u_bf, v_t, hn_t, s1, c1, s2, e2, tau)


def _final_kernel(pt_ref, h_ref, g_ref, yp_ref, yt_ref, *, nprompt):
    i = pl.program_id(0)
    y = _rms(h_ref[...] + pt_ref[...].T, g_ref[...])

    @pl.when(i < nprompt)
    def _():
        yp_ref[...] = y

    @pl.when(i >= nprompt)
    def _():
        yt_ref[...] = y


def _final(peer_t, h, g, tp, tr):
    d, t = peer_t.shape
    nprompt = tp // tr
    return pl.pallas_call(
        functools.partial(_final_kernel, nprompt=nprompt),
        grid=(nprompt + 1,),
        in_specs=[
            pl.BlockSpec((d, tr), lambda i: (0, i)),
            pl.BlockSpec((tr, d), lambda i: (i, 0)),
            pl.BlockSpec((1, d), lambda i: (0, 0)),
        ],
        out_specs=[
            pl.BlockSpec((tr, d), lambda i: (jnp.minimum(i, nprompt - 1), 0)),
            pl.BlockSpec((tr, d), lambda i: (0, 0)),
        ],
        out_shape=[jax.ShapeDtypeStruct((tp, d), F32), jax.ShapeDtypeStruct((tr, d), F32)],
        compiler_params=_cparams("arbitrary"),
        name="final_norm",
    )(peer_t, h, g)


def _block_diag(w, ngrp):
    nslab, _, a, b = w.shape
    eye = jnp.eye(ngrp, dtype=w.dtype)
    full = w[:, :, :, None, :] * eye[None, :, None, :, None]
    return full.reshape(nslab, ngrp * a, ngrp * b)


def kernel(x_prompt, x_sample, state_s5_re, state_s5_im, state_gla, meta_tokens, norm_mix_g, w_in, s5_lam_re, s5_lam_im, s5_log_dt, s5_b_re, s5_b_im, s5_c_re, s5_c_im, s5_d, s5_w_glu, s5_b_glu, s5_norm_g, gla_w_gate2, gla_b_gate2, gla_norm_g, w_out, norm_ffn_g, peer_w_q, peer_keys, peer_u, peer_v, norm_final_g):
    nb, seq, d = x_prompt.shape
    ns = x_sample.shape[0]
    nm = meta_tokens.shape[0]
    depth = w_in.shape[0]
    assert depth == 1 and x_sample.shape[1] == 1
    tp = nb * seq
    assert tp % TAIL_ROWS == 0 and ns + nm <= TAIL_ROWS and ns % SUBLANES == 0
    assert seq % GLA_CHUNK == 0 and nm % SUBLANES == 0
    t_all = tp + TAIL_ROWS

    ngrp, nstate, gch = s5_b_re.shape[1:]
    d_ssm = ngrp * gch
    gps = LANES // gch
    nslab = ngrp // gps
    nh, dk, dv = state_gla.shape[2:]
    d_gla = nh * dv
    rank = gla_w_gate2.shape[1]
    q0 = d_ssm
    k0 = q0 + nh * dk
    v0 = k0 + nh * dk
    r0 = v0 + d_gla
    g0 = r0 + d_gla
    assert g0 + rank == w_in.shape[2] and d_ssm + d_gla == w_out.shape[1]

    x_p = x_prompt.reshape(tp, d)
    x_t = jnp.concatenate([x_sample.reshape(ns, d), meta_tokens,
                           jnp.zeros((TAIL_ROWS - ns - nm, d), F32)], axis=0)
    w_in_bf = w_in[0].astype(BF16)
    w_g1 = jnp.pad(w_in[0, :, g0:], ((0, 0), (0, LANES - rank))).astype(BF16)
    w_g2 = jnp.pad(gla_w_gate2[0], ((0, LANES - rank), (0, 0))).astype(BF16)
    lam_re = s5_lam_re[0].reshape(1, ngrp * nstate)
    lam_im = s5_lam_im[0].reshape(1, ngrp * nstate)
    logdt = jnp.repeat(s5_log_dt[0], nstate).reshape(1, ngrp * nstate)
    b_re4 = jnp.transpose(s5_b_re[0].reshape(nslab, gps, nstate, gch), (0, 1, 3, 2))
    b_im4 = jnp.transpose(s5_b_im[0].reshape(nslab, gps, nstate, gch), (0, 1, 3, 2))
    c_re4 = jnp.transpose(s5_c_re[0].reshape(nslab, gps, gch, nstate), (0, 1, 3, 2))
    c_im4 = jnp.transpose(s5_c_im[0].reshape(nslab, gps, gch, nstate), (0, 1, 3, 2))
    bre_bd = _block_diag(b_re4, gps)
    bim_bd = _block_diag(b_im4, gps)
    cre_bd = _block_diag(c_re4, gps)
    cim_bd = _block_diag(c_im4, gps)
    h0r = state_s5_re[0].reshape(ns, ngrp * nstate)
    h0i = state_s5_im[0].reshape(ns, ngrp * nstate)

    tm = _row_tile(t_all, (768, 512, 256))

    xn = _prenorm(x_p, x_t, norm_mix_g)
    proj, xg, u_bf = _in_proj(xn, w_in_bf, w_g1, peer_u[0], tm, PROJ_COLS, g0)

    y_raw, pr, pi_, sr, si = _s5(proj, h0r, h0i, lam_re, lam_im, logdt, bre_bd, bim_bd,
                                 cre_bd, cim_bd, s5_d, nb, seq, ns, nm)
    y_ssm = _glu_norm(y_raw, s5_w_glu[0].astype(BF16), s5_b_glu, s5_norm_g, TAIL_ROWS)

    og_p, gla_p = _gla_prompt(proj, xg, w_g2, gla_b_gate2, gla_norm_g, nb, seq, ns, nm,
                              nh, dk, dv, q0, k0, v0, r0)
    og_s, gla_s = _gla_sample(proj, xg, w_g2, gla_b_gate2, gla_norm_g, state_gla[0],
                              tp, ns, nh, dk, dv, q0, k0, v0, r0)
    og_t = jnp.concatenate([og_s.astype(BF16), jnp.zeros((TAIL_ROWS - ns, d_gla), BF16)], axis=0)

    w_o = w_out[0].astype(BF16)
    assert d_ssm == d_gla
    h = _out_proj(y_ssm, og_p, og_t, w_o, x_p, x_t, tm, PROJ_COLS)

    hn_t = _norm_t(h, norm_ffn_g, TAIL_ROWS)
    wq_t = _transpose_cast(peer_w_q[0], WT_ROWS, WT_COLS)
    s1, c1, s2, e2, tau, v_t = _route(wq_t, hn_t, peer_keys[0, :, 0], peer_keys[0, :, 1],
                                      peer_v[0], tm)
    peer_t = _peer(u_bf, v_t, hn_t, jnp.transpose(s1, (1, 0, 2)), jnp.transpose(c1, (1, 0, 2)),
                   s2, e2, tau, tm, EXPERT_BLOCK)
    y_p, y_t = _final(peer_t, h, norm_final_g.reshape(1, d), tp, TAIL_ROWS)

    y_prompt = y_p.reshape(nb, seq, d)
    y_sample = y_t[:ns].reshape(ns, 1, d)
    return (y_prompt, y_sample,
            pr.reshape(1, nb, ngrp, nstate), pi_.reshape(1, nb, ngrp, nstate), gla_p[None],
            sr.reshape(1, ns, ngrp, nstate), si.reshape(1, ns, ngrp, nstate), gla_s[None])
```

```python
import functools

import jax
import jax.numpy as jnp
from jax import lax
from jax.experimental import pallas as pl
from jax.experimental.pallas import tpu as pltpu

F32 = jnp.float32
BF16 = jnp.bfloat16

EPS = 1e-6
GLA_TAU = 16.0
GLA_CHUNK = 64
GLA_SUB = 16
GLA_SAMPLES_PER_STEP = 4
PEER_TOPK = 16
LANES = 128
SUBLANES = 8
TAIL_ROWS = 256
PROJ_COLS = 1024
MXU_DEPTH = 256
WT_ROWS = 2048
WT_COLS = 1024
VMEM_LIMIT = 56 * 1024 * 1024
PEER_VMEM_LIMIT = 60 * 1024 * 1024


def _cparams(*sem):
    return pltpu.CompilerParams(dimension_semantics=sem, vmem_limit_bytes=VMEM_LIMIT)


def _gelu(x):
    return 0.5 * x * (1.0 + jnp.tanh(0.7978845608028654 * (x + 0.044715 * (x * x * x))))


def _sigmoid(x):
    return 1.0 / (1.0 + jnp.exp(-x))


def _log_sigmoid(x):
    return jnp.minimum(x, 0.0) - jnp.log(1.0 + jnp.exp(-jnp.abs(x)))


def _row_tile(n, cands):
    for c in cands:
        if n % c == 0:
            return c
    raise ValueError(f"no row tile for {n}")


def _transpose_block(x_ref, o_ref):
    k = MXU_DEPTH
    eye = jnp.where(lax.broadcasted_iota(jnp.int32, (k, k), 0)
                    == lax.broadcasted_iota(jnp.int32, (k, k), 1), 1.0, 0.0).astype(BF16)
    for c in range(x_ref.shape[1] // k):
        xb = x_ref[:, c * k:(c + 1) * k].astype(BF16)
        o_ref[c * k:(c + 1) * k, :] = lax.dot_general(
            eye, xb, (((1,), (1,)), ((), ())), preferred_element_type=F32).astype(BF16)


def _transpose_cast_kernel(x_ref, o_ref):
    _transpose_block(x_ref, o_ref)


def _transpose_cast(x, rb, cb):
    r, c = x.shape
    return pl.pallas_call(
        _transpose_cast_kernel,
        grid=(r // rb, c // cb),
        in_specs=[pl.BlockSpec((rb, cb), lambda i, j: (i, j))],
        out_specs=pl.BlockSpec((cb, rb), lambda i, j: (j, i)),
        out_shape=jax.ShapeDtypeStruct((c, r), BF16),
        compiler_params=_cparams("parallel", "parallel"),
        name="transpose_cast",
    )(x)


def _rms(x, g):
    return x * lax.rsqrt(jnp.mean(x * x, axis=-1, keepdims=True) + EPS) * g


def _prenorm_kernel(xp_ref, xt_ref, g_ref, o_ref, *, nprompt):
    i = pl.program_id(0)

    @pl.when(i < nprompt)
    def _():
        o_ref[...] = _rms(xp_ref[...], g_ref[...]).astype(BF16)

    @pl.when(i >= nprompt)
    def _():
        o_ref[...] = _rms(xt_ref[...], g_ref[...]).astype(BF16)


def _prenorm(x_p, x_t, g):
    tp, d = x_p.shape
    tr = x_t.shape[0]
    nprompt = tp // tr
    return pl.pallas_call(
        functools.partial(_prenorm_kernel, nprompt=nprompt),
        grid=(nprompt + 1,),
        in_specs=[
            pl.BlockSpec((tr, d), lambda i: (jnp.minimum(i, nprompt - 1), 0)),
            pl.BlockSpec((tr, d), lambda i: (0, 0)),
            pl.BlockSpec((1, d), lambda i: (0, 0)),
        ],
        out_specs=pl.BlockSpec((tr, d), lambda i: (i, 0)),
        out_shape=jax.ShapeDtypeStruct((tp + tr, d), BF16),
        compiler_params=_cparams("arbitrary"),
        name="mix_norm",
    )(x_p, x_t, g)


def _in_proj_kernel(xn_ref, w_ref, wg_ref, side_ref, o_ref, og_ref, side_o_ref, *, nside):
    nj = pl.num_programs(1)

    @pl.when(pl.program_id(1) == 0)
    def _():
        og_ref[...] = jnp.dot(xn_ref[...], wg_ref[...], preferred_element_type=F32)

    o_ref[...] = jnp.dot(xn_ref[...], w_ref[...], preferred_element_type=F32)

    @pl.when(pl.program_id(0) * nj + pl.program_id(1) < nside)
    def _():
        side_o_ref[...] = side_ref[...].astype(BF16)


def _in_proj(xn, w, wg, side, tm, tn, n):
    t, d = xn.shape
    ni, nj = t // tm, n // tn
    rs, cs = side.shape
    nside = max(k for k in range(1, ni * nj + 1) if rs % k == 0 and (rs // k) % SUBLANES == 0)
    rb = rs // nside
    side_idx = lambda i, j: (jnp.minimum(i * nj + j, nside - 1), 0)
    return pl.pallas_call(
        functools.partial(_in_proj_kernel, nside=nside),
        grid=(ni, nj),
        in_specs=[
            pl.BlockSpec((tm, d), lambda i, j: (i, 0)),
            pl.BlockSpec((d, tn), lambda i, j: (0, j)),
            pl.BlockSpec((d, LANES), lambda i, j: (0, 0)),
            pl.BlockSpec((rb, cs), side_idx),
        ],
        out_specs=[
            pl.BlockSpec((tm, tn), lambda i, j: (i, j)),
            pl.BlockSpec((tm, LANES), lambda i, j: (i, 0)),
            pl.BlockSpec((rb, cs), side_idx),
        ],
        out_shape=[jax.ShapeDtypeStruct((t, n), F32), jax.ShapeDtypeStruct((t, LANES), F32),
                   jax.ShapeDtypeStruct((rs, cs), BF16)],
        compiler_params=_cparams("arbitrary", "arbitrary"),
        name="in_proj",
    )(xn, w, wg, side)


def _s5_kernel(u_ref, h0r_ref, h0i_ref, lr_ref, li_ref, ldt_ref, bre_ref, bim_ref,
               cre_ref, cim_ref, d_ref,
               y_ref, pr_ref, pi_ref, sr_ref, si_ref,
               x_scr, init_scr, up_scr, yp_scr, *, nb, seq, ns, nm):
    sw = lr_ref.shape[1]
    tp = nb * seq
    nseg = SUBLANES
    ls = seq // nseg

    lr = lr_ref[...]
    li = li_ref[...]
    dt = jnp.exp(ldt_ref[...])
    mag = jnp.exp(lr * dt)
    ang = li * dt
    ar = mag * jnp.cos(ang)
    ai = mag * jnp.sin(ang)
    den = lr * lr + li * li
    nr = ar - 1.0
    qr = (nr * lr + ai * li) / den
    qi = (ai * lr - nr * li) / den
    bre = bre_ref[...]
    bim = bim_ref[...]
    bcat = jnp.concatenate([qr * bre - qi * bim, qr * bim + qi * bre], axis=1).astype(BF16)
    ccat = jnp.concatenate([cre_ref[...], -cim_ref[...]], axis=0).astype(BF16)
    dvec = d_ref[...]
    mag_s = jnp.exp(lr * dt * float(ls))
    asr = mag_s * jnp.cos(ang * float(ls))
    asi = mag_s * jnp.sin(ang * float(ls))

    def project_out(h, u):
        return jnp.dot(h.astype(BF16), ccat, preferred_element_type=F32) + dvec * u

    u_s = u_ref[tp:tp + ns, :]
    x_s = jnp.dot(u_s.astype(BF16), bcat, preferred_element_type=F32)
    h0r = h0r_ref[...]
    h0i = h0i_ref[...]
    hr_s = ar * h0r - ai * h0i + x_s[:, :sw]
    hi_s = ar * h0i + ai * h0r + x_s[:, sw:]
    sr_ref[...] = hr_s
    si_ref[...] = hi_s
    y_ref[tp:tp + ns, :] = project_out(jnp.concatenate([hr_s, hi_s], axis=1), u_s)
    y_ref[tp + ns:, :] = jnp.zeros((y_ref.shape[0] - tp - ns, y_ref.shape[1]), F32)

    u_m = u_ref[tp + ns:tp + ns + nm, :]
    x_m = jnp.dot(u_m.astype(BF16), bcat, preferred_element_type=F32)
    mr = jnp.zeros((1, sw), F32)
    mi = jnp.zeros((1, sw), F32)
    for t in range(nm):
        mr, mi = (ar * mr - ai * mi + x_m[t:t + 1, :sw],
                  ar * mi + ai * mr + x_m[t:t + 1, sw:])

    ar8 = jnp.broadcast_to(ar, (nseg, sw))
    ai8 = jnp.broadcast_to(ai, (nseg, sw))
    unroll = 4 if ls % 4 == 0 else 1

    def scan_pass(store):
        def body(i, c):
            hr, hi = c
            r = pl.multiple_of(i * nseg, nseg)
            nhr = ar8 * hr - ai8 * hi + x_scr[pl.ds(r, nseg), 0:sw]
            nhi = ar8 * hi + ai8 * hr + x_scr[pl.ds(r, nseg), sw:2 * sw]
            if store:
                x_scr[pl.ds(r, nseg), 0:sw] = nhr
                x_scr[pl.ds(r, nseg), sw:2 * sw] = nhi
            return nhr, nhi
        return body

    for b in range(nb):
        def regroup_in(i, c, b=b):
            r = pl.multiple_of(i * nseg, nseg)
            up_scr[pl.ds(r, nseg), :] = u_ref[pl.ds(b * seq + i, nseg, stride=ls), :]
            return c
        lax.fori_loop(0, ls, regroup_in, 0, unroll=unroll)
        u_b = up_scr[...]
        x_scr[...] = jnp.dot(u_b.astype(BF16), bcat, preferred_element_type=F32)
        z = jnp.zeros((nseg, sw), F32)
        fr, fi = lax.fori_loop(0, ls, scan_pass(False), (z, z), unroll=unroll)
        cr, ci = mr, mi
        for k in range(nseg):
            init_scr[k:k + 1, 0:sw] = cr
            init_scr[k:k + 1, sw:2 * sw] = ci
            cr, ci = (asr * cr - asi * ci + fr[k:k + 1, :],
                      asr * ci + asi * cr + fi[k:k + 1, :])
        pr_ref[b:b + 1, :] = cr
        pi_ref[b:b + 1, :] = ci
        lax.fori_loop(0, ls, scan_pass(True), (init_scr[:, 0:sw], init_scr[:, sw:2 * sw]),
                      unroll=unroll)
        yp_scr[...] = project_out(x_scr[...], u_b)

        def regroup_out(i, c, b=b):
            r = pl.multiple_of(i * nseg, nseg)
            y_ref[pl.ds(b * seq + i, nseg, stride=ls), :] = yp_scr[pl.ds(r, nseg), :]
            return c
        lax.fori_loop(0, ls, regroup_out, 0, unroll=unroll)


def _s5(proj, h0r, h0i, lam_re, lam_im, logdt, bre_bd, bim_bd, cre_bd, cim_bd, dvec,
        nb, seq, ns, nm):
    t_all = proj.shape[0]
    nslab, cw, sw = bre_bd.shape
    d_ssm = nslab * cw
    kern = functools.partial(_s5_kernel, nb=nb, seq=seq, ns=ns, nm=nm)
    return pl.pallas_call(
        kern,
        grid=(nslab,),
        in_specs=[
            pl.BlockSpec((t_all, cw), lambda s: (0, s)),
            pl.BlockSpec((ns, sw), lambda s: (0, s)),
            pl.BlockSpec((ns, sw), lambda s: (0, s)),
            pl.BlockSpec((1, sw), lambda s: (0, s)),
            pl.BlockSpec((1, sw), lambda s: (0, s)),
            pl.BlockSpec((1, sw), lambda s: (0, s)),
            pl.BlockSpec((None, cw, sw), lambda s: (s, 0, 0)),
            pl.BlockSpec((None, cw, sw), lambda s: (s, 0, 0)),
            pl.BlockSpec((None, sw, cw), lambda s: (s, 0, 0)),
            pl.BlockSpec((None, sw, cw), lambda s: (s, 0, 0)),
            pl.BlockSpec((1, cw), lambda s: (0, s)),
        ],
        out_specs=[
            pl.BlockSpec((t_all, cw), lambda s: (0, s)),
            pl.BlockSpec((nb, sw), lambda s: (0, s)),
            pl.BlockSpec((nb, sw), lambda s: (0, s)),
            pl.BlockSpec((ns, sw), lambda s: (0, s)),
            pl.BlockSpec((ns, sw), lambda s: (0, s)),
        ],
        out_shape=[
            jax.ShapeDtypeStruct((t_all, d_ssm), F32),
            jax.ShapeDtypeStruct((nb, nslab * sw), F32),
            jax.ShapeDtypeStruct((nb, nslab * sw), F32),
            jax.ShapeDtypeStruct((ns, nslab * sw), F32),
            jax.ShapeDtypeStruct((ns, nslab * sw), F32),
        ],
        scratch_shapes=[pltpu.VMEM((seq, 2 * sw), F32), pltpu.VMEM((SUBLANES, 2 * sw), F32),
                        pltpu.VMEM((seq, cw), F32), pltpu.VMEM((seq, cw), F32)],
        compiler_params=_cparams("parallel"),
        name="s5_scan",
    )(proj, h0r, h0i, lam_re, lam_im, logdt, bre_bd, bim_bd, cre_bd, cim_bd, dvec)


def _glu_norm_kernel(y_ref, w_ref, b_ref, g_ref, o_ref):
    z = _gelu(y_ref[...])
    gate = jnp.dot(z.astype(BF16), w_ref[...], preferred_element_type=F32) + b_ref[...]
    zz = z * _sigmoid(gate)
    s = lax.rsqrt(jnp.mean(zz * zz, axis=-1, keepdims=True) + EPS)
    o_ref[...] = (zz * s * g_ref[...]).astype(o_ref.dtype)


def _glu_norm(y, w, b, g, tm):
    t, d = y.shape
    return pl.pallas_call(
        _glu_norm_kernel,
        grid=(t // tm,),
        in_specs=[
            pl.BlockSpec((tm, d), lambda i: (i, 0)),
            pl.BlockSpec((d, d), lambda i: (0, 0)),
            pl.BlockSpec((1, d), lambda i: (0, 0)),
            pl.BlockSpec((1, d), lambda i: (0, 0)),
        ],
        out_specs=pl.BlockSpec((tm, d), lambda i: (i, 0)),
        out_shape=jax.ShapeDtypeStruct((t, d), BF16),
        compiler_params=_cparams("parallel"),
        name="s5_glu_norm",
    )(y, w, b, g)


def _cumsum_rows(x):
    n = x.shape[0]
    row = lax.broadcasted_iota(jnp.int32, x.shape, 0)
    s = 1
    while s < n:
        x = x + jnp.where(row >= s, pltpu.roll(x, s, axis=0), 0.0)
        s *= 2
    return x


def _gate(xg, w2, b2):
    pre = jnp.dot(xg.astype(BF16), w2, preferred_element_type=F32) + b2
    return _log_sigmoid(pre) * (1.0 / GLA_TAU)


def _head_out(o, r, g):
    o = o * lax.rsqrt(jnp.mean(o * o, axis=-1, keepdims=True) + EPS)
    return o * g * (r * _sigmoid(r))


def _gla_prompt_kernel(q_ref, k_ref, v_ref, r_ref, xg_ref, kt_ref, vt_ref, xgt_ref,
                       w2_ref, b2_ref, g_ref, o_ref, s_ref, st_scr, *, seq, ns, nm):
    dk = q_ref.shape[1]
    scale = float(dk) ** -0.5
    w2 = w2_ref[...]
    b2 = b2_ref[...]
    g = g_ref[...]

    lgm = _gate(xgt_ref[ns:ns + nm, :], w2, b2)
    bm = _cumsum_rows(lgm)
    kdm = kt_ref[ns:ns + nm, :] * jnp.exp(bm[nm - 1:nm, :] - bm)
    st_scr[...] = lax.dot_general(vt_ref[ns:ns + nm, :].astype(BF16), kdm.astype(BF16),
                                  (((0,), (0,)), ((), ())), preferred_element_type=F32)

    c = GLA_CHUNK
    nsub = c // GLA_SUB

    def chunk(ci, carry):
        r0 = pl.multiple_of(ci * c, c)
        q = q_ref[pl.ds(r0, c), :] * scale
        k = k_ref[pl.ds(r0, c), :]
        v = v_ref[pl.ds(r0, c), :]
        vb = v.astype(BF16)
        lg = _gate(xg_ref[pl.ds(r0, c), :], w2, b2)
        b = _cumsum_rows(lg)
        bl = b[c - 1:c, :]
        st = st_scr[...]
        o_inter = lax.dot_general((q * jnp.exp(b)).astype(BF16), st.astype(BF16),
                                  (((1,), (1,)), ((), ())), preferred_element_type=F32)
        outs = []
        for sb in range(nsub):
            lo = sb * GLA_SUB
            hi = lo + GLA_SUB
            beta = b[lo - 1:lo, :] if sb > 0 else jnp.zeros((1, dk), F32)
            qs = q[lo:hi, :] * jnp.exp(b[lo:hi, :] - beta)
            ks = k[0:hi, :] * jnp.exp(beta - b[0:hi, :])
            sc = lax.dot_general(qs.astype(BF16), ks.astype(BF16),
                                 (((1,), (1,)), ((), ())), preferred_element_type=F32)
            rowi = lax.broadcasted_iota(jnp.int32, (GLA_SUB, hi), 0)
            coli = lax.broadcasted_iota(jnp.int32, (GLA_SUB, hi), 1)
            sc = jnp.where(coli <= rowi + lo, sc, 0.0)
            outs.append(jnp.dot(sc.astype(BF16), vb[0:hi, :], preferred_element_type=F32))
        o = o_inter + jnp.concatenate(outs, axis=0)
        o_ref[pl.ds(r0, c), :] = _head_out(o, r_ref[pl.ds(r0, c), :], g).astype(o_ref.dtype)
        kd = k * jnp.exp(bl - b)
        st_scr[...] = jnp.exp(bl) * st + lax.dot_general(
            vb, kd.astype(BF16), (((0,), (0,)), ((), ())), preferred_element_type=F32)
        return carry

    lax.fori_loop(0, seq // c, chunk, 0, unroll=min(8, seq // c))
    s_ref[...] = st_scr[...].T


def _gla_prompt(proj, xg, w2, b2, g, nb, seq, ns, nm, nh, dk, dv, q0, k0, v0, r0):
    tp = nb * seq
    tail_blk = tp // TAIL_ROWS
    rb = seq
    kern = functools.partial(_gla_prompt_kernel, seq=seq, ns=ns, nm=nm)
    return pl.pallas_call(
        kern,
        grid=(nb, nh),
        in_specs=[
            pl.BlockSpec((rb, dk), lambda b, h: (b, q0 // dk + h)),
            pl.BlockSpec((rb, dk), lambda b, h: (b, k0 // dk + h)),
            pl.BlockSpec((rb, dv), lambda b, h: (b, v0 // dv + h)),
            pl.BlockSpec((rb, dv), lambda b, h: (b, r0 // dv + h)),
            pl.BlockSpec((rb, LANES), lambda b, h: (b, 0)),
            pl.BlockSpec((TAIL_ROWS, dk), lambda b, h: (tail_blk, k0 // dk + h)),
            pl.BlockSpec((TAIL_ROWS, dv), lambda b, h: (tail_blk, v0 // dv + h)),
            pl.BlockSpec((TAIL_ROWS, LANES), lambda b, h: (tail_blk, 0)),
            pl.BlockSpec((LANES, dk), lambda b, h: (0, h)),
            pl.BlockSpec((1, dk), lambda b, h: (0, h)),
            pl.BlockSpec((1, dv), lambda b, h: (0, h)),
        ],
        out_specs=[
            pl.BlockSpec((rb, dv), lambda b, h: (b, h)),
            pl.BlockSpec((None, None, dk, dv), lambda b, h: (b, h, 0, 0)),
        ],
        out_shape=[
            jax.ShapeDtypeStruct((tp, nh * dv), BF16),
            jax.ShapeDtypeStruct((nb, nh, dk, dv), F32),
        ],
        scratch_shapes=[pltpu.VMEM((dv, dk), F32)],
        compiler_params=_cparams("parallel", "parallel"),
        name="gla_prompt",
    )(proj, proj, proj, proj, xg, proj, proj, xg, w2, b2, g)


def _gla_sample_kernel(q_ref, k_ref, v_ref, r_ref, xg_ref, w2_ref, b2_ref, g_ref, s0_ref,
                       o_ref, s_ref, *, nh, dk, dv, sps):
    for si in range(sps):
        _gla_sample_one(q_ref, k_ref, v_ref, r_ref, xg_ref, w2_ref, b2_ref, g_ref, s0_ref,
                        o_ref, s_ref, si, nh=nh, dk=dk, dv=dv, sps=sps)


def _gla_sample_one(q_ref, k_ref, v_ref, r_ref, xg_ref, w2_ref, b2_ref, g_ref, s0_ref,
                    o_ref, s_ref, si, *, nh, dk, dv, sps):
    n = pl.program_id(0) * sps + si
    j = n % SUBLANES
    scale = float(dk) ** -0.5
    lg_all = _gate(xg_ref[pl.ds(j, 1), :], w2_ref[...], b2_ref[...])
    q_all = q_ref[pl.ds(j, 1), :] * scale
    k_all = k_ref[pl.ds(j, 1), :]
    v_all = v_ref[pl.ds(j, 1), :]
    r_all = r_ref[pl.ds(j, 1), :]
    g_all = g_ref[...]
    rows = 2 * SUBLANES
    rk = lax.broadcasted_iota(jnp.int32, (rows, dk), 0)
    rv = lax.broadcasted_iota(jnp.int32, (rows, dv), 0)
    outs = []
    for h in range(nh):
        lg = lg_all[:, h * dk:(h + 1) * dk]
        q = q_all[:, h * dk:(h + 1) * dk]
        k = k_all[:, h * dk:(h + 1) * dk]
        v = v_all[:, h * dv:(h + 1) * dv]
        e = jnp.exp(lg)
        s0 = s0_ref[si, h]
        e_hi = e.astype(BF16)
        e_mid = (e - e_hi.astype(F32)).astype(BF16)
        e_lo = (e - e_hi.astype(F32) - e_mid.astype(F32)).astype(BF16)
        def rows_of(x, w):
            return jnp.broadcast_to(x.astype(F32), (rows, w))
        lhs = jnp.where(rk == 0, rows_of(e_hi, dk),
              jnp.where(rk == 1, rows_of(e_mid, dk),
              jnp.where(rk == 2, rows_of(e_lo, dk),
              jnp.where(rk == 3, rows_of(k, dk), 0.0)))).astype(BF16)
        ones_part = jnp.where(rv < 3, 1.0, 0.0)
        v_part = jnp.where(rv == 3, rows_of(v, dv), 0.0)
        rhs = jnp.concatenate([ones_part, v_part], axis=1).astype(BF16)
        both = lax.dot_general(lhs, rhs, (((0,), (0,)), ((), ())), preferred_element_type=F32)
        s_ref[si, h] = both[:, :dv] * s0 + both[:, dv:]
        qe = jnp.broadcast_to((q * e).astype(BF16), (rows, dk))
        o = jnp.dot(qe, s0.astype(BF16), preferred_element_type=F32)[0:1, :]
        o = o + jnp.sum(q * k, axis=-1, keepdims=True) * v
        outs.append(_head_out(o, r_all[:, h * dv:(h + 1) * dv], g_all[:, h * dv:(h + 1) * dv]))
    o_ref[pl.ds(j, 1), :] = jnp.concatenate(outs, axis=1)


def _gla_sample(proj, xg, w2, b2, g, s0, tp, ns, nh, dk, dv, q0, k0, v0, r0):
    rb = SUBLANES
    base = tp // rb
    qk_w = nh * dk
    v_w = nh * dv
    sps = GLA_SAMPLES_PER_STEP
    assert ns % sps == 0 and rb % sps == 0
    kern = functools.partial(_gla_sample_kernel, nh=nh, dk=dk, dv=dv, sps=sps)
    row = lambda n: base + (n * sps) // rb
    return pl.pallas_call(
        kern,
        grid=(ns // sps,),
        in_specs=[
            pl.BlockSpec((rb, qk_w), lambda n: (row(n), q0 // qk_w)),
            pl.BlockSpec((rb, qk_w), lambda n: (row(n), k0 // qk_w)),
            pl.BlockSpec((rb, v_w), lambda n: (row(n), v0 // v_w)),
            pl.BlockSpec((rb, v_w), lambda n: (row(n), r0 // v_w)),
            pl.BlockSpec((rb, LANES), lambda n: (row(n), 0)),
            pl.BlockSpec((LANES, qk_w), lambda n: (0, 0)),
            pl.BlockSpec((1, qk_w), lambda n: (0, 0)),
            pl.BlockSpec((1, v_w), lambda n: (0, 0)),
            pl.BlockSpec((sps, nh, dk, dv), lambda n: (n, 0, 0, 0)),
        ],
        out_specs=[
            pl.BlockSpec((rb, v_w), lambda n: ((n * sps) // rb, 0)),
            pl.BlockSpec((sps, nh, dk, dv), lambda n: (n, 0, 0, 0)),
        ],
        out_shape=[
            jax.ShapeDtypeStruct((ns, v_w), F32),
            jax.ShapeDtypeStruct((ns, nh, dk, dv), F32),
        ],
        compiler_params=_cparams("arbitrary"),
        name="gla_sample",
    )(proj, proj, proj, proj, xg, w2, b2, g, s0)


def _out_proj_kernel(ys_ref, ogp_ref, ogt_ref, w1_ref, w2_ref, xp_ref, xt_ref, h_ref,
                     *, nfull, split):
    i = pl.program_id(0)
    w2 = w2_ref[...]
    acc = jnp.dot(ys_ref[...], w1_ref[...], preferred_element_type=F32)

    @pl.when(i < nfull)
    def _():
        h_ref[...] = xp_ref[...] + acc + jnp.dot(ogp_ref[...], w2, preferred_element_type=F32)

    @pl.when(i >= nfull)
    def _():
        if split:
            h_ref[0:split, :] = xp_ref[0:split, :] + acc[0:split] + jnp.dot(
                ogp_ref[0:split, :], w2, preferred_element_type=F32)
        h_ref[split:, :] = xt_ref[...] + acc[split:] + jnp.dot(
            ogt_ref[...], w2, preferred_element_type=F32)


def _out_proj(ys, og_p, og_t, w, x_p, x_t, tm, tn):
    t, dh = ys.shape
    d = w.shape[1]
    tp, tr = x_p.shape[0], x_t.shape[0]
    nfull = tp // tm
    split = tp - nfull * tm
    assert t == tp + tr and t == (nfull + 1) * tm and tm - split == tr
    last = pl.cdiv(tp, tm) - 1
    return pl.pallas_call(
        functools.partial(_out_proj_kernel, nfull=nfull, split=split),
        grid=(t // tm, d // tn),
        in_specs=[
            pl.BlockSpec((tm, dh), lambda i, j: (i, 0)),
            pl.BlockSpec((tm, dh), lambda i, j: (jnp.minimum(i, last), 0)),
            pl.BlockSpec((tr, dh), lambda i, j: (0, 0)),
            pl.BlockSpec((dh, tn), lambda i, j: (0, j)),
            pl.BlockSpec((dh, tn), lambda i, j: (1, j)),
            pl.BlockSpec((tm, tn), lambda i, j: (jnp.minimum(i, last), j)),
            pl.BlockSpec((tr, tn), lambda i, j: (0, j)),
        ],
        out_specs=pl.BlockSpec((tm, tn), lambda i, j: (i, j)),
        out_shape=jax.ShapeDtypeStruct((t, d), F32),
        compiler_params=_cparams("parallel", "parallel"),
        name="out_proj",
    )(ys, og_p, og_t, w, w, x_p, x_t)


def _norm_t_kernel(h_ref, g_ref, o_ref):
    o_ref[...] = _rms(h_ref[...], g_ref[...]).T.astype(BF16)


def _norm_t(h, g, tm):
    t, d = h.shape
    return pl.pallas_call(
        _norm_t_kernel,
        grid=(t // tm,),
        in_specs=[pl.BlockSpec((tm, d), lambda i: (i, 0)), pl.BlockSpec((1, d), lambda i: (0, 0))],
        out_specs=pl.BlockSpec((d, tm), lambda i: (0, i)),
        out_shape=jax.ShapeDtypeStruct((d, t), BF16),
        compiler_params=_cparams("parallel"),
        name="ffn_norm_t",
    )(h, g)


def _topk_rows(x, k):
    outs = []
    for _ in range(k):
        m = jnp.max(x, axis=0, keepdims=True)
        outs.append(m)
        x = jnp.where(x == m, -jnp.inf, x)
    return outs


def _route_kernel(wq_ref, hn_ref, k1_ref, k2_ref, side_ref,
                  s1_ref, c1_ref, s2_ref, e2_ref, tau_ref, side_o_ref, *, nside):
    @pl.when(pl.program_id(0) * pl.num_programs(1) + pl.program_id(1) < nside)
    def _():
        _transpose_block(side_ref, side_o_ref)

    half = k1_ref.shape[1]
    tb = hn_ref.shape[1]
    qt = jnp.dot(wq_ref[...], hn_ref[...], preferred_element_type=F32)
    s1 = jnp.dot(k1_ref[...].astype(BF16), qt[:half, :].astype(BF16), preferred_element_type=F32)
    s2 = jnp.dot(k2_ref[...].astype(BF16), qt[half:, :].astype(BF16), preferred_element_type=F32)
    s1_ref[...] = s1
    s2_ref[...] = s2
    for t in range(tb // LANES):
        sl = slice(t * LANES, (t + 1) * LANES)
        a = s1[:, sl]
        b = s2[:, sl]
        kk = PEER_TOPK
        v1 = _topk_rows(a, kk)
        v2l = _topk_rows(b, kk)
        v2 = jnp.concatenate(v2l, axis=0)
        cand = jnp.concatenate(
            [v1[0] + v2]
            + [v1[i] + v2[:kk // 2] for i in range(1, kk // 2)]
            + [jnp.concatenate(v1[kk // 2:], axis=0) + v2l[0]], axis=0)
        top = _topk_rows(cand, kk)
        z = jnp.zeros_like(top[0])
        for c in top:
            z = z + jnp.exp(c - top[0])
        tau_ref[:, sl] = top[PEER_TOPK - 1]
        c1_ref[:, sl] = jnp.exp(a - v1[0]) / z
        e2_ref[:, sl] = jnp.exp(b - v2[0:1, :])


def _route(wq_t, hn_t, k1, k2, side, tb):
    d, t = hn_t.shape
    nh, nk, half = k1.shape
    qd = 2 * half
    ni = t // tb
    rs, cs = side.shape
    nside = max(k for k in range(1, ni * nh + 1) if rs % k == 0 and (rs // k) % MXU_DEPTH == 0)
    rb = rs // nside
    tab = jax.ShapeDtypeStruct((nh, nk, t), F32)
    tab_spec = pl.BlockSpec((None, nk, tb), lambda i, h: (h, 0, i))
    return pl.pallas_call(
        functools.partial(_route_kernel, nside=nside),
        grid=(ni, nh),
        in_specs=[
            pl.BlockSpec((qd, d), lambda i, h: (h, 0)),
            pl.BlockSpec((d, tb), lambda i, h: (0, i)),
            pl.BlockSpec((None, nk, half), lambda i, h: (h, 0, 0)),
            pl.BlockSpec((None, nk, half), lambda i, h: (h, 0, 0)),
            pl.BlockSpec((rb, cs), lambda i, h: (jnp.minimum(i * nh + h, nside - 1), 0)),
        ],
        out_specs=[tab_spec, tab_spec, tab_spec, tab_spec,
                   pl.BlockSpec((None, 1, tb), lambda i, h: (h, 0, i)),
                   pl.BlockSpec((cs, rb), lambda i, h: (0, jnp.minimum(i * nh + h, nside - 1)))],
        out_shape=[tab, tab, tab, tab, jax.ShapeDtypeStruct((nh, 1, t), F32),
                   jax.ShapeDtypeStruct((cs, rs), BF16)],
        compiler_params=_cparams("arbitrary", "arbitrary"),
        name="peer_route",
    )(wq_t, hn_t, k1, k2, side)


GATE_ROWS = 32
OUT_ROWS = 512


K_CHUNK = 512
EXPERT_BLOCK = 512


def _peer_kernel(u_ref, vt_ref, hn_ref, s1_ref, c1_ref, s2_ref, e2_ref, tau_ref, o_ref,
                 w_scr, act_scr, a_scr):
    eb, d = u_ref.shape
    tb = hn_ref.shape[1]
    nh, nk, _ = s2_ref.shape
    nr = eb // nk

    @pl.when(pl.program_id(1) == 0)
    def _():
        o_ref[...] = jnp.zeros_like(o_ref)

    def gate_group(t, q):
        sl = slice(t * LANES, (t + 1) * LANES)
        rows = slice(q * GATE_ROWS, (q + 1) * GATE_ROWS)
        accs = [None] * nr
        for h in range(nh):
            s2t = s2_ref[h, rows, sl]
            e2t = e2_ref[h, rows, sl]
            tau = tau_ref[h, :, sl]
            for r in range(nr):
                ssum = s1_ref[r, h:h + 1, sl] + s2t
                term = jnp.where(ssum >= tau, c1_ref[r, h:h + 1, sl] * e2t, 0.0)
                accs[r] = term if accs[r] is None else accs[r] + term
        for r in range(nr):
            lo = r * nk + q * GATE_ROWS
            w_scr[lo:lo + GATE_ROWS, sl] = accs[r]

    groups = [(t, q) for t in range(tb // LANES) for q in range(nk // GATE_ROWS)]
    nkc = d // K_CHUNK
    per = -(-len(groups) // nkc)
    for kc in range(nkc):
        @pl.when(pl.program_id(1) >= 0)
        def _(kc=kc):
            ks = slice(kc * K_CHUNK, (kc + 1) * K_CHUNK)
            part = jnp.dot(u_ref[:, ks], hn_ref[ks, :], preferred_element_type=F32)
            if kc == 0:
                act_scr[...] = part
            else:
                act_scr[...] += part
            for t, q in groups[kc * per:(kc + 1) * per]:
                gate_group(t, q)
    a_scr[...] = (w_scr[...] * _gelu(act_scr[...])).astype(BF16)
    for dc in range(d // OUT_ROWS):
        dr = slice(dc * OUT_ROWS, (dc + 1) * OUT_ROWS)
        o_ref[dr, :] += jnp.dot(vt_ref[dr, :], a_scr[...], preferred_element_type=F32)


def _peer(u_bf, v_t, hn_t, s1, c1, s2, e2, tau, tb, eb):
    ne, d = u_bf.shape
    t = hn_t.shape[1]
    nh, nk, _ = s2.shape
    once = pl.Buffered(1)
    tab_spec = pl.BlockSpec((nh, nk, tb), lambda i, e: (0, 0, i), pipeline_mode=once)
    row_spec = pl.BlockSpec((eb // nk, nh, tb), lambda i, e: (e, 0, i))
    return pl.pallas_call(
        _peer_kernel,
        grid=(t // tb, ne // eb),
        in_specs=[
            pl.BlockSpec((eb, d), lambda i, e: (e, 0)),
            pl.BlockSpec((d, eb), lambda i, e: (0, e)),
            pl.BlockSpec((d, tb), lambda i, e: (0, i), pipeline_mode=once),
            row_spec, row_spec, tab_spec, tab_spec,
            pl.BlockSpec((nh, 1, tb), lambda i, e: (0, 0, i)),
        ],
        out_specs=pl.BlockSpec((d, tb), lambda i, e: (0, i)),
        out_shape=jax.ShapeDtypeStruct((d, t), F32),
        scratch_shapes=[pltpu.VMEM((eb, tb), F32), pltpu.VMEM((eb, tb), F32),
                        pltpu.VMEM((eb, tb), BF16)],
        compiler_params=pltpu.CompilerParams(
            dimension_semantics=("parallel", "arbitrary"), vmem_limit_bytes=PEER_VMEM_LIMIT),
        name="peer_experts",
    )(u_bf, v_t, hn_t, s1, c1, s2, e2, tau)


def _final_kernel(pt_ref, h_ref, g_ref, yp_ref, yt_ref, *, nprompt):
    i = pl.program_id(0)
    y = _rms(h_ref[...] + pt_ref[...].T, g_ref[...])

    @pl.when(i < nprompt)
    def _():
        yp_ref[...] = y

    @pl.when(i >= nprompt)
    def _():
        yt_ref[...] = y


def _final(peer_t, h, g, tp, tr):
    d, t = peer_t.shape
    nprompt = tp // tr
    return pl.pallas_call(
        functools.partial(_final_kernel, nprompt=nprompt),
        grid=(nprompt + 1,),
        in_specs=[
            pl.BlockSpec((d, tr), lambda i: (0, i)),
            pl.BlockSpec((tr, d), lambda i: (i, 0)),
            pl.BlockSpec((1, d), lambda i: (0, 0)),
        ],
        out_specs=[
            pl.BlockSpec((tr, d), lambda i: (jnp.minimum(i, nprompt - 1), 0)),
            pl.BlockSpec((tr, d), lambda i: (0, 0)),
        ],
        out_shape=[jax.ShapeDtypeStruct((tp, d), F32), jax.ShapeDtypeStruct((tr, d), F32)],
        compiler_params=_cparams("arbitrary"),
        name="final_norm",
    )(peer_t, h, g)


def _block_diag(w, ngrp):
    nslab, _, a, b = w.shape
    eye = jnp.eye(ngrp, dtype=w.dtype)
    full = w[:, :, :, None, :] * eye[None, :, None, :, None]
    return full.reshape(nslab, ngrp * a, ngrp * b)


def kernel(x_prompt, x_sample, state_s5_re, state_s5_im, state_gla, meta_tokens, norm_mix_g, w_in, s5_lam_re, s5_lam_im, s5_log_dt, s5_b_re, s5_b_im, s5_c_re, s5_c_im, s5_d, s5_w_glu, s5_b_glu, s5_norm_g, gla_w_gate2, gla_b_gate2, gla_norm_g, w_out, norm_ffn_g, peer_w_q, peer_keys, peer_u, peer_v, norm_final_g):
    nb, seq, d = x_prompt.shape
    ns = x_sample.shape[0]
    nm = meta_tokens.shape[0]
    depth = w_in.shape[0]
    assert depth == 1 and x_sample.shape[1] == 1
    tp = nb * seq
    assert tp % TAIL_ROWS == 0 and ns + nm <= TAIL_ROWS and ns % SUBLANES == 0
    assert seq % GLA_CHUNK == 0 and nm % SUBLANES == 0
    t_all = tp + TAIL_ROWS

    ngrp, nstate, gch = s5_b_re.shape[1:]
    d_ssm = ngrp * gch
    gps = LANES // gch
    nslab = ngrp // gps
    nh, dk, dv = state_gla.shape[2:]
    d_gla = nh * dv
    rank = gla_w_gate2.shape[1]
    q0 = d_ssm
    k0 = q0 + nh * dk
    v0 = k0 + nh * dk
    r0 = v0 + d_gla
    g0 = r0 + d_gla
    assert g0 + rank == w_in.shape[2] and d_ssm + d_gla == w_out.shape[1]

    x_p = x_prompt.reshape(tp, d)
    x_t = jnp.concatenate([x_sample.reshape(ns, d), meta_tokens,
                           jnp.zeros((TAIL_ROWS - ns - nm, d), F32)], axis=0)
    w_in_bf = w_in[0].astype(BF16)
    w_g1 = jnp.pad(w_in[0, :, g0:], ((0, 0), (0, LANES - rank))).astype(BF16)
    w_g2 = jnp.pad(gla_w_gate2[0], ((0, LANES - rank), (0, 0))).astype(BF16)
    lam_re = s5_lam_re[0].reshape(1, ngrp * nstate)
    lam_im = s5_lam_im[0].reshape(1, ngrp * nstate)
    logdt = jnp.repeat(s5_log_dt[0], nstate).reshape(1, ngrp * nstate)
    b_re4 = jnp.transpose(s5_b_re[0].reshape(nslab, gps, nstate, gch), (0, 1, 3, 2))
    b_im4 = jnp.transpose(s5_b_im[0].reshape(nslab, gps, nstate, gch), (0, 1, 3, 2))
    c_re4 = jnp.transpose(s5_c_re[0].reshape(nslab, gps, gch, nstate), (0, 1, 3, 2))
    c_im4 = jnp.transpose(s5_c_im[0].reshape(nslab, gps, gch, nstate), (0, 1, 3, 2))
    bre_bd = _block_diag(b_re4, gps)
    bim_bd = _block_diag(b_im4, gps)
    cre_bd = _block_diag(c_re4, gps)
    cim_bd = _block_diag(c_im4, gps)
    h0r = state_s5_re[0].reshape(ns, ngrp * nstate)
    h0i = state_s5_im[0].reshape(ns, ngrp * nstate)

    tm = _row_tile(t_all, (768, 512, 256))

    xn = _prenorm(x_p, x_t, norm_mix_g)
    proj, xg, u_bf = _in_proj(xn, w_in_bf, w_g1, peer_u[0], tm, PROJ_COLS, g0)

    y_raw, pr, pi_, sr, si = _s5(proj, h0r, h0i, lam_re, lam_im, logdt, bre_bd, bim_bd,
                                 cre_bd, cim_bd, s5_d, nb, seq, ns, nm)
    y_ssm = _glu_norm(y_raw, s5_w_glu[0].astype(BF16), s5_b_glu, s5_norm_g, TAIL_ROWS)

    og_p, gla_p = _gla_prompt(proj, xg, w_g2, gla_b_gate2, gla_norm_g, nb, seq, ns, nm,
                              nh, dk, dv, q0, k0, v0, r0)
    og_s, gla_s = _gla_sample(proj, xg, w_g2, gla_b_gate2, gla_norm_g, state_gla[0],
                              tp, ns, nh, dk, dv, q0, k0, v0, r0)
    og_t = jnp.concatenate([og_s.astype(BF16), jnp.zeros((TAIL_ROWS - ns, d_gla), BF16)], axis=0)

    w_o = w_out[0].astype(BF16)
    assert d_ssm == d_gla
    h = _out_proj(y_ssm, og_p, og_t, w_o, x_p, x_t, tm, PROJ_COLS)

    hn_t = _norm_t(h, norm_ffn_g, TAIL_ROWS)
    wq_t = _transpose_cast(peer_w_q[0], WT_ROWS, WT_COLS)
    s1, c1, s2, e2, tau, v_t = _route(wq_t, hn_t, peer_keys[0, :, 0], peer_keys[0, :, 1],
                                      peer_v[0], tm)
    peer_t = _peer(u_bf, v_t, hn_t, jnp.transpose(s1, (1, 0, 2)), jnp.transpose(c1, (1, 0, 2)),
                   s2, e2, tau, tm, EXPERT_BLOCK)
    y_p, y_t = _final(peer_t, h, norm_final_g.reshape(1, d), tp, TAIL_ROWS)

    y_prompt = y_p.reshape(nb, seq, d)
    y_sample = y_t[:ns].reshape(ns, 1, d)
    return (y_prompt, y_sample,
            pr.reshape(1, nb, ngrp, nstate), pi_.reshape(1, nb, ngrp, nstate), gla_p[None],
            sr.reshape(1, ns, ngrp, nstate), si.reshape(1, ns, ngrp, nstate), gla_s[None])
```

```python
import functools

import jax
import jax.numpy as jnp
from jax import lax
from jax.experimental import pallas as pl
from jax.experimental.pallas import tpu as pltpu

F32 = jnp.float32
BF16 = jnp.bfloat16

EPS = 1e-6
GLA_TAU = 16.0
GLA_CHUNK = 64
GLA_SUB = 16
GLA_SAMPLES_PER_STEP = 4
PEER_TOPK = 16
LANES = 128
SUBLANES = 8
TAIL_ROWS = 256
PROJ_COLS = 1024
MXU_DEPTH = 256
WT_ROWS = 2048
WT_COLS = 1024
VMEM_LIMIT = 56 * 1024 * 1024
PEER_VMEM_LIMIT = 60 * 1024 * 1024


def _cparams(*sem):
    return pltpu.CompilerParams(dimension_semantics=sem, vmem_limit_bytes=VMEM_LIMIT)


def _gelu(x):
    return 0.5 * x * (1.0 + jnp.tanh(0.7978845608028654 * (x + 0.044715 * (x * x * x))))


def _sigmoid(x):
    return 1.0 / (1.0 + jnp.exp(-x))


def _log_sigmoid(x):
    return jnp.minimum(x, 0.0) - jnp.log(1.0 + jnp.exp(-jnp.abs(x)))


def _row_tile(n, cands):
    for c in cands:
        if n % c == 0:
            return c
    raise ValueError(f"no row tile for {n}")


def _transpose_block(x_ref, o_ref):
    k = MXU_DEPTH
    eye = jnp.where(lax.broadcasted_iota(jnp.int32, (k, k), 0)
                    == lax.broadcasted_iota(jnp.int32, (k, k), 1), 1.0, 0.0).astype(BF16)
    for c in range(x_ref.shape[1] // k):
        xb = x_ref[:, c * k:(c + 1) * k].astype(BF16)
        o_ref[c * k:(c + 1) * k, :] = lax.dot_general(
            eye, xb, (((1,), (1,)), ((), ())), preferred_element_type=F32).astype(BF16)


def _transpose_cast_kernel(x_ref, o_ref):
    _transpose_block(x_ref, o_ref)


def _transpose_cast(x, rb, cb):
    r, c = x.shape
    return pl.pallas_call(
        _transpose_cast_kernel,
        grid=(r // rb, c // cb),
        in_specs=[pl.BlockSpec((rb, cb), lambda i, j: (i, j))],
        out_specs=pl.BlockSpec((cb, rb), lambda i, j: (j, i)),
        out_shape=jax.ShapeDtypeStruct((c, r), BF16),
        compiler_params=_cparams("parallel", "parallel"),
        name="transpose_cast",
    )(x)


def _rms(x, g):
    return x * lax.rsqrt(jnp.mean(x * x, axis=-1, keepdims=True) + EPS) * g


def _prenorm_kernel(xp_ref, xt_ref, g_ref, o_ref, *, nprompt):
    i = pl.program_id(0)

    @pl.when(i < nprompt)
    def _():
        o_ref[...] = _rms(xp_ref[...], g_ref[...]).astype(BF16)

    @pl.when(i >= nprompt)
    def _():
        o_ref[...] = _rms(xt_ref[...], g_ref[...]).astype(BF16)


def _prenorm(x_p, x_t, g):
    tp, d = x_p.shape
    tr = x_t.shape[0]
    nprompt = tp // tr
    return pl.pallas_call(
        functools.partial(_prenorm_kernel, nprompt=nprompt),
        grid=(nprompt + 1,),
        in_specs=[
            pl.BlockSpec((tr, d), lambda i: (jnp.minimum(i, nprompt - 1), 0)),
            pl.BlockSpec((tr, d), lambda i: (0, 0)),
            pl.BlockSpec((1, d), lambda i: (0, 0)),
        ],
        out_specs=pl.BlockSpec((tr, d), lambda i: (i, 0)),
        out_shape=jax.ShapeDtypeStruct((tp + tr, d), BF16),
        compiler_params=_cparams("arbitrary"),
        name="mix_norm",
    )(x_p, x_t, g)


def _in_proj_kernel(xn_ref, w_ref, wg_ref, side_ref, o_ref, og_ref, side_o_ref, *, nside):
    nj = pl.num_programs(1)

    @pl.when(pl.program_id(1) == 0)
    def _():
        og_ref[...] = jnp.dot(xn_ref[...], wg_ref[...], preferred_element_type=F32)

    o_ref[...] = jnp.dot(xn_ref[...], w_ref[...], preferred_element_type=F32)

    @pl.when(pl.program_id(0) * nj + pl.program_id(1) < nside)
    def _():
        side_o_ref[...] = side_ref[...].astype(BF16)


def _in_proj(xn, w, wg, side, tm, tn, n):
    t, d = xn.shape
    ni, nj = t // tm, n // tn
    rs, cs = side.shape
    nside = max(k for k in range(1, ni * nj + 1) if rs % k == 0 and (rs // k) % SUBLANES == 0)
    rb = rs // nside
    side_idx = lambda i, j: (jnp.minimum(i * nj + j, nside - 1), 0)
    return pl.pallas_call(
        functools.partial(_in_proj_kernel, nside=nside),
        grid=(ni, nj),
        in_specs=[
            pl.BlockSpec((tm, d), lambda i, j: (i, 0)),
            pl.BlockSpec((d, tn), lambda i, j: (0, j)),
            pl.BlockSpec((d, LANES), lambda i, j: (0, 0)),
            pl.BlockSpec((rb, cs), side_idx),
        ],
        out_specs=[
            pl.BlockSpec((tm, tn), lambda i, j: (i, j)),
            pl.BlockSpec((tm, LANES), lambda i, j: (i, 0)),
            pl.BlockSpec((rb, cs), side_idx),
        ],
        out_shape=[jax.ShapeDtypeStruct((t, n), F32), jax.ShapeDtypeStruct((t, LANES), F32),
                   jax.ShapeDtypeStruct((rs, cs), BF16)],
        compiler_params=_cparams("arbitrary", "arbitrary"),
        name="in_proj",
    )(xn, w, wg, side)


def _s5_kernel(u_ref, h0r_ref, h0i_ref, lr_ref, li_ref, ldt_ref, bre_ref, bim_ref,
               cre_ref, cim_ref, d_ref,
               y_ref, pr_ref, pi_ref, sr_ref, si_ref,
               x_scr, init_scr, up_scr, yp_scr, *, nb, seq, ns, nm):
    sw = lr_ref.shape[1]
    tp = nb * seq
    nseg = SUBLANES
    ls = seq // nseg

    lr = lr_ref[...]
    li = li_ref[...]
    dt = jnp.exp(ldt_ref[...])
    mag = jnp.exp(lr * dt)
    ang = li * dt
    ar = mag * jnp.cos(ang)
    ai = mag * jnp.sin(ang)
    den = lr * lr + li * li
    nr = ar - 1.0
    qr = (nr * lr + ai * li) / den
    qi = (ai * lr - nr * li) / den
    bre = bre_ref[...]
    bim = bim_ref[...]
    bcat = jnp.concatenate([qr * bre - qi * bim, qr * bim + qi * bre], axis=1).astype(BF16)
    ccat = jnp.concatenate([cre_ref[...], -cim_ref[...]], axis=0).astype(BF16)
    dvec = d_ref[...]
    mag_s = jnp.exp(lr * dt * float(ls))
    asr = mag_s * jnp.cos(ang * float(ls))
    asi = mag_s * jnp.sin(ang * float(ls))

    def project_out(h, u):
        return jnp.dot(h.astype(BF16), ccat, preferred_element_type=F32) + dvec * u

    u_s = u_ref[tp:tp + ns, :]
    x_s = jnp.dot(u_s.astype(BF16), bcat, preferred_element_type=F32)
    h0r = h0r_ref[...]
    h0i = h0i_ref[...]
    hr_s = ar * h0r - ai * h0i + x_s[:, :sw]
    hi_s = ar * h0i + ai * h0r + x_s[:, sw:]
    sr_ref[...] = hr_s
    si_ref[...] = hi_s
    y_ref[tp:tp + ns, :] = project_out(jnp.concatenate([hr_s, hi_s], axis=1), u_s)
    y_ref[tp + ns:, :] = jnp.zeros((y_ref.shape[0] - tp - ns, y_ref.shape[1]), F32)

    u_m = u_ref[tp + ns:tp + ns + nm, :]
    x_m = jnp.dot(u_m.astype(BF16), bcat, preferred_element_type=F32)
    mr = jnp.zeros((1, sw), F32)
    mi = jnp.zeros((1, sw), F32)
    for t in range(nm):
        mr, mi = (ar * mr - ai * mi + x_m[t:t + 1, :sw],
                  ar * mi + ai * mr + x_m[t:t + 1, sw:])

    ar8 = jnp.broadcast_to(ar, (nseg, sw))
    ai8 = jnp.broadcast_to(ai, (nseg, sw))
    unroll = 4 if ls % 4 == 0 else 1

    def scan_pass(store):
        def body(i, c):
            hr, hi = c
            r = pl.multiple_of(i * nseg, nseg)
            nhr = ar8 * hr - ai8 * hi + x_scr[pl.ds(r, nseg), 0:sw]
            nhi = ar8 * hi + ai8 * hr + x_scr[pl.ds(r, nseg), sw:2 * sw]
            if store:
                x_scr[pl.ds(r, nseg), 0:sw] = nhr
                x_scr[pl.ds(r, nseg), sw:2 * sw] = nhi
            return nhr, nhi
        return body

    for b in range(nb):
        def regroup_in(i, c, b=b):
            r = pl.multiple_of(i * nseg, nseg)
            up_scr[pl.ds(r, nseg), :] = u_ref[pl.ds(b * seq + i, nseg, stride=ls), :]
            return c
        lax.fori_loop(0, ls, regroup_in, 0, unroll=unroll)
        u_b = up_scr[...]
        x_scr[...] = jnp.dot(u_b.astype(BF16), bcat, preferred_element_type=F32)
        z = jnp.zeros((nseg, sw), F32)
        fr, fi = lax.fori_loop(0, ls, scan_pass(False), (z, z), unroll=unroll)
        cr, ci = mr, mi
        for k in range(nseg):
            init_scr[k:k + 1, 0:sw] = cr
            init_scr[k:k + 1, sw:2 * sw] = ci
            cr, ci = (asr * cr - asi * ci + fr[k:k + 1, :],
                      asr * ci + asi * cr + fi[k:k + 1, :])
        pr_ref[b:b + 1, :] = cr
        pi_ref[b:b + 1, :] = ci
        lax.fori_loop(0, ls, scan_pass(True), (init_scr[:, 0:sw], init_scr[:, sw:2 * sw]),
                      unroll=unroll)
        yp_scr[...] = project_out(x_scr[...], u_b)

        def regroup_out(i, c, b=b):
            r = pl.multiple_of(i * nseg, nseg)
            y_ref[pl.ds(b * seq + i, nseg, stride=ls), :] = yp_scr[pl.ds(r, nseg), :]
            return c
        lax.fori_loop(0, ls, regroup_out, 0, unroll=unroll)


def _s5(proj, h0r, h0i, lam_re, lam_im, logdt, bre_bd, bim_bd, cre_bd, cim_bd, dvec,
        nb, seq, ns, nm):
    t_all = proj.shape[0]
    nslab, cw, sw = bre_bd.shape
    d_ssm = nslab * cw
    kern = functools.partial(_s5_kernel, nb=nb, seq=seq, ns=ns, nm=nm)
    return pl.pallas_call(
        kern,
        grid=(nslab,),
        in_specs=[
            pl.BlockSpec((t_all, cw), lambda s: (0, s)),
            pl.BlockSpec((ns, sw), lambda s: (0, s)),
            pl.BlockSpec((ns, sw), lambda s: (0, s)),
            pl.BlockSpec((1, sw), lambda s: (0, s)),
            pl.BlockSpec((1, sw), lambda s: (0, s)),
            pl.BlockSpec((1, sw), lambda s: (0, s)),
            pl.BlockSpec((None, cw, sw), lambda s: (s, 0, 0)),
            pl.BlockSpec((None, cw, sw), lambda s: (s, 0, 0)),
            pl.BlockSpec((None, sw, cw), lambda s: (s, 0, 0)),
            pl.BlockSpec((None, sw, cw), lambda s: (s, 0, 0)),
            pl.BlockSpec((1, cw), lambda s: (0, s)),
        ],
        out_specs=[
            pl.BlockSpec((t_all, cw), lambda s: (0, s)),
            pl.BlockSpec((nb, sw), lambda s: (0, s)),
            pl.BlockSpec((nb, sw), lambda s: (0, s)),
            pl.BlockSpec((ns, sw), lambda s: (0, s)),
            pl.BlockSpec((ns, sw), lambda s: (0, s)),
        ],
        out_shape=[
            jax.ShapeDtypeStruct((t_all, d_ssm), F32),
            jax.ShapeDtypeStruct((nb, nslab * sw), F32),
            jax.ShapeDtypeStruct((nb, nslab * sw), F32),
            jax.ShapeDtypeStruct((ns, nslab * sw), F32),
            jax.ShapeDtypeStruct((ns, nslab * sw), F32),
        ],
        scratch_shapes=[pltpu.VMEM((seq, 2 * sw), F32), pltpu.VMEM((SUBLANES, 2 * sw), F32),
                        pltpu.VMEM((seq, cw), F32), pltpu.VMEM((seq, cw), F32)],
        compiler_params=_cparams("parallel"),
        name="s5_scan",
    )(proj, h0r, h0i, lam_re, lam_im, logdt, bre_bd, bim_bd, cre_bd, cim_bd, dvec)


def _glu_norm_kernel(y_ref, w_ref, b_ref, g_ref, o_ref):
    z = _gelu(y_ref[...])
    gate = jnp.dot(z.astype(BF16), w_ref[...], preferred_element_type=F32) + b_ref[...]
    zz = z * _sigmoid(gate)
    s = lax.rsqrt(jnp.mean(zz * zz, axis=-1, keepdims=True) + EPS)
    o_ref[...] = (zz * s * g_ref[...]).astype(o_ref.dtype)


def _glu_norm(y, w, b, g, tm):
    t, d = y.shape
    return pl.pallas_call(
        _glu_norm_kernel,
        grid=(t // tm,),
        in_specs=[
            pl.BlockSpec((tm, d), lambda i: (i, 0)),
            pl.BlockSpec((d, d), lambda i: (0, 0)),
            pl.BlockSpec((1, d), lambda i: (0, 0)),
            pl.BlockSpec((1, d), lambda i: (0, 0)),
        ],
        out_specs=pl.BlockSpec((tm, d), lambda i: (i, 0)),
        out_shape=jax.ShapeDtypeStruct((t, d), BF16),
        compiler_params=_cparams("parallel"),
        name="s5_glu_norm",
    )(y, w, b, g)


def _cumsum_rows(x):
    n = x.shape[0]
    row = lax.broadcasted_iota(jnp.int32, x.shape, 0)
    s = 1
    while s < n:
        x = x + jnp.where(row >= s, pltpu.roll(x, s, axis=0), 0.0)
        s *= 2
    return x


def _gate(xg, w2, b2):
    pre = jnp.dot(xg.astype(BF16), w2, preferred_element_type=F32) + b2
    return _log_sigmoid(pre) * (1.0 / GLA_TAU)


def _head_out(o, r, g):
    o = o * lax.rsqrt(jnp.mean(o * o, axis=-1, keepdims=True) + EPS)
    return o * g * (r * _sigmoid(r))


def _gla_prompt_kernel(q_ref, k_ref, v_ref, r_ref, xg_ref, kt_ref, vt_ref, xgt_ref,
                       w2_ref, b2_ref, g_ref, o_ref, s_ref, st_scr, *, seq, ns, nm):
    dk = q_ref.shape[1]
    scale = float(dk) ** -0.5
    w2 = w2_ref[...]
    b2 = b2_ref[...]
    g = g_ref[...]

    lgm = _gate(xgt_ref[ns:ns + nm, :], w2, b2)
    bm = _cumsum_rows(lgm)
    kdm = kt_ref[ns:ns + nm, :] * jnp.exp(bm[nm - 1:nm, :] - bm)
    st_scr[...] = lax.dot_general(vt_ref[ns:ns + nm, :].astype(BF16), kdm.astype(BF16),
                                  (((0,), (0,)), ((), ())), preferred_element_type=F32)

    c = GLA_CHUNK
    nsub = c // GLA_SUB

    def chunk(ci, carry):
        r0 = pl.multiple_of(ci * c, c)
        q = q_ref[pl.ds(r0, c), :] * scale
        k = k_ref[pl.ds(r0, c), :]
        v = v_ref[pl.ds(r0, c), :]
        vb = v.astype(BF16)
        lg = _gate(xg_ref[pl.ds(r0, c), :], w2, b2)
        b = _cumsum_rows(lg)
        bl = b[c - 1:c, :]
        st = st_scr[...]
        o_inter = lax.dot_general((q * jnp.exp(b)).astype(BF16), st.astype(BF16),
                                  (((1,), (1,)), ((), ())), preferred_element_type=F32)
        outs = []
        for sb in range(nsub):
            lo = sb * GLA_SUB
            hi = lo + GLA_SUB
            beta = b[lo - 1:lo, :] if sb > 0 else jnp.zeros((1, dk), F32)
            qs = q[lo:hi, :] * jnp.exp(b[lo:hi, :] - beta)
            ks = k[0:hi, :] * jnp.exp(beta - b[0:hi, :])
            sc = lax.dot_general(qs.astype(BF16), ks.astype(BF16),
                                 (((1,), (1,)), ((), ())), preferred_element_type=F32)
            rowi = lax.broadcasted_iota(jnp.int32, (GLA_SUB, hi), 0)
            coli = lax.broadcasted_iota(jnp.int32, (GLA_SUB, hi), 1)
            sc = jnp.where(coli <= rowi + lo, sc, 0.0)
            outs.append(jnp.dot(sc.astype(BF16), vb[0:hi, :], preferred_element_type=F32))
        o = o_inter + jnp.concatenate(outs, axis=0)
        o_ref[pl.ds(r0, c), :] = _head_out(o, r_ref[pl.ds(r0, c), :], g).astype(o_ref.dtype)
        kd = k * jnp.exp(bl - b)
        st_scr[...] = jnp.exp(bl) * st + lax.dot_general(
            vb, kd.astype(BF16), (((0,), (0,)), ((), ())), preferred_element_type=F32)
        return carry

    lax.fori_loop(0, seq // c, chunk, 0, unroll=min(8, seq // c))
    s_ref[...] = st_scr[...].T


def _gla_prompt(proj, xg, w2, b2, g, nb, seq, ns, nm, nh, dk, dv, q0, k0, v0, r0):
    tp = nb * seq
    tail_blk = tp // TAIL_ROWS
    rb = seq
    kern = functools.partial(_gla_prompt_kernel, seq=seq, ns=ns, nm=nm)
    return pl.pallas_call(
        kern,
        grid=(nb, nh),
        in_specs=[
            pl.BlockSpec((rb, dk), lambda b, h: (b, q0 // dk + h)),
            pl.BlockSpec((rb, dk), lambda b, h: (b, k0 // dk + h)),
            pl.BlockSpec((rb, dv), lambda b, h: (b, v0 // dv + h)),
            pl.BlockSpec((rb, dv), lambda b, h: (b, r0 // dv + h)),
            pl.BlockSpec((rb, LANES), lambda b, h: (b, 0)),
            pl.BlockSpec((TAIL_ROWS, dk), lambda b, h: (tail_blk, k0 // dk + h)),
            pl.BlockSpec((TAIL_ROWS, dv), lambda b, h: (tail_blk, v0 // dv + h)),
            pl.BlockSpec((TAIL_ROWS, LANES), lambda b, h: (tail_blk, 0)),
            pl.BlockSpec((LANES, dk), lambda b, h: (0, h)),
            pl.BlockSpec((1, dk), lambda b, h: (0, h)),
            pl.BlockSpec((1, dv), lambda b, h: (0, h)),
        ],
        out_specs=[
            pl.BlockSpec((rb, dv), lambda b, h: (b, h)),
            pl.BlockSpec((None, None, dk, dv), lambda b, h: (b, h, 0, 0)),
        ],
        out_shape=[
            jax.ShapeDtypeStruct((tp, nh * dv), BF16),
            jax.ShapeDtypeStruct((nb, nh, dk, dv), F32),
        ],
        scratch_shapes=[pltpu.VMEM((dv, dk), F32)],
        compiler_params=_cparams("parallel", "parallel"),
        name="gla_prompt",
    )(proj, proj, proj, proj, xg, proj, proj, xg, w2, b2, g)


def _gla_sample_kernel(q_ref, k_ref, v_ref, r_ref, xg_ref, w2_ref, b2_ref, g_ref, s0_ref,
                       o_ref, s_ref, *, nh, dk, dv, sps):
    for si in range(sps):
        _gla_sample_one(q_ref, k_ref, v_ref, r_ref, xg_ref, w2_ref, b2_ref, g_ref, s0_ref,
                        o_ref, s_ref, si, nh=nh, dk=dk, dv=dv, sps=sps)


def _gla_sample_one(q_ref, k_ref, v_ref, r_ref, xg_ref, w2_ref, b2_ref, g_ref, s0_ref,
                    o_ref, s_ref, si, *, nh, dk, dv, sps):
    n = pl.program_id(0) * sps + si
    j = n % SUBLANES
    scale = float(dk) ** -0.5
    lg_all = _gate(xg_ref[pl.ds(j, 1), :], w2_ref[...], b2_ref[...])
    q_all = q_ref[pl.ds(j, 1), :] * scale
    k_all = k_ref[pl.ds(j, 1), :]
    v_all = v_ref[pl.ds(j, 1), :]
    r_all = r_ref[pl.ds(j, 1), :]
    g_all = g_ref[...]
    rows = 2 * SUBLANES
    rk = lax.broadcasted_iota(jnp.int32, (rows, dk), 0)
    rv = lax.broadcasted_iota(jnp.int32, (rows, dv), 0)
    outs = []
    for h in range(nh):
        lg = lg_all[:, h * dk:(h + 1) * dk]
        q = q_all[:, h * dk:(h + 1) * dk]
        k = k_all[:, h * dk:(h + 1) * dk]
        v = v_all[:, h * dv:(h + 1) * dv]
        e = jnp.exp(lg)
        s0 = s0_ref[si, h]
        e_hi = e.astype(BF16)
        e_mid = (e - e_hi.astype(F32)).astype(BF16)
        e_lo = (e - e_hi.astype(F32) - e_mid.astype(F32)).astype(BF16)
        def rows_of(x, w):
            return jnp.broadcast_to(x.astype(F32), (rows, w))
        lhs = jnp.where(rk == 0, rows_of(e_hi, dk),
              jnp.where(rk == 1, rows_of(e_mid, dk),
              jnp.where(rk == 2, rows_of(e_lo, dk),
              jnp.where(rk == 3, rows_of(k, dk), 0.0)))).astype(BF16)
        ones_part = jnp.where(rv < 3, 1.0, 0.0)
        v_part = jnp.where(rv == 3, rows_of(v, dv), 0.0)
        rhs = jnp.concatenate([ones_part, v_part], axis=1).astype(BF16)
        both = lax.dot_general(lhs, rhs, (((0,), (0,)), ((), ())), preferred_element_type=F32)
        s_ref[si, h] = both[:, :dv] * s0 + both[:, dv:]
        qe = jnp.broadcast_to((q * e).astype(BF16), (rows, dk))
        o = jnp.dot(qe, s0.astype(BF16), preferred_element_type=F32)[0:1, :]
        o = o + jnp.sum(q * k, axis=-1, keepdims=True) * v
        outs.append(_head_out(o, r_all[:, h * dv:(h + 1) * dv], g_all[:, h * dv:(h + 1) * dv]))
    o_ref[pl.ds(j, 1), :] = jnp.concatenate(outs, axis=1)


def _gla_sample(proj, xg, w2, b2, g, s0, tp, ns, nh, dk, dv, q0, k0, v0, r0):
    rb = SUBLANES
    base = tp // rb
    qk_w = nh * dk
    v_w = nh * dv
    sps = GLA_SAMPLES_PER_STEP
    assert ns % sps == 0 and rb % sps == 0
    kern = functools.partial(_gla_sample_kernel, nh=nh, dk=dk, dv=dv, sps=sps)
    row = lambda n: base + (n * sps) // rb
    return pl.pallas_call(
        kern,
        grid=(ns // sps,),
        in_specs=[
            pl.BlockSpec((rb, qk_w), lambda n: (row(n), q0 // qk_w)),
            pl.BlockSpec((rb, qk_w), lambda n: (row(n), k0 // qk_w)),
            pl.BlockSpec((rb, v_w), lambda n: (row(n), v0 // v_w)),
            pl.BlockSpec((rb, v_w), lambda n: (row(n), r0 // v_w)),
            pl.BlockSpec((rb, LANES), lambda n: (row(n), 0)),
            pl.BlockSpec((LANES, qk_w), lambda n: (0, 0)),
            pl.BlockSpec((1, qk_w), lambda n: (0, 0)),
            pl.BlockSpec((1, v_w), lambda n: (0, 0)),
            pl.BlockSpec((sps, nh, dk, dv), lambda n: (n, 0, 0, 0)),
        ],
        out_specs=[
            pl.BlockSpec((rb, v_w), lambda n: ((n * sps) // rb, 0)),
            pl.BlockSpec((sps, nh, dk, dv), lambda n: (n, 0, 0, 0)),
        ],
        out_shape=[
            jax.ShapeDtypeStruct((ns, v_w), F32),
            jax.ShapeDtypeStruct((ns, nh, dk, dv), F32),
        ],
        compiler_params=_cparams("arbitrary"),
        name="gla_sample",
    )(proj, proj, proj, proj, xg, w2, b2, g, s0)


def _out_proj_kernel(ys_ref, ogp_ref, ogt_ref, w1_ref, w2_ref, xp_ref, xt_ref, h_ref,
                     *, nfull, split):
    i = pl.program_id(0)
    w2 = w2_ref[...]
    acc = jnp.dot(ys_ref[...], w1_ref[...], preferred_element_type=F32)

    @pl.when(i < nfull)
    def _():
        h_ref[...] = xp_ref[...] + acc + jnp.dot(ogp_ref[...], w2, preferred_element_type=F32)

    @pl.when(i >= nfull)
    def _():
        if split:
            h_ref[0:split, :] = xp_ref[0:split, :] + acc[0:split] + jnp.dot(
                ogp_ref[0:split, :], w2, preferred_element_type=F32)
        h_ref[split:, :] = xt_ref[...] + acc[split:] + jnp.dot(
            ogt_ref[...], w2, preferred_element_type=F32)


def _out_proj(ys, og_p, og_t, w, x_p, x_t, tm, tn):
    t, dh = ys.shape
    d = w.shape[1]
    tp, tr = x_p.shape[0], x_t.shape[0]
    nfull = tp // tm
    split = tp - nfull * tm
    assert t == tp + tr and t == (nfull + 1) * tm and tm - split == tr
    last = pl.cdiv(tp, tm) - 1
    return pl.pallas_call(
        functools.partial(_out_proj_kernel, nfull=nfull, split=split),
        grid=(t // tm, d // tn),
        in_specs=[
            pl.BlockSpec((tm, dh), lambda i, j: (i, 0)),
            pl.BlockSpec((tm, dh), lambda i, j: (jnp.minimum(i, last), 0)),
            pl.BlockSpec((tr, dh), lambda i, j: (0, 0)),
            pl.BlockSpec((dh, tn), lambda i, j: (0, j)),
            pl.BlockSpec((dh, tn), lambda i, j: (1, j)),
            pl.BlockSpec((tm, tn), lambda i, j: (jnp.minimum(i, last), j)),
            pl.BlockSpec((tr, tn), lambda i, j: (0, j)),
        ],
        out_specs=pl.BlockSpec((tm, tn), lambda i, j: (i, j)),
        out_shape=jax.ShapeDtypeStruct((t, d), F32),
        compiler_params=_cparams("parallel", "parallel"),
        name="out_proj",
    )(ys, og_p, og_t, w, w, x_p, x_t)


def _norm_t_kernel(h_ref, g_ref, o_ref):
    o_ref[...] = _rms(h_ref[...], g_ref[...]).T.astype(BF16)


def _norm_t(h, g, tm):
    t, d = h.shape
    return pl.pallas_call(
        _norm_t_kernel,
        grid=(t // tm,),
        in_specs=[pl.BlockSpec((tm, d), lambda i: (i, 0)), pl.BlockSpec((1, d), lambda i: (0, 0))],
        out_specs=pl.BlockSpec((d, tm), lambda i: (0, i)),
        out_shape=jax.ShapeDtypeStruct((d, t), BF16),
        compiler_params=_cparams("parallel"),
        name="ffn_norm_t",
    )(h, g)


def _topk_rows(x, k):
    outs = []
    for _ in range(k):
        m = jnp.max(x, axis=0, keepdims=True)
        outs.append(m)
        x = jnp.where(x == m, -jnp.inf, x)
    return outs


def _route_kernel(wq_ref, hn_ref, k1_ref, k2_ref, side_ref,
                  s1_ref, c1_ref, s2_ref, e2_ref, tau_ref, side_o_ref, *, nside):
    @pl.when(pl.program_id(0) * pl.num_programs(1) + pl.program_id(1) < nside)
    def _():
        _transpose_block(side_ref, side_o_ref)

    half = k1_ref.shape[1]
    tb = hn_ref.shape[1]
    qt = jnp.dot(wq_ref[...], hn_ref[...], preferred_element_type=F32)
    s1 = jnp.dot(k1_ref[...].astype(BF16), qt[:half, :].astype(BF16), preferred_element_type=F32)
    s2 = jnp.dot(k2_ref[...].astype(BF16), qt[half:, :].astype(BF16), preferred_element_type=F32)
    s1_ref[...] = s1
    s2_ref[...] = s2
    for t in range(tb // LANES):
        sl = slice(t * LANES, (t + 1) * LANES)
        a = s1[:, sl]
        b = s2[:, sl]
        kk = PEER_TOPK
        v1 = _topk_rows(a, kk)
        v2l = _topk_rows(b, kk)
        v2 = jnp.concatenate(v2l, axis=0)
        cand = jnp.concatenate(
            [v1[0] + v2]
            + [v1[i] + v2[:kk // 2] for i in range(1, kk // 2)]
            + [jnp.concatenate(v1[kk // 2:], axis=0) + v2l[0]], axis=0)
        top = _topk_rows(cand, kk)
        z = jnp.zeros_like(top[0])
        for c in top:
            z = z + jnp.exp(c - top[0])
        tau_ref[:, sl] = top[PEER_TOPK - 1]
        c1_ref[:, sl] = jnp.exp(a - v1[0]) / z
        e2_ref[:, sl] = jnp.exp(b - v2[0:1, :])


def _route(wq_t, hn_t, k1, k2, side, tb):
    d, t = hn_t.shape
    nh, nk, half = k1.shape
    qd = 2 * half
    ni = t // tb
    rs, cs = side.shape
    nside = max(k for k in range(1, ni * nh + 1) if rs % k == 0 and (rs // k) % MXU_DEPTH == 0)
    rb = rs // nside
    tab = jax.ShapeDtypeStruct((nh, nk, t), F32)
    tab_spec = pl.BlockSpec((None, nk, tb), lambda i, h: (h, 0, i))
    return pl.pallas_call(
        functools.partial(_route_kernel, nside=nside),
        grid=(ni, nh),
        in_specs=[
            pl.BlockSpec((qd, d), lambda i, h: (h, 0)),
            pl.BlockSpec((d, tb), lambda i, h: (0, i)),
            pl.BlockSpec((None, nk, half), lambda i, h: (h, 0, 0)),
            pl.BlockSpec((None, nk, half), lambda i, h: (h, 0, 0)),
            pl.BlockSpec((rb, cs), lambda i, h: (jnp.minimum(i * nh + h, nside - 1), 0)),
        ],
        out_specs=[tab_spec, tab_spec, tab_spec, tab_spec,
                   pl.BlockSpec((None, 1, tb), lambda i, h: (h, 0, i)),
                   pl.BlockSpec((cs, rb), lambda i, h: (0, jnp.minimum(i * nh + h, nside - 1)))],
        out_shape=[tab, tab, tab, tab, jax.ShapeDtypeStruct((nh, 1, t), F32),
                   jax.ShapeDtypeStruct((cs, rs), BF16)],
        compiler_params=_cparams("arbitrary", "arbitrary"),
        name="peer_route",
    )(wq_t, hn_t, k1, k2, side)


GATE_ROWS = 32
OUT_ROWS = 512


K_CHUNK = 256
EXPERT_BLOCK = 512


def _peer_kernel(u_ref, vt_ref, hn_ref, s1_ref, c1_ref, s2_ref, e2_ref, tau_ref, o_ref,
                 w_scr, act_scr, a_scr):
    eb, d = u_ref.shape
    tb = hn_ref.shape[1]
    nh, nk, _ = s2_ref.shape
    nr = eb // nk

    @pl.when(pl.program_id(1) == 0)
    def _():
        o_ref[...] = jnp.zeros_like(o_ref)

    def gate_group(t, q):
        sl = slice(t * LANES, (t + 1) * LANES)
        rows = slice(q * GATE_ROWS, (q + 1) * GATE_ROWS)
        accs = [None] * nr
        for h in range(nh):
            s2t = s2_ref[h, rows, sl]
            e2t = e2_ref[h, rows, sl]
            tau = tau_ref[h, :, sl]
            for r in range(nr):
                ssum = s1_ref[r, h:h + 1, sl] + s2t
                term = jnp.where(ssum >= tau, c1_ref[r, h:h + 1, sl] * e2t, 0.0)
                accs[r] = term if accs[r] is None else accs[r] + term
        for r in range(nr):
            lo = r * nk + q * GATE_ROWS
            w_scr[lo:lo + GATE_ROWS, sl] = accs[r]

    groups = [(t, q) for t in range(tb // LANES) for q in range(nk // GATE_ROWS)]
    nkc = d // K_CHUNK
    per = -(-len(groups) // nkc)
    for kc in range(nkc):
        @pl.when(pl.program_id(1) >= 0)
        def _(kc=kc):
            ks = slice(kc * K_CHUNK, (kc + 1) * K_CHUNK)
            part = jnp.dot(u_ref[:, ks], hn_ref[ks, :], preferred_element_type=F32)
            if kc == 0:
                act_scr[...] = part
            else:
                act_scr[...] += part
            for t, q in groups[kc * per:(kc + 1) * per]:
                gate_group(t, q)
    a_scr[...] = (w_scr[...] * _gelu(act_scr[...])).astype(BF16)
    for dc in range(d // OUT_ROWS):
        dr = slice(dc * OUT_ROWS, (dc + 1) * OUT_ROWS)
        o_ref[dr, :] += jnp.dot(vt_ref[dr, :], a_scr[...], preferred_element_type=F32)


def _peer(u_bf, v_t, hn_t, s1, c1, s2, e2, tau, tb, eb):
    ne, d = u_bf.shape
    t = hn_t.shape[1]
    nh, nk, _ = s2.shape
    once = pl.Buffered(1)
    tab_spec = pl.BlockSpec((nh, nk, tb), lambda i, e: (0, 0, i), pipeline_mode=once)
    row_spec = pl.BlockSpec((eb // nk, nh, tb), lambda i, e: (e, 0, i))
    return pl.pallas_call(
        _peer_kernel,
        grid=(t // tb, ne // eb),
        in_specs=[
            pl.BlockSpec((eb, d), lambda i, e: (e, 0)),
            pl.BlockSpec((d, eb), lambda i, e: (0, e)),
            pl.BlockSpec((d, tb), lambda i, e: (0, i), pipeline_mode=once),
            row_spec, row_spec, tab_spec, tab_spec,
            pl.BlockSpec((nh, 1, tb), lambda i, e: (0, 0, i)),
        ],
        out_specs=pl.BlockSpec((d, tb), lambda i, e: (0, i)),
        out_shape=jax.ShapeDtypeStruct((d, t), F32),
        scratch_shapes=[pltpu.VMEM((eb, tb), F32), pltpu.VMEM((eb, tb), F32),
                        pltpu.VMEM((eb, tb), BF16)],
        compiler_params=pltpu.CompilerParams(
            dimension_semantics=("parallel", "arbitrary"), vmem_limit_bytes=PEER_VMEM_LIMIT),
        name="peer_experts",
    )(u_bf, v_t, hn_t, s1, c1, s2, e2, tau)


def _final_kernel(pt_ref, h_ref, g_ref, yp_ref, yt_ref, *, nprompt):
    i = pl.program_id(0)
    y = _rms(h_ref[...] + pt_ref[...].T, g_ref[...])

    @pl.when(i < nprompt)
    def _():
        yp_ref[...] = y

    @pl.when(i >= nprompt)
    def _():
        yt_ref[...] = y


def _final(peer_t, h, g, tp, tr):
    d, t = peer_t.shape
    nprompt = tp // tr
    return pl.pallas_call(
        functools.partial(_final_kernel, nprompt=nprompt),
        grid=(nprompt + 1,),
        in_specs=[
            pl.BlockSpec((d, tr), lambda i: (0, i)),
            pl.BlockSpec((tr, d), lambda i: (i, 0)),
            pl.BlockSpec((1, d), lambda i: (0, 0)),
        ],
        out_specs=[
            pl.BlockSpec((tr, d), lambda i: (jnp.minimum(i, nprompt - 1), 0)),
            pl.BlockSpec((tr, d), lambda i: (0, 0)),
        ],
        out_shape=[jax.ShapeDtypeStruct((tp, d), F32), jax.ShapeDtypeStruct((tr, d), F32)],
        compiler_params=_cparams("arbitrary"),
        name="final_norm",
    )(peer_t, h, g)


def _block_diag(w, ngrp):
    nslab, _, a, b = w.shape
    eye = jnp.eye(ngrp, dtype=w.dtype)
    full = w[:, :, :, None, :] * eye[None, :, None, :, None]
    return full.reshape(nslab, ngrp * a, ngrp * b)


def kernel(x_prompt, x_sample, state_s5_re, state_s5_im, state_gla, meta_tokens, norm_mix_g, w_in, s5_lam_re, s5_lam_im, s5_log_dt, s5_b_re, s5_b_im, s5_c_re, s5_c_im, s5_d, s5_w_glu, s5_b_glu, s5_norm_g, gla_w_gate2, gla_b_gate2, gla_norm_g, w_out, norm_ffn_g, peer_w_q, peer_keys, peer_u, peer_v, norm_final_g):
    nb, seq, d = x_prompt.shape
    ns = x_sample.shape[0]
    nm = meta_tokens.shape[0]
    depth = w_in.shape[0]
    assert depth == 1 and x_sample.shape[1] == 1
    tp = nb * seq
    assert tp % TAIL_ROWS == 0 and ns + nm <= TAIL_ROWS and ns % SUBLANES == 0
    assert seq % GLA_CHUNK == 0 and nm % SUBLANES == 0
    t_all = tp + TAIL_ROWS

    ngrp, nstate, gch = s5_b_re.shape[1:]
    d_ssm = ngrp * gch
    gps = LANES // gch
    nslab = ngrp // gps
    nh, dk, dv = state_gla.shape[2:]
    d_gla = nh * dv
    rank = gla_w_gate2.shape[1]
    q0 = d_ssm
    k0 = q0 + nh * dk
    v0 = k0 + nh * dk
    r0 = v0 + d_gla
    g0 = r0 + d_gla
    assert g0 + rank == w_in.shape[2] and d_ssm + d_gla == w_out.shape[1]

    x_p = x_prompt.reshape(tp, d)
    x_t = jnp.concatenate([x_sample.reshape(ns, d), meta_tokens,
                           jnp.zeros((TAIL_ROWS - ns - nm, d), F32)], axis=0)
    w_in_bf = w_in[0].astype(BF16)
    w_g1 = jnp.pad(w_in[0, :, g0:], ((0, 0), (0, LANES - rank))).astype(BF16)
    w_g2 = jnp.pad(gla_w_gate2[0], ((0, LANES - rank), (0, 0))).astype(BF16)
    lam_re = s5_lam_re[0].reshape(1, ngrp * nstate)
    lam_im = s5_lam_im[0].reshape(1, ngrp * nstate)
    logdt = jnp.repeat(s5_log_dt[0], nstate).reshape(1, ngrp * nstate)
    b_re4 = jnp.transpose(s5_b_re[0].reshape(nslab, gps, nstate, gch), (0, 1, 3, 2))
    b_im4 = jnp.transpose(s5_b_im[0].reshape(nslab, gps, nstate, gch), (0, 1, 3, 2))
    c_re4 = jnp.transpose(s5_c_re[0].reshape(nslab, gps, gch, nstate), (0, 1, 3, 2))
    c_im4 = jnp.transpose(s5_c_im[0].reshape(nslab, gps, gch, nstate), (0, 1, 3, 2))
    bre_bd = _block_diag(b_re4, gps)
    bim_bd = _block_diag(b_im4, gps)
    cre_bd = _block_diag(c_re4, gps)
    cim_bd = _block_diag(c_im4, gps)
    h0r = state_s5_re[0].reshape(ns, ngrp * nstate)
    h0i = state_s5_im[0].reshape(ns, ngrp * nstate)

    tm = _row_tile(t_all, (768, 512, 256))

    xn = _prenorm(x_p, x_t, norm_mix_g)
    proj, xg, u_bf = _in_proj(xn, w_in_bf, w_g1, peer_u[0], tm, PROJ_COLS, g0)

    y_raw, pr, pi_, sr, si = _s5(proj, h0r, h0i, lam_re, lam_im, logdt, bre_bd, bim_bd,
                                 cre_bd, cim_bd, s5_d, nb, seq, ns, nm)
    y_ssm = _glu_norm(y_raw, s5_w_glu[0].astype(BF16), s5_b_glu, s5_norm_g, TAIL_ROWS)

    og_p, gla_p = _gla_prompt(proj, xg, w_g2, gla_b_gate2, gla_norm_g, nb, seq, ns, nm,
                              nh, dk, dv, q0, k0, v0, r0)
    og_s, gla_s = _gla_sample(proj, xg, w_g2, gla_b_gate2, gla_norm_g, state_gla[0],
                              tp, ns, nh, dk, dv, q0, k0, v0, r0)
    og_t = jnp.concatenate([og_s.astype(BF16), jnp.zeros((TAIL_ROWS - ns, d_gla), BF16)], axis=0)

    w_o = w_out[0].astype(BF16)
    assert d_ssm == d_gla
    h = _out_proj(y_ssm, og_p, og_t, w_o, x_p, x_t, tm, PROJ_COLS)

    hn_t = _norm_t(h, norm_ffn_g, TAIL_ROWS)
    wq_t = _transpose_cast(peer_w_q[0], WT_ROWS, WT_COLS)
    s1, c1, s2, e2, tau, v_t = _route(wq_t, hn_t, peer_keys[0, :, 0], peer_keys[0, :, 1],
                                      peer_v[0], tm)
    peer_t = _peer(u_bf, v_t, hn_t, jnp.transpose(s1, (1, 0, 2)), jnp.transpose(c1, (1, 0, 2)),
                   s2, e2, tau, tm, EXPERT_BLOCK)
    y_p, y_t = _final(peer_t, h, norm_final_g.reshape(1, d), tp, TAIL_ROWS)

    y_prompt = y_p.reshape(nb, seq, d)
    y_sample = y_t[:ns].reshape(ns, 1, d)
    return (y_prompt, y_sample,
            pr.reshape(1, nb, ngrp, nstate), pi_.reshape(1, nb, ngrp, nstate), gla_p[None],
            sr.reshape(1, ns, ngrp, nstate), si.reshape(1, ns, ngrp, nstate), gla_s[None])
```

```python
import functools

import jax
import jax.numpy as jnp
from jax import lax
from jax.experimental import pallas as pl
from jax.experimental.pallas import tpu as pltpu

F32 = jnp.float32
BF16 = jnp.bfloat16

EPS = 1e-6
GLA_TAU = 16.0
GLA_CHUNK = 64
GLA_SUB = 16
GLA_SAMPLES_PER_STEP = 4
PEER_TOPK = 16
LANES = 128
SUBLANES = 8
TAIL_ROWS = 256
S5_ROW_CHUNK = 512
PROJ_COLS = 1024
MXU_DEPTH = 256
WT_ROWS = 2048
WT_COLS = 1024
VMEM_LIMIT = 56 * 1024 * 1024
PEER_VMEM_LIMIT = 60 * 1024 * 1024


def _cparams(*sem):
    return pltpu.CompilerParams(dimension_semantics=sem, vmem_limit_bytes=VMEM_LIMIT)


def _gelu(x):
    return 0.5 * x * (1.0 + jnp.tanh(0.7978845608028654 * (x + 0.044715 * (x * x * x))))


def _sigmoid(x):
    return 1.0 / (1.0 + jnp.exp(-x))


def _log_sigmoid(x):
    return jnp.minimum(x, 0.0) - jnp.log(1.0 + jnp.exp(-jnp.abs(x)))


def _row_tile(n, cands):
    for c in cands:
        if n % c == 0:
            return c
    raise ValueError(f"no row tile for {n}")


def _transpose_block(x_ref, o_ref):
    k = MXU_DEPTH
    eye = jnp.where(lax.broadcasted_iota(jnp.int32, (k, k), 0)
                    == lax.broadcasted_iota(jnp.int32, (k, k), 1), 1.0, 0.0).astype(BF16)
    for c in range(x_ref.shape[1] // k):
        xb = x_ref[:, c * k:(c + 1) * k].astype(BF16)
        o_ref[c * k:(c + 1) * k, :] = lax.dot_general(
            eye, xb, (((1,), (1,)), ((), ())), preferred_element_type=F32).astype(BF16)


def _transpose_cast_kernel(x_ref, o_ref):
    _transpose_block(x_ref, o_ref)


def _transpose_cast(x, rb, cb):
    r, c = x.shape
    return pl.pallas_call(
        _transpose_cast_kernel,
        grid=(r // rb, c // cb),
        in_specs=[pl.BlockSpec((rb, cb), lambda i, j: (i, j))],
        out_specs=pl.BlockSpec((cb, rb), lambda i, j: (j, i)),
        out_shape=jax.ShapeDtypeStruct((c, r), BF16),
        compiler_params=_cparams("parallel", "parallel"),
        name="transpose_cast",
    )(x)


def _rms(x, g):
    return x * lax.rsqrt(jnp.mean(x * x, axis=-1, keepdims=True) + EPS) * g


def _prenorm_kernel(xp_ref, xt_ref, g_ref, o_ref, *, nprompt):
    i = pl.program_id(0)

    @pl.when(i < nprompt)
    def _():
        o_ref[...] = _rms(xp_ref[...], g_ref[...]).astype(BF16)

    @pl.when(i >= nprompt)
    def _():
        o_ref[...] = _rms(xt_ref[...], g_ref[...]).astype(BF16)


def _prenorm(x_p, x_t, g):
    tp, d = x_p.shape
    tr = x_t.shape[0]
    nprompt = tp // tr
    return pl.pallas_call(
        functools.partial(_prenorm_kernel, nprompt=nprompt),
        grid=(nprompt + 1,),
        in_specs=[
            pl.BlockSpec((tr, d), lambda i: (jnp.minimum(i, nprompt - 1), 0)),
            pl.BlockSpec((tr, d), lambda i: (0, 0)),
            pl.BlockSpec((1, d), lambda i: (0, 0)),
        ],
        out_specs=pl.BlockSpec((tr, d), lambda i: (i, 0)),
        out_shape=jax.ShapeDtypeStruct((tp + tr, d), BF16),
        compiler_params=_cparams("arbitrary"),
        name="mix_norm",
    )(x_p, x_t, g)


def _in_proj_kernel(xn_ref, w_ref, wg_ref, side_ref, o_ref, og_ref, side_o_ref, *, nside):
    nj = pl.num_programs(1)

    @pl.when(pl.program_id(1) == 0)
    def _():
        og_ref[...] = jnp.dot(xn_ref[...], wg_ref[...], preferred_element_type=F32)

    o_ref[...] = jnp.dot(xn_ref[...], w_ref[...], preferred_element_type=F32)

    @pl.when(pl.program_id(0) * nj + pl.program_id(1) < nside)
    def _():
        side_o_ref[...] = side_ref[...].astype(BF16)


def _in_proj(xn, w, wg, side, tm, tn, n):
    t, d = xn.shape
    ni, nj = t // tm, n // tn
    rs, cs = side.shape
    nside = max(k for k in range(1, ni * nj + 1) if rs % k == 0 and (rs // k) % SUBLANES == 0)
    rb = rs // nside
    side_idx = lambda i, j: (jnp.minimum(i * nj + j, nside - 1), 0)
    return pl.pallas_call(
        functools.partial(_in_proj_kernel, nside=nside),
        grid=(ni, nj),
        in_specs=[
            pl.BlockSpec((tm, d), lambda i, j: (i, 0)),
            pl.BlockSpec((d, tn), lambda i, j: (0, j)),
            pl.BlockSpec((d, LANES), lambda i, j: (0, 0)),
            pl.BlockSpec((rb, cs), side_idx),
        ],
        out_specs=[
            pl.BlockSpec((tm, tn), lambda i, j: (i, j)),
            pl.BlockSpec((tm, LANES), lambda i, j: (i, 0)),
            pl.BlockSpec((rb, cs), side_idx),
        ],
        out_shape=[jax.ShapeDtypeStruct((t, n), F32), jax.ShapeDtypeStruct((t, LANES), F32),
                   jax.ShapeDtypeStruct((rs, cs), BF16)],
        compiler_params=_cparams("arbitrary", "arbitrary"),
        name="in_proj",
    )(xn, w, wg, side)


def _s5_kernel(u_ref, h0r_ref, h0i_ref, lr_ref, li_ref, ldt_ref, bre_ref, bim_ref,
               cre_ref, cim_ref, d_ref,
               y_ref, pr_ref, pi_ref, sr_ref, si_ref,
               x_scr, init_scr, up_scr, yp_scr, *, nb, seq, ns, nm):
    sw = lr_ref.shape[1]
    tp = nb * seq
    nseg = SUBLANES
    ls = seq // nseg

    lr = lr_ref[...]
    li = li_ref[...]
    dt = jnp.exp(ldt_ref[...])
    mag = jnp.exp(lr * dt)
    ang = li * dt
    ar = mag * jnp.cos(ang)
    ai = mag * jnp.sin(ang)
    den = lr * lr + li * li
    nr = ar - 1.0
    qr = (nr * lr + ai * li) / den
    qi = (ai * lr - nr * li) / den
    bre = bre_ref[...]
    bim = bim_ref[...]
    bcat = jnp.concatenate([qr * bre - qi * bim, qr * bim + qi * bre], axis=1).astype(BF16)
    ccat = jnp.concatenate([cre_ref[...], -cim_ref[...]], axis=0).astype(BF16)
    dvec = d_ref[...]
    mag_s = jnp.exp(lr * dt * float(ls))
    asr = mag_s * jnp.cos(ang * float(ls))
    asi = mag_s * jnp.sin(ang * float(ls))

    def project_out(h, u):
        return jnp.dot(h.astype(BF16), ccat, preferred_element_type=F32) + dvec * u

    u_s = u_ref[tp:tp + ns, :]
    x_s = jnp.dot(u_s.astype(BF16), bcat, preferred_element_type=F32)
    h0r = h0r_ref[...]
    h0i = h0i_ref[...]
    hr_s = ar * h0r - ai * h0i + x_s[:, :sw]
    hi_s = ar * h0i + ai * h0r + x_s[:, sw:]
    sr_ref[...] = hr_s
    si_ref[...] = hi_s
    y_ref[tp:tp + ns, :] = project_out(jnp.concatenate([hr_s, hi_s], axis=1), u_s)
    y_ref[tp + ns:, :] = jnp.zeros((y_ref.shape[0] - tp - ns, y_ref.shape[1]), F32)

    u_m = u_ref[tp + ns:tp + ns + nm, :]
    x_m = jnp.dot(u_m.astype(BF16), bcat, preferred_element_type=F32)
    mr = jnp.zeros((1, sw), F32)
    mi = jnp.zeros((1, sw), F32)
    for t in range(nm):
        mr, mi = (ar * mr - ai * mi + x_m[t:t + 1, :sw],
                  ar * mi + ai * mr + x_m[t:t + 1, sw:])

    ar8 = jnp.broadcast_to(ar, (nseg, sw))
    ai8 = jnp.broadcast_to(ai, (nseg, sw))
    unroll = 4 if ls % 4 == 0 else 1

    def scan_pass(store):
        def body(i, c):
            hr, hi = c
            r = pl.multiple_of(i * nseg, nseg)
            nhr = ar8 * hr - ai8 * hi + x_scr[pl.ds(r, nseg), 0:sw]
            nhi = ar8 * hi + ai8 * hr + x_scr[pl.ds(r, nseg), sw:2 * sw]
            if store:
                x_scr[pl.ds(r, nseg), 0:sw] = nhr
                x_scr[pl.ds(r, nseg), sw:2 * sw] = nhi
            return nhr, nhi
        return body

    for b in range(nb):
        def regroup_in(i, c, b=b):
            r = pl.multiple_of(i * nseg, nseg)
            up_scr[pl.ds(r, nseg), :] = u_ref[pl.ds(b * seq + i, nseg, stride=ls), :]
            return c
        lax.fori_loop(0, ls, regroup_in, 0, unroll=unroll)
        rc = min(S5_ROW_CHUNK, seq)
        for c0 in range(0, seq, rc):
            x_scr[c0:c0 + rc, :] = jnp.dot(up_scr[c0:c0 + rc, :].astype(BF16), bcat,
                                           preferred_element_type=F32)
        z = jnp.zeros((nseg, sw), F32)
        fr, fi = lax.fori_loop(0, ls, scan_pass(False), (z, z), unroll=unroll)
        cr, ci = mr, mi
        for k in range(nseg):
            init_scr[k:k + 1, 0:sw] = cr
            init_scr[k:k + 1, sw:2 * sw] = ci
            cr, ci = (asr * cr - asi * ci + fr[k:k + 1, :],
                      asr * ci + asi * cr + fi[k:k + 1, :])
        pr_ref[b:b + 1, :] = cr
        pi_ref[b:b + 1, :] = ci
        lax.fori_loop(0, ls, scan_pass(True), (init_scr[:, 0:sw], init_scr[:, sw:2 * sw]),
                      unroll=unroll)
        for c0 in range(0, seq, rc):
            yp_scr[c0:c0 + rc, :] = project_out(x_scr[c0:c0 + rc, :], up_scr[c0:c0 + rc, :])

        def regroup_out(i, c, b=b):
            r = pl.multiple_of(i * nseg, nseg)
            y_ref[pl.ds(b * seq + i, nseg, stride=ls), :] = yp_scr[pl.ds(r, nseg), :]
            return c
        lax.fori_loop(0, ls, regroup_out, 0, unroll=unroll)


def _s5(proj, h0r, h0i, lam_re, lam_im, logdt, bre_bd, bim_bd, cre_bd, cim_bd, dvec,
        nb, seq, ns, nm):
    t_all = proj.shape[0]
    nslab, cw, sw = bre_bd.shape
    d_ssm = nslab * cw
    kern = functools.partial(_s5_kernel, nb=nb, seq=seq, ns=ns, nm=nm)
    return pl.pallas_call(
        kern,
        grid=(nslab,),
        in_specs=[
            pl.BlockSpec((t_all, cw), lambda s: (0, s)),
            pl.BlockSpec((ns, sw), lambda s: (0, s)),
            pl.BlockSpec((ns, sw), lambda s: (0, s)),
            pl.BlockSpec((1, sw), lambda s: (0, s)),
            pl.BlockSpec((1, sw), lambda s: (0, s)),
            pl.BlockSpec((1, sw), lambda s: (0, s)),
            pl.BlockSpec((None, cw, sw), lambda s: (s, 0, 0)),
            pl.BlockSpec((None, cw, sw), lambda s: (s, 0, 0)),
            pl.BlockSpec((None, sw, cw), lambda s: (s, 0, 0)),
            pl.BlockSpec((None, sw, cw), lambda s: (s, 0, 0)),
            pl.BlockSpec((1, cw), lambda s: (0, s)),
        ],
        out_specs=[
            pl.BlockSpec((t_all, cw), lambda s: (0, s)),
            pl.BlockSpec((nb, sw), lambda s: (0, s)),
            pl.BlockSpec((nb, sw), lambda s: (0, s)),
            pl.BlockSpec((ns, sw), lambda s: (0, s)),
            pl.BlockSpec((ns, sw), lambda s: (0, s)),
        ],
        out_shape=[
            jax.ShapeDtypeStruct((t_all, d_ssm), F32),
            jax.ShapeDtypeStruct((nb, nslab * sw), F32),
            jax.ShapeDtypeStruct((nb, nslab * sw), F32),
            jax.ShapeDtypeStruct((ns, nslab * sw), F32),
            jax.ShapeDtypeStruct((ns, nslab * sw), F32),
        ],
        scratch_shapes=[pltpu.VMEM((seq, 2 * sw), F32), pltpu.VMEM((SUBLANES, 2 * sw), F32),
                        pltpu.VMEM((seq, cw), F32), pltpu.VMEM((seq, cw), F32)],
        compiler_params=_cparams("parallel"),
        name="s5_scan",
    )(proj, h0r, h0i, lam_re, lam_im, logdt, bre_bd, bim_bd, cre_bd, cim_bd, dvec)


def _glu_norm_kernel(y_ref, w_ref, b_ref, g_ref, o_ref):
    z = _gelu(y_ref[...])
    gate = jnp.dot(z.astype(BF16), w_ref[...], preferred_element_type=F32) + b_ref[...]
    zz = z * _sigmoid(gate)
    s = lax.rsqrt(jnp.mean(zz * zz, axis=-1, keepdims=True) + EPS)
    o_ref[...] = (zz * s * g_ref[...]).astype(o_ref.dtype)


def _glu_norm(y, w, b, g, tm):
    t, d = y.shape
    return pl.pallas_call(
        _glu_norm_kernel,
        grid=(t // tm,),
        in_specs=[
            pl.BlockSpec((tm, d), lambda i: (i, 0)),
            pl.BlockSpec((d, d), lambda i: (0, 0)),
            pl.BlockSpec((1, d), lambda i: (0, 0)),
            pl.BlockSpec((1, d), lambda i: (0, 0)),
        ],
        out_specs=pl.BlockSpec((tm, d), lambda i: (i, 0)),
        out_shape=jax.ShapeDtypeStruct((t, d), BF16),
        compiler_params=_cparams("parallel"),
        name="s5_glu_norm",
    )(y, w, b, g)


def _cumsum_rows(x):
    n = x.shape[0]
    row = lax.broadcasted_iota(jnp.int32, x.shape, 0)
    s = 1
    while s < n:
        x = x + jnp.where(row >= s, pltpu.roll(x, s, axis=0), 0.0)
        s *= 2
    return x


def _gate(xg, w2, b2):
    pre = jnp.dot(xg.astype(BF16), w2, preferred_element_type=F32) + b2
    return _log_sigmoid(pre) * (1.0 / GLA_TAU)


def _head_out(o, r, g):
    o = o * lax.rsqrt(jnp.mean(o * o, axis=-1, keepdims=True) + EPS)
    return o * g * (r * _sigmoid(r))


def _gla_prompt_kernel(q_ref, k_ref, v_ref, r_ref, xg_ref, kt_ref, vt_ref, xgt_ref,
                       w2_ref, b2_ref, g_ref, o_ref, s_ref, st_scr, *, seq, ns, nm):
    dk = q_ref.shape[1]
    scale = float(dk) ** -0.5
    w2 = w2_ref[...]
    b2 = b2_ref[...]
    g = g_ref[...]

    lgm = _gate(xgt_ref[ns:ns + nm, :], w2, b2)
    bm = _cumsum_rows(lgm)
    kdm = kt_ref[ns:ns + nm, :] * jnp.exp(bm[nm - 1:nm, :] - bm)
    st_scr[...] = lax.dot_general(vt_ref[ns:ns + nm, :].astype(BF16), kdm.astype(BF16),
                                  (((0,), (0,)), ((), ())), preferred_element_type=F32)

    c = GLA_CHUNK
    nsub = c // GLA_SUB

    def chunk(ci, carry):
        r0 = pl.multiple_of(ci * c, c)
        q = q_ref[pl.ds(r0, c), :] * scale
        k = k_ref[pl.ds(r0, c), :]
        v = v_ref[pl.ds(r0, c), :]
        vb = v.astype(BF16)
        lg = _gate(xg_ref[pl.ds(r0, c), :], w2, b2)
        b = _cumsum_rows(lg)
        bl = b[c - 1:c, :]
        st = st_scr[...]
        o_inter = lax.dot_general((q * jnp.exp(b)).astype(BF16), st.astype(BF16),
                                  (((1,), (1,)), ((), ())), preferred_element_type=F32)
        outs = []
        for sb in range(nsub):
            lo = sb * GLA_SUB
            hi = lo + GLA_SUB
            beta = b[lo - 1:lo, :] if sb > 0 else jnp.zeros((1, dk), F32)
            qs = q[lo:hi, :] * jnp.exp(b[lo:hi, :] - beta)
            ks = k[0:hi, :] * jnp.exp(beta - b[0:hi, :])
            sc = lax.dot_general(qs.astype(BF16), ks.astype(BF16),
                                 (((1,), (1,)), ((), ())), preferred_element_type=F32)
            rowi = lax.broadcasted_iota(jnp.int32, (GLA_SUB, hi), 0)
            coli = lax.broadcasted_iota(jnp.int32, (GLA_SUB, hi), 1)
            sc = jnp.where(coli <= rowi + lo, sc, 0.0)
            outs.append(jnp.dot(sc.astype(BF16), vb[0:hi, :], preferred_element_type=F32))
        o = o_inter + jnp.concatenate(outs, axis=0)
        o_ref[pl.ds(r0, c), :] = _head_out(o, r_ref[pl.ds(r0, c), :], g).astype(o_ref.dtype)
        kd = k * jnp.exp(bl - b)
        st_scr[...] = jnp.exp(bl) * st + lax.dot_general(
            vb, kd.astype(BF16), (((0,), (0,)), ((), ())), preferred_element_type=F32)
        return carry

    lax.fori_loop(0, seq // c, chunk, 0, unroll=min(8, seq // c))
    s_ref[...] = st_scr[...].T


def _gla_prompt(proj, xg, w2, b2, g, nb, seq, ns, nm, nh, dk, dv, q0, k0, v0, r0):
    tp = nb * seq
    tail_blk = tp // TAIL_ROWS
    rb = seq
    kern = functools.partial(_gla_prompt_kernel, seq=seq, ns=ns, nm=nm)
    return pl.pallas_call(
        kern,
        grid=(nb, nh),
        in_specs=[
            pl.BlockSpec((rb, dk), lambda b, h: (b, q0 // dk + h)),
            pl.BlockSpec((rb, dk), lambda b, h: (b, k0 // dk + h)),
            pl.BlockSpec((rb, dv), lambda b, h: (b, v0 // dv + h)),
            pl.BlockSpec((rb, dv), lambda b, h: (b, r0 // dv + h)),
            pl.BlockSpec((rb, LANES), lambda b, h: (b, 0)),
            pl.BlockSpec((TAIL_ROWS, dk), lambda b, h: (tail_blk, k0 // dk + h)),
            pl.BlockSpec((TAIL_ROWS, dv), lambda b, h: (tail_blk, v0 // dv + h)),
            pl.BlockSpec((TAIL_ROWS, LANES), lambda b, h: (tail_blk, 0)),
            pl.BlockSpec((LANES, dk), lambda b, h: (0, h)),
            pl.BlockSpec((1, dk), lambda b, h: (0, h)),
            pl.BlockSpec((1, dv), lambda b, h: (0, h)),
        ],
        out_specs=[
            pl.BlockSpec((rb, dv), lambda b, h: (b, h)),
            pl.BlockSpec((None, None, dk, dv), lambda b, h: (b, h, 0, 0)),
        ],
        out_shape=[
            jax.ShapeDtypeStruct((tp, nh * dv), BF16),
            jax.ShapeDtypeStruct((nb, nh, dk, dv), F32),
        ],
        scratch_shapes=[pltpu.VMEM((dv, dk), F32)],
        compiler_params=_cparams("parallel", "parallel"),
        name="gla_prompt",
    )(proj, proj, proj, proj, xg, proj, proj, xg, w2, b2, g)


def _gla_sample_kernel(q_ref, k_ref, v_ref, r_ref, xg_ref, w2_ref, b2_ref, g_ref, s0_ref,
                       o_ref, s_ref, *, nh, dk, dv, sps):
    for si in range(sps):
        _gla_sample_one(q_ref, k_ref, v_ref, r_ref, xg_ref, w2_ref, b2_ref, g_ref, s0_ref,
                        o_ref, s_ref, si, nh=nh, dk=dk, dv=dv, sps=sps)


def _gla_sample_one(q_ref, k_ref, v_ref, r_ref, xg_ref, w2_ref, b2_ref, g_ref, s0_ref,
                    o_ref, s_ref, si, *, nh, dk, dv, sps):
    n = pl.program_id(0) * sps + si
    j = n % SUBLANES
    scale = float(dk) ** -0.5
    lg_all = _gate(xg_ref[pl.ds(j, 1), :], w2_ref[...], b2_ref[...])
    q_all = q_ref[pl.ds(j, 1), :] * scale
    k_all = k_ref[pl.ds(j, 1), :]
    v_all = v_ref[pl.ds(j, 1), :]
    r_all = r_ref[pl.ds(j, 1), :]
    g_all = g_ref[...]
    rows = 2 * SUBLANES
    rk = lax.broadcasted_iota(jnp.int32, (rows, dk), 0)
    rv = lax.broadcasted_iota(jnp.int32, (rows, dv), 0)
    outs = []
    for h in range(nh):
        lg = lg_all[:, h * dk:(h + 1) * dk]
        q = q_all[:, h * dk:(h + 1) * dk]
        k = k_all[:, h * dk:(h + 1) * dk]
        v = v_all[:, h * dv:(h + 1) * dv]
        e = jnp.exp(lg)
        s0 = s0_ref[si, h]
        e_hi = e.astype(BF16)
        e_mid = (e - e_hi.astype(F32)).astype(BF16)
        e_lo = (e - e_hi.astype(F32) - e_mid.astype(F32)).astype(BF16)
        def rows_of(x, w):
            return jnp.broadcast_to(x.astype(F32), (rows, w))
        lhs = jnp.where(rk == 0, rows_of(e_hi, dk),
              jnp.where(rk == 1, rows_of(e_mid, dk),
              jnp.where(rk == 2, rows_of(e_lo, dk),
              jnp.where(rk == 3, rows_of(k, dk), 0.0)))).astype(BF16)
        ones_part = jnp.where(rv < 3, 1.0, 0.0)
        v_part = jnp.where(rv == 3, rows_of(v, dv), 0.0)
        rhs = jnp.concatenate([ones_part, v_part], axis=1).astype(BF16)
        both = lax.dot_general(lhs, rhs, (((0,), (0,)), ((), ())), preferred_element_type=F32)
        s_ref[si, h] = both[:, :dv] * s0 + both[:, dv:]
        qe = jnp.broadcast_to((q * e).astype(BF16), (rows, dk))
        o = jnp.dot(qe, s0.astype(BF16), preferred_element_type=F32)[0:1, :]
        o = o + jnp.sum(q * k, axis=-1, keepdims=True) * v
        outs.append(_head_out(o, r_all[:, h * dv:(h + 1) * dv], g_all[:, h * dv:(h + 1) * dv]))
    o_ref[pl.ds(j, 1), :] = jnp.concatenate(outs, axis=1)


def _gla_sample(proj, xg, w2, b2, g, s0, tp, ns, nh, dk, dv, q0, k0, v0, r0):
    rb = SUBLANES
    base = tp // rb
    qk_w = nh * dk
    v_w = nh * dv
    sps = GLA_SAMPLES_PER_STEP
    assert ns % sps == 0 and rb % sps == 0
    kern = functools.partial(_gla_sample_kernel, nh=nh, dk=dk, dv=dv, sps=sps)
    row = lambda n: base + (n * sps) // rb
    return pl.pallas_call(
        kern,
        grid=(ns // sps,),
        in_specs=[
            pl.BlockSpec((rb, qk_w), lambda n: (row(n), q0 // qk_w)),
            pl.BlockSpec((rb, qk_w), lambda n: (row(n), k0 // qk_w)),
            pl.BlockSpec((rb, v_w), lambda n: (row(n), v0 // v_w)),
            pl.BlockSpec((rb, v_w), lambda n: (row(n), r0 // v_w)),
            pl.BlockSpec((rb, LANES), lambda n: (row(n), 0)),
            pl.BlockSpec((LANES, qk_w), lambda n: (0, 0)),
            pl.BlockSpec((1, qk_w), lambda n: (0, 0)),
            pl.BlockSpec((1, v_w), lambda n: (0, 0)),
            pl.BlockSpec((sps, nh, dk, dv), lambda n: (n, 0, 0, 0)),
        ],
        out_specs=[
            pl.BlockSpec((rb, v_w), lambda n: ((n * sps) // rb, 0)),
            pl.BlockSpec((sps, nh, dk, dv), lambda n: (n, 0, 0, 0)),
        ],
        out_shape=[
            jax.ShapeDtypeStruct((ns, v_w), F32),
            jax.ShapeDtypeStruct((ns, nh, dk, dv), F32),
        ],
        compiler_params=_cparams("arbitrary"),
        name="gla_sample",
    )(proj, proj, proj, proj, xg, w2, b2, g, s0)


def _out_proj_kernel(ys_ref, ogp_ref, ogt_ref, w1_ref, w2_ref, xp_ref, xt_ref, h_ref,
                     *, nfull, split):
    i = pl.program_id(0)
    w2 = w2_ref[...]
    acc = jnp.dot(ys_ref[...], w1_ref[...], preferred_element_type=F32)

    @pl.when(i < nfull)
    def _():
        h_ref[...] = xp_ref[...] + acc + jnp.dot(ogp_ref[...], w2, preferred_element_type=F32)

    @pl.when(i >= nfull)
    def _():
        if split:
            h_ref[0:split, :] = xp_ref[0:split, :] + acc[0:split] + jnp.dot(
                ogp_ref[0:split, :], w2, preferred_element_type=F32)
        h_ref[split:, :] = xt_ref[...] + acc[split:] + jnp.dot(
            ogt_ref[...], w2, preferred_element_type=F32)


def _out_proj(ys, og_p, og_t, w, x_p, x_t, tm, tn):
    t, dh = ys.shape
    d = w.shape[1]
    tp, tr = x_p.shape[0], x_t.shape[0]
    nfull = tp // tm
    split = tp - nfull * tm
    assert t == tp + tr and t == (nfull + 1) * tm and tm - split == tr
    last = pl.cdiv(tp, tm) - 1
    return pl.pallas_call(
        functools.partial(_out_proj_kernel, nfull=nfull, split=split),
        grid=(t // tm, d // tn),
        in_specs=[
            pl.BlockSpec((tm, dh), lambda i, j: (i, 0)),
            pl.BlockSpec((tm, dh), lambda i, j: (jnp.minimum(i, last), 0)),
            pl.BlockSpec((tr, dh), lambda i, j: (0, 0)),
            pl.BlockSpec((dh, tn), lambda i, j: (0, j)),
            pl.BlockSpec((dh, tn), lambda i, j: (1, j)),
            pl.BlockSpec((tm, tn), lambda i, j: (jnp.minimum(i, last), j)),
            pl.BlockSpec((tr, tn), lambda i, j: (0, j)),
        ],
        out_specs=pl.BlockSpec((tm, tn), lambda i, j: (i, j)),
        out_shape=jax.ShapeDtypeStruct((t, d), F32),
        compiler_params=_cparams("parallel", "parallel"),
        name="out_proj",
    )(ys, og_p, og_t, w, w, x_p, x_t)


def _norm_t_kernel(h_ref, g_ref, o_ref):
    o_ref[...] = _rms(h_ref[...], g_ref[...]).T.astype(BF16)


def _norm_t(h, g, tm):
    t, d = h.shape
    return pl.pallas_call(
        _norm_t_kernel,
        grid=(t // tm,),
        in_specs=[pl.BlockSpec((tm, d), lambda i: (i, 0)), pl.BlockSpec((1, d), lambda i: (0, 0))],
        out_specs=pl.BlockSpec((d, tm), lambda i: (0, i)),
        out_shape=jax.ShapeDtypeStruct((d, t), BF16),
        compiler_params=_cparams("parallel"),
        name="ffn_norm_t",
    )(h, g)


def _topk_rows(x, k):
    outs = []
    for _ in range(k):
        m = jnp.max(x, axis=0, keepdims=True)
        outs.append(m)
        x = jnp.where(x == m, -jnp.inf, x)
    return outs


def _route_kernel(wq_ref, hn_ref, k1_ref, k2_ref, side_ref,
                  s1_ref, c1_ref, s2_ref, e2_ref, tau_ref, side_o_ref, *, nside):
    @pl.when(pl.program_id(0) * pl.num_programs(1) + pl.program_id(1) < nside)
    def _():
        _transpose_block(side_ref, side_o_ref)

    half = k1_ref.shape[1]
    tb = hn_ref.shape[1]
    qt = jnp.dot(wq_ref[...], hn_ref[...], preferred_element_type=F32)
    s1 = jnp.dot(k1_ref[...].astype(BF16), qt[:half, :].astype(BF16), preferred_element_type=F32)
    s2 = jnp.dot(k2_ref[...].astype(BF16), qt[half:, :].astype(BF16), preferred_element_type=F32)
    s1_ref[...] = s1
    s2_ref[...] = s2
    for t in range(tb // LANES):
        sl = slice(t * LANES, (t + 1) * LANES)
        a = s1[:, sl]
        b = s2[:, sl]
        kk = PEER_TOPK
        v1 = _topk_rows(a, kk)
        v2l = _topk_rows(b, kk)
        v2 = jnp.concatenate(v2l, axis=0)
        cand = jnp.concatenate(
            [v1[0] + v2]
            + [v1[i] + v2[:kk // 2] for i in range(1, kk // 2)]
            + [jnp.concatenate(v1[kk // 2:], axis=0) + v2l[0]], axis=0)
        top = _topk_rows(cand, kk)
        z = jnp.zeros_like(top[0])
        for c in top:
            z = z + jnp.exp(c - top[0])
        tau_ref[:, sl] = top[PEER_TOPK - 1]
        c1_ref[:, sl] = jnp.exp(a - v1[0]) / z
        e2_ref[:, sl] = jnp.exp(b - v2[0:1, :])


def _route(wq_t, hn_t, k1, k2, side, tb):
    d, t = hn_t.shape
    nh, nk, half = k1.shape
    qd = 2 * half
    ni = t // tb
    rs, cs = side.shape
    nside = max(k for k in range(1, ni * nh + 1) if rs % k == 0 and (rs // k) % MXU_DEPTH == 0)
    rb = rs // nside
    tab = jax.ShapeDtypeStruct((nh, nk, t), F32)
    tab_spec = pl.BlockSpec((None, nk, tb), lambda i, h: (h, 0, i))
    return pl.pallas_call(
        functools.partial(_route_kernel, nside=nside),
        grid=(ni, nh),
        in_specs=[
            pl.BlockSpec((qd, d), lambda i, h: (h, 0)),
            pl.BlockSpec((d, tb), lambda i, h: (0, i)),
            pl.BlockSpec((None, nk, half), lambda i, h: (h, 0, 0)),
            pl.BlockSpec((None, nk, half), lambda i, h: (h, 0, 0)),
            pl.BlockSpec((rb, cs), lambda i, h: (jnp.minimum(i * nh + h, nside - 1), 0)),
        ],
        out_specs=[tab_spec, tab_spec, tab_spec, tab_spec,
                   pl.BlockSpec((None, 1, tb), lambda i, h: (h, 0, i)),
                   pl.BlockSpec((cs, rb), lambda i, h: (0, jnp.minimum(i * nh + h, nside - 1)))],
        out_shape=[tab, tab, tab, tab, jax.ShapeDtypeStruct((nh, 1, t), F32),
                   jax.ShapeDtypeStruct((cs, rs), BF16)],
        compiler_params=_cparams("arbitrary", "arbitrary"),
        name="peer_route",
    )(wq_t, hn_t, k1, k2, side)


GATE_ROWS = 32
OUT_ROWS = 512


K_CHUNK = 512
EXPERT_BLOCK = 512


def _peer_kernel(u_ref, vt_ref, hn_ref, s1_ref, c1_ref, s2_ref, e2_ref, tau_ref, o_ref,
                 w_scr, act_scr, a_scr):
    eb, d = u_ref.shape
    tb = hn_ref.shape[1]
    nh, nk, _ = s2_ref.shape
    nr = eb // nk

    @pl.when(pl.program_id(1) == 0)
    def _():
        o_ref[...] = jnp.zeros_like(o_ref)

    def gate_group(t, q):
        sl = slice(t * LANES, (t + 1) * LANES)
        rows = slice(q * GATE_ROWS, (q + 1) * GATE_ROWS)
        accs = [None] * nr
        for h in range(nh):
            s2t = s2_ref[h, rows, sl]
            e2t = e2_ref[h, rows, sl]
            tau = tau_ref[h, :, sl]
            for r in range(nr):
                ssum = s1_ref[r, h:h + 1, sl] + s2t
                term = jnp.where(ssum >= tau, c1_ref[r, h:h + 1, sl] * e2t, 0.0)
                accs[r] = term if accs[r] is None else accs[r] + term
        for r in range(nr):
            lo = r * nk + q * GATE_ROWS
            w_scr[lo:lo + GATE_ROWS, sl] = accs[r]

    groups = [(t, q) for t in range(tb // LANES) for q in range(nk // GATE_ROWS)]
    nkc = d // K_CHUNK
    per = -(-len(groups) // nkc)
    for kc in range(nkc):
        @pl.when(pl.program_id(1) >= 0)
        def _(kc=kc):
            ks = slice(kc * K_CHUNK, (kc + 1) * K_CHUNK)
            part = jnp.dot(u_ref[:, ks], hn_ref[ks, :], preferred_element_type=F32)
            if kc == 0:
                act_scr[...] = part
            else:
                act_scr[...] += part
            for t, q in groups[kc * per:(kc + 1) * per]:
                gate_group(t, q)
    a_scr[...] = (w_scr[...] * _gelu(act_scr[...])).astype(BF16)
    for dc in range(d // OUT_ROWS):
        dr = slice(dc * OUT_ROWS, (dc + 1) * OUT_ROWS)
        o_ref[dr, :] += jnp.dot(vt_ref[dr, :], a_scr[...], preferred_element_type=F32)


def _peer(u_bf, v_t, hn_t, s1, c1, s2, e2, tau, tb, eb):
    ne, d = u_bf.shape
    t = hn_t.shape[1]
    nh, nk, _ = s2.shape
    once = pl.Buffered(1)
    tab_spec = pl.BlockSpec((nh, nk, tb), lambda i, e: (0, 0, i), pipeline_mode=once)
    row_spec = pl.BlockSpec((eb // nk, nh, tb), lambda i, e: (e, 0, i))
    return pl.pallas_call(
        _peer_kernel,
        grid=(t // tb, ne // eb),
        in_specs=[
            pl.BlockSpec((eb, d), lambda i, e: (e, 0)),
            pl.BlockSpec((d, eb), lambda i, e: (0, e)),
            pl.BlockSpec((d, tb), lambda i, e: (0, i), pipeline_mode=once),
            row_spec, row_spec, tab_spec, tab_spec,
            pl.BlockSpec((nh, 1, tb), lambda i, e: (0, 0, i)),
        ],
        out_specs=pl.BlockSpec((d, tb), lambda i, e: (0, i)),
        out_shape=jax.ShapeDtypeStruct((d, t), F32),
        scratch_shapes=[pltpu.VMEM((eb, tb), F32), pltpu.VMEM((eb, tb), F32),
                        pltpu.VMEM((eb, tb), BF16)],
        compiler_params=pltpu.CompilerParams(
            dimension_semantics=("parallel", "arbitrary"), vmem_limit_bytes=PEER_VMEM_LIMIT),
        name="peer_experts",
    )(u_bf, v_t, hn_t, s1, c1, s2, e2, tau)


def _final_kernel(pt_ref, h_ref, g_ref, yp_ref, yt_ref, *, nprompt):
    i = pl.program_id(0)
    y = _rms(h_ref[...] + pt_ref[...].T, g_ref[...])

    @pl.when(i < nprompt)
    def _():
        yp_ref[...] = y

    @pl.when(i >= nprompt)
    def _():
        yt_ref[...] = y


def _final(peer_t, h, g, tp, tr):
    d, t = peer_t.shape
    nprompt = tp // tr
    return pl.pallas_call(
        functools.partial(_final_kernel, nprompt=nprompt),
        grid=(nprompt + 1,),
        in_specs=[
            pl.BlockSpec((d, tr), lambda i: (0, i)),
            pl.BlockSpec((tr, d), lambda i: (i, 0)),
            pl.BlockSpec((1, d), lambda i: (0, 0)),
        ],
        out_specs=[
            pl.BlockSpec((tr, d), lambda i: (jnp.minimum(i, nprompt - 1), 0)),
            pl.BlockSpec((tr, d), lambda i: (0, 0)),
        ],
        out_shape=[jax.ShapeDtypeStruct((tp, d), F32), jax.ShapeDtypeStruct((tr, d), F32)],
        compiler_params=_cparams("arbitrary"),
        name="final_norm",
    )(peer_t, h, g)


def _block_diag(w, ngrp):
    nslab, _, a, b = w.shape
    eye = jnp.eye(ngrp, dtype=w.dtype)
    full = w[:, :, :, None, :] * eye[None, :, None, :, None]
    return full.reshape(nslab, ngrp * a, ngrp * b)


def kernel(x_prompt, x_sample, state_s5_re, state_s5_im, state_gla, meta_tokens, norm_mix_g, w_in, s5_lam_re, s5_lam_im, s5_log_dt, s5_b_re, s5_b_im, s5_c_re, s5_c_im, s5_d, s5_w_glu, s5_b_glu, s5_norm_g, gla_w_gate2, gla_b_gate2, gla_norm_g, w_out, norm_ffn_g, peer_w_q, peer_keys, peer_u, peer_v, norm_final_g):
    nb, seq, d = x_prompt.shape
    ns = x_sample.shape[0]
    nm = meta_tokens.shape[0]
    depth = w_in.shape[0]
    assert depth == 1 and x_sample.shape[1] == 1
    tp = nb * seq
    assert tp % TAIL_ROWS == 0 and ns + nm <= TAIL_ROWS and ns % SUBLANES == 0
    assert seq % GLA_CHUNK == 0 and nm % SUBLANES == 0
    t_all = tp + TAIL_ROWS

    ngrp, nstate, gch = s5_b_re.shape[1:]
    d_ssm = ngrp * gch
    gps = LANES // gch
    nslab = ngrp // gps
    nh, dk, dv = state_gla.shape[2:]
    d_gla = nh * dv
    rank = gla_w_gate2.shape[1]
    q0 = d_ssm
    k0 = q0 + nh * dk
    v0 = k0 + nh * dk
    r0 = v0 + d_gla
    g0 = r0 + d_gla
    assert g0 + rank == w_in.shape[2] and d_ssm + d_gla == w_out.shape[1]

    x_p = x_prompt.reshape(tp, d)
    x_t = jnp.concatenate([x_sample.reshape(ns, d), meta_tokens,
                           jnp.zeros((TAIL_ROWS - ns - nm, d), F32)], axis=0)
    w_in_bf = w_in[0].astype(BF16)
    w_g1 = jnp.pad(w_in[0, :, g0:], ((0, 0), (0, LANES - rank))).astype(BF16)
    w_g2 = jnp.pad(gla_w_gate2[0], ((0, LANES - rank), (0, 0))).astype(BF16)
    lam_re = s5_lam_re[0].reshape(1, ngrp * nstate)
    lam_im = s5_lam_im[0].reshape(1, ngrp * nstate)
    logdt = jnp.repeat(s5_log_dt[0], nstate).reshape(1, ngrp * nstate)
    b_re4 = jnp.transpose(s5_b_re[0].reshape(nslab, gps, nstate, gch), (0, 1, 3, 2))
    b_im4 = jnp.transpose(s5_b_im[0].reshape(nslab, gps, nstate, gch), (0, 1, 3, 2))
    c_re4 = jnp.transpose(s5_c_re[0].reshape(nslab, gps, gch, nstate), (0, 1, 3, 2))
    c_im4 = jnp.transpose(s5_c_im[0].reshape(nslab, gps, gch, nstate), (0, 1, 3, 2))
    bre_bd = _block_diag(b_re4, gps)
    bim_bd = _block_diag(b_im4, gps)
    cre_bd = _block_diag(c_re4, gps)
    cim_bd = _block_diag(c_im4, gps)
    h0r = state_s5_re[0].reshape(ns, ngrp * nstate)
    h0i = state_s5_im[0].reshape(ns, ngrp * nstate)

    tm = _row_tile(t_all, (768, 512, 256))

    xn = _prenorm(x_p, x_t, norm_mix_g)
    proj, xg, u_bf = _in_proj(xn, w_in_bf, w_g1, peer_u[0], tm, PROJ_COLS, g0)

    y_raw, pr, pi_, sr, si = _s5(proj, h0r, h0i, lam_re, lam_im, logdt, bre_bd, bim_bd,
                                 cre_bd, cim_bd, s5_d, nb, seq, ns, nm)
    y_ssm = _glu_norm(y_raw, s5_w_glu[0].astype(BF16), s5_b_glu, s5_norm_g, TAIL_ROWS)

    og_p, gla_p = _gla_prompt(proj, xg, w_g2, gla_b_gate2, gla_norm_g, nb, seq, ns, nm,
                              nh, dk, dv, q0, k0, v0, r0)
    og_s, gla_s = _gla_sample(proj, xg, w_g2, gla_b_gate2, gla_norm_g, state_gla[0],
                              tp, ns, nh, dk, dv, q0, k0, v0, r0)
    og_t = jnp.concatenate([og_s.astype(BF16), jnp.zeros((TAIL_ROWS - ns, d_gla), BF16)], axis=0)

    w_o = w_out[0].astype(BF16)
    assert d_ssm == d_gla
    h = _out_proj(y_ssm, og_p, og_t, w_o, x_p, x_t, tm, PROJ_COLS)

    hn_t = _norm_t(h, norm_ffn_g, TAIL_ROWS)
    wq_t = _transpose_cast(peer_w_q[0], WT_ROWS, WT_COLS)
    s1, c1, s2, e2, tau, v_t = _route(wq_t, hn_t, peer_keys[0, :, 0], peer_keys[0, :, 1],
                                      peer_v[0], tm)
    peer_t = _peer(u_bf, v_t, hn_t, jnp.transpose(s1, (1, 0, 2)), jnp.transpose(c1, (1, 0, 2)),
                   s2, e2, tau, tm, EXPERT_BLOCK)
    y_p, y_t = _final(peer_t, h, norm_final_g.reshape(1, d), tp, TAIL_ROWS)

    y_prompt = y_p.reshape(nb, seq, d)
    y_sample = y_t[:ns].reshape(ns, 1, d)
    return (y_prompt, y_sample,
            pr.reshape(1, nb, ngrp, nstate), pi_.reshape(1, nb, ngrp, nstate), gla_p[None],
            sr.reshape(1, ns, ngrp, nstate), si.reshape(1, ns, ngrp, nstate), gla_s[None])
```
